```python
import jax, jax.numpy as jnp
from jax import lax
import numpy as np

D_MODEL = 1024
BATCH = 2
SEQ = 8192
DEPTH = 1
DEC_BATCH = 128
DEC_SEQ = 4
PAST_LEN = 8192
PAGE_SIZE = 128

D_MIX = D_MODEL
MLA_HEADS = 8
QK_NOPE_DIM = 64
QK_ROPE_DIM = 32
QK_HEAD_DIM = QK_NOPE_DIM + QK_ROPE_DIM
V_HEAD_DIM = 64
Q_LORA_RANK = 256
KV_LORA_RANK = 128
MLA_WIDTH = MLA_HEADS * V_HEAD_DIM
RET_HEADS = 8
RET_DK = 64
RET_DV = 64
RET_WIDTH = RET_HEADS * RET_DV
RET_CHUNK = 128
ROPE_THETA = 10000.0
QUERY_BLOCK = 128
IN_SIZES = (Q_LORA_RANK, KV_LORA_RANK, QK_ROPE_DIM, RET_HEADS * RET_DK, RET_HEADS * RET_DK, RET_HEADS * RET_DV, RET_WIDTH)
D_IN = sum(IN_SIZES)
PEER_KEYS = 128
PEER_EXPERTS = PEER_KEYS * PEER_KEYS
PEER_HEADS = 8
PEER_QDIM = 256
PEER_TOPK = 16
PEER_BLOCK = 256
EPS = 1e-6

kernel_name = "hymba_mla_retnet_peer_step"


def rms_norm(x, w):
    xf = x.astype(jnp.float32)
    y = xf * lax.rsqrt(jnp.mean(xf * xf, -1, keepdims=True) + EPS)
    return (y * w.astype(jnp.float32)).astype(x.dtype)


def rope(x, pos):
    half = x.shape[-1] // 2
    inv = ROPE_THETA ** (-jnp.arange(half, dtype=jnp.float32) / half)
    ang = pos.astype(jnp.float32)[:, None] * inv[None, :]
    cos = jnp.cos(ang)[:, None, :]
    sin = jnp.sin(ang)[:, None, :]
    x1, x2 = x[..., :half], x[..., half:]
    return jnp.concatenate([x1 * cos - x2 * sin, x1 * sin + x2 * cos], -1).astype(x.dtype)


def mixer_inputs(x, pos, attn_norm_w, w_in, q_a_norm_w, w_uq, q_norm_w, kv_a_norm_w):
    xn = rms_norm(x, attn_norm_w)
    offsets = [int(o) for o in np.cumsum(IN_SIZES)[:-1]]
    q_lat, c_kv, k_pe, rq, rk, rv, rg = jnp.split(xn @ w_in, offsets, axis=-1)
    lead = x.shape[:-1]
    q = (rms_norm(q_lat, q_a_norm_w) @ w_uq).reshape(lead + (MLA_HEADS, QK_HEAD_DIM))
    q = rms_norm(q, q_norm_w)
    q = jnp.concatenate([q[..., :QK_NOPE_DIM], rope(q[..., QK_NOPE_DIM:], pos)], -1)
    c_kv = rms_norm(c_kv, kv_a_norm_w)
    rq = rope(rq.reshape(lead + (RET_HEADS, RET_DK)), pos)
    rk = rope(rk.reshape(lead + (RET_HEADS, RET_DK)), pos) * (RET_DK ** -0.5)
    rv = rv.reshape(lead + (RET_HEADS, RET_DV))
    return q, c_kv, k_pe, rq, rk, rv, rg


def mla_kv(c_kv, k_pe, w_ukv, k_norm_w, pos):
    kv = (c_kv @ w_ukv).reshape(c_kv.shape[:-1] + (MLA_HEADS, QK_NOPE_DIM + V_HEAD_DIM))
    k_nope, v = kv[..., :QK_NOPE_DIM], kv[..., QK_NOPE_DIM:]
    k_pe_h = jnp.broadcast_to(k_pe[..., None, :], k_nope.shape[:-1] + (QK_ROPE_DIM,)).astype(k_nope.dtype)
    k = rms_norm(jnp.concatenate([k_nope, k_pe_h], -1), k_norm_w)
    k = jnp.concatenate([k[..., :QK_NOPE_DIM], rope(k[..., QK_NOPE_DIM:], pos)], -1)
    return k, v


def attend(q, k, v, q_pos, k_pos):
    s = jnp.einsum('...qhd,...khd->...hqk', q, k, preferred_element_type=jnp.float32) * (QK_HEAD_DIM ** -0.5)
    s = jnp.where(k_pos[None, :] <= q_pos[:, None], s, -jnp.inf)
    p = jax.nn.softmax(s, axis=-1).astype(v.dtype)
    return jnp.einsum('...hqk,...khd->...qhd', p, v)


def prompt_mla(q, c_kv, k_pe, pos, w_ukv, k_norm_w):
    b, t = q.shape[:2]
    k, v = mla_kv(c_kv, k_pe, w_ukv, k_norm_w, pos)
    nb = t // QUERY_BLOCK
    q_blocks = q.reshape(b, nb, QUERY_BLOCK, MLA_HEADS, QK_HEAD_DIM).swapaxes(0, 1)
    pos_blocks = pos.reshape(nb, QUERY_BLOCK)
    out = lax.map(lambda a: attend(a[0], k, v, a[1], pos), (q_blocks, pos_blocks))
    return out.swapaxes(0, 1).reshape(b, t, MLA_WIDTH)


def sample_mla(q, c_all, pe_all, q_pos, k_pos, w_ukv, k_norm_w):
    def one(a):
        q_s, c_s, p_s = a
        k, v = mla_kv(c_s, p_s, w_ukv, k_norm_w, k_pos)
        return attend(q_s, k, v, q_pos, k_pos)
    out = lax.map(one, (q, c_all, pe_all))
    return out.reshape(q.shape[0], q.shape[1], MLA_WIDTH)


def ret_log_decay():
    return jnp.log1p(-jnp.power(2.0, -5.0 - jnp.arange(RET_HEADS, dtype=jnp.float32)))


def retention_chunk(R, qkv):
    q, k, v = qkv
    q = q.astype(jnp.float32); k = k.astype(jnp.float32); v = v.astype(jnp.float32)
    c = q.shape[1]
    lg = ret_log_decay()
    i = jnp.arange(c, dtype=jnp.float32)
    diff = i[:, None] - i[None, :]
    dmat = jnp.where(diff >= 0, jnp.exp(lg[:, None, None] * jnp.maximum(diff, 0.0)), 0.0)
    inner = jnp.einsum('bihd,bjhd->bhij', q, k) * dmat
    o = jnp.einsum('bhij,bjhv->bihv', inner, v)
    cross = jnp.exp(lg[None, :] * (i[:, None] + 1.0))
    o = o + jnp.einsum('bihd,bhdv->bihv', q, R) * cross[None, :, :, None]
    k_dec = jnp.exp(lg[None, :] * (c - 1.0 - i[:, None]))
    R_new = jnp.exp(lg * c)[None, :, None, None] * R + jnp.einsum('bjhd,bjhv->bhdv', k * k_dec[None, :, :, None], v)
    return R_new, o


def retention(q, k, v, R0, chunk):
    b, t = q.shape[:2]
    nc = t // chunk
    def to_chunks(a):
        return a.reshape((b, nc, chunk) + a.shape[2:]).swapaxes(0, 1)
    R, o = lax.scan(retention_chunk, R0, (to_chunks(q), to_chunks(k), to_chunks(v)))
    return o.swapaxes(0, 1).reshape(b, t, RET_HEADS, RET_DV), R


def head_group_norm(o, w):
    mu = jnp.mean(o, -1, keepdims=True)
    var = jnp.mean(jnp.square(o - mu), -1, keepdims=True)
    y = (o - mu) * lax.rsqrt(var + EPS)
    return y.reshape(o.shape[:-2] + (RET_WIDTH,)) * w.astype(jnp.float32)


def peer_block(xb, w_peer_q, sub_keys1, sub_keys2, peer_u, peer_v):
    n = xb.shape[0]
    q = (xb @ w_peer_q).reshape(n, PEER_HEADS, 2, PEER_QDIM // 2)
    s1 = jnp.einsum('nhd,kd->nhk', q[:, :, 0], sub_keys1, preferred_element_type=jnp.float32)
    s2 = jnp.einsum('nhd,kd->nhk', q[:, :, 1], sub_keys2, preferred_element_type=jnp.float32)
    v1, i1 = lax.top_k(s1, PEER_TOPK)
    v2, i2 = lax.top_k(s2, PEER_TOPK)
    cand = (v1[..., :, None] + v2[..., None, :]).reshape(n, PEER_HEADS, PEER_TOPK * PEER_TOPK)
    vs, ic = lax.top_k(cand, PEER_TOPK)
    e1 = jnp.take_along_axis(i1, ic // PEER_TOPK, axis=-1)
    e2 = jnp.take_along_axis(i2, ic % PEER_TOPK, axis=-1)
    experts = e1 * PEER_KEYS + e2
    g = jax.nn.softmax(vs, axis=-1).astype(xb.dtype)
    act = jax.nn.gelu(jnp.einsum('nd,nhkd->nhk', xb, peer_u[experts]), approximate=False)
    return jnp.einsum('nhk,nhkd->nd', g * act, peer_v[experts])


def peer(x, w_peer_q, sub_keys1, sub_keys2, peer_u, peer_v):
    shape = x.shape
    xf = x.reshape(-1, D_MODEL)
    n = xf.shape[0]
    nb = -(-n // PEER_BLOCK)
    xp = jnp.pad(xf, ((0, nb * PEER_BLOCK - n), (0, 0))).reshape(nb, PEER_BLOCK, D_MODEL)
    out = lax.map(lambda b: peer_block(b, w_peer_q, sub_keys1, sub_keys2, peer_u, peer_v), xp)
    return out.reshape(nb * PEER_BLOCK, D_MODEL)[:n].reshape(shape)


def output_and_ffn(x, mla_out, ret_o, rg, ret_norm_w, w_o, ffn_norm_w, w_peer_q, sub_keys1, sub_keys2, peer_u, peer_v):
    ret_out = jax.nn.silu(rg) * head_group_norm(ret_o, ret_norm_w).astype(x.dtype)
    h = x + jnp.concatenate([mla_out.astype(x.dtype), ret_out], -1) @ w_o
    return h + peer(rms_norm(h, ffn_norm_w), w_peer_q, sub_keys1, sub_keys2, peer_u, peer_v)


def setup_inputs(seed: int = 0) -> dict:
    key = jax.random.key(seed)
    ks = jax.random.split(key, 24)
    n_pages = PAST_LEN // PAGE_SIZE
    n_used = DEC_BATCH * n_pages
    n_pool = n_used + n_used // 4
    def nrm(k, shape, scale):
        return jax.random.normal(k, shape, jnp.float32) * scale
    def gain(k, shape):
        return 1.0 + 0.02 * jax.random.normal(k, shape, jnp.float32)
    page_table = jax.random.permutation(ks[0], n_pool)[:n_used].reshape(DEC_BATCH, n_pages).astype(jnp.int32)
    return {
        'x_prompt': nrm(ks[1], (BATCH, SEQ, D_MODEL), 1.0),
        'x_sample': nrm(ks[2], (DEC_BATCH, DEC_SEQ, D_MODEL), 1.0),
        'cache_kv_latent': nrm(ks[3], (DEPTH, n_pool, PAGE_SIZE, KV_LORA_RANK), 1.0),
        'cache_k_rope': nrm(ks[4], (DEPTH, n_pool, PAGE_SIZE, QK_ROPE_DIM), 1.0),
        'state_ret': nrm(ks[5], (DEPTH, DEC_BATCH, RET_HEADS, RET_DK, RET_DV), 0.5),
        'page_table': page_table,
        'attn_norm_w': gain(ks[6], (DEPTH, D_MODEL)),
        'w_in': nrm(ks[7], (DEPTH, D_MODEL, D_IN), D_MODEL ** -0.5),
        'q_a_norm_w': gain(ks[8], (DEPTH, Q_LORA_RANK)),
        'w_uq': nrm(ks[9], (DEPTH, Q_LORA_RANK, MLA_HEADS * QK_HEAD_DIM), Q_LORA_RANK ** -0.5),
        'q_norm_w': gain(ks[10], (DEPTH, QK_HEAD_DIM)),
        'kv_a_norm_w': gain(ks[11], (DEPTH, KV_LORA_RANK)),
        'w_ukv': nrm(ks[12], (DEPTH, KV_LORA_RANK, MLA_HEADS * (QK_NOPE_DIM + V_HEAD_DIM)), KV_LORA_RANK ** -0.5),
        'k_norm_w': gain(ks[13], (DEPTH, QK_HEAD_DIM)),
        'ret_norm_w': gain(ks[14], (DEPTH, RET_WIDTH)),
        'w_o': nrm(ks[15], (DEPTH, D_MIX, D_MODEL), D_MIX ** -0.5),
        'ffn_norm_w': gain(ks[16], (DEPTH, D_MODEL)),
        'w_peer_q': nrm(ks[17], (DEPTH, D_MODEL, PEER_HEADS * PEER_QDIM), D_MODEL ** -0.5),
        'peer_sub_keys1': nrm(ks[18], (DEPTH, PEER_KEYS, PEER_QDIM // 2), (PEER_QDIM // 2) ** -0.5),
        'peer_sub_keys2': nrm(ks[19], (DEPTH, PEER_KEYS, PEER_QDIM // 2), (PEER_QDIM // 2) ** -0.5),
        'peer_u': nrm(ks[20], (DEPTH, PEER_EXPERTS, D_MODEL), D_MODEL ** -0.5),
        'peer_v': nrm(ks[21], (DEPTH, PEER_EXPERTS, D_MODEL), 0.3),
    }


def reference(x_prompt, x_sample, cache_kv_latent, cache_k_rope, state_ret, page_table,
              attn_norm_w, w_in, q_a_norm_w, w_uq, q_norm_w, kv_a_norm_w, w_ukv, k_norm_w,
              ret_norm_w, w_o, ffn_norm_w, w_peer_q, peer_sub_keys1, peer_sub_keys2, peer_u, peer_v):
    n_seq_s = x_sample.shape[0]
    pos_p = jnp.arange(SEQ)
    pos_s = PAST_LEN + jnp.arange(DEC_SEQ)
    pos_all = jnp.arange(PAST_LEN + DEC_SEQ)
    hp, hs = x_prompt, x_sample
    lat_p, pe_p, R_p, lat_s, pe_s, R_s = [], [], [], [], [], []
    for l in range(DEPTH):
        mix_w = (attn_norm_w[l], w_in[l], q_a_norm_w[l], w_uq[l], q_norm_w[l], kv_a_norm_w[l])
        out_w = (ret_norm_w[l], w_o[l], ffn_norm_w[l], w_peer_q[l], peer_sub_keys1[l], peer_sub_keys2[l], peer_u[l], peer_v[l])
        q, c_kv, k_pe, rq, rk, rv, rg = mixer_inputs(hp, pos_p, *mix_w)
        mla_out = prompt_mla(q, c_kv, k_pe, pos_p, w_ukv[l], k_norm_w[l])
        R0 = jnp.zeros((hp.shape[0], RET_HEADS, RET_DK, RET_DV), jnp.float32)
        ret_o, R_new = retention(rq, rk, rv, R0, RET_CHUNK)
        hp = output_and_ffn(hp, mla_out, ret_o, rg, *out_w)
        lat_p.append(c_kv); pe_p.append(k_pe); R_p.append(R_new)
        q, c_kv, k_pe, rq, rk, rv, rg = mixer_inputs(hs, pos_s, *mix_w)
        past_lat = cache_kv_latent[l][page_table].reshape(n_seq_s, PAST_LEN, KV_LORA_RANK)
        past_pe = cache_k_rope[l][page_table].reshape(n_seq_s, PAST_LEN, QK_ROPE_DIM)
        c_all = jnp.concatenate([past_lat.astype(c_kv.dtype), c_kv], axis=1)
        pe_all = jnp.concatenate([past_pe.astype(k_pe.dtype), k_pe], axis=1)
        mla_out = sample_mla(q, c_all, pe_all, pos_s, pos_all, w_ukv[l], k_norm_w[l])
        ret_o, R_new = retention(rq, rk, rv, state_ret[l].astype(jnp.float32), DEC_SEQ)
        hs = output_and_ffn(hs, mla_out, ret_o, rg, *out_w)
        lat_s.append(c_kv); pe_s.append(k_pe); R_s.append(R_new)
    return (hp, hs, jnp.stack(lat_p), jnp.stack(pe_p), jnp.stack(R_p), jnp.stack(lat_s), jnp.stack(pe_s), jnp.stack(R_s))
```

```python
import functools

import numpy as np
import jax
import jax.numpy as jnp
from jax import lax
from jax.experimental import pallas as pl
from jax.experimental.pallas import tpu as pltpu

F32 = jnp.float32
BF16 = jnp.bfloat16

D_MODEL = 1024
PAGE = 128
MLA_HEADS = 8
QK_NOPE = 64
QK_ROPE = 32
QK_HEAD = QK_NOPE + QK_ROPE
V_HEAD = 64
Q_LORA = 256
KV_LORA = 128
RET_HEADS = 8
RET_DK = 64
RET_DV = 64
RET_CHUNK = 128
ROPE_THETA = 10000.0
PEER_KEYS = 128
PEER_HEADS = 8
PEER_QDIM = 256
PEER_TOPK = 16
EPS = 1e-6
LANES = 128
HEAD_PAD = 128
NEG_INF = float("-inf")
INV_SQRT2 = 0.7071067811865476

MIX_TM = 256
ATT_T = 512
SAMP_PAGES = 16
SRET_SB = 8
TOPK_TT = 256
PEER_TN = 256
PEER_EB = 1024


def _cparams(sem, vmem_mb):
    return pltpu.CompilerParams(dimension_semantics=sem, vmem_limit_bytes=vmem_mb << 20)


def _full(shape):
    n = len(shape)
    return pl.BlockSpec(shape, lambda *_: (0,) * n)


def _rms(x, w, n):
    return x * lax.rsqrt(jnp.sum(x * x, -1, keepdims=True) * (1.0 / n) + EPS) * w


def _dot(a, b):
    return jnp.dot(a, b, preferred_element_type=F32)


def _dot_nt(a, b):
    return lax.dot_general(a, b, (((1,), (1,)), ((), ())), preferred_element_type=F32)


def _dot_tn(a, b):
    return lax.dot_general(a, b, (((0,), (0,)), ((), ())), preferred_element_type=F32)


def _mixer_body(x_ref, anw_ref, win_ref, qanw_ref, wuq_ref, qnw_ref, kvnw_ref, wukn_ref, wuv_ref, knw_ref,
                cq_ref, sq_ref, cr_ref, sr_ref,
                q_ref, k_ref, v_ref, ckv_ref, kpe_ref, rq_ref, rk_ref, rv_ref, rg_ref):
    x = x_ref[...]
    xn = _rms(x, anw_ref[...], D_MODEL)
    proj = _dot(xn.astype(BF16), win_ref[...])
    lane = lax.broadcasted_iota(jnp.int32, (x.shape[0], LANES), 1)
    cq, sq, cr, sr = cq_ref[...], sq_ref[...], cr_ref[...], sr_ref[...]
    lo_q = lane < QK_NOPE + QK_ROPE // 2
    lo_r = (lane & (RET_DK - 1)) < RET_DK // 2

    def rope_q(v):
        return v * cq + jnp.where(lo_q, pltpu.roll(v, LANES - 16, 1), pltpu.roll(v, 16, 1)) * sq

    def rope_r(v):
        return v * cr + jnp.where(lo_r, pltpu.roll(v, LANES - 32, 1), pltpu.roll(v, 32, 1)) * sr

    q_lat = _rms(proj[:, 0:256], qanw_ref[...], Q_LORA)
    qf = _dot(q_lat.astype(BF16), wuq_ref[...])
    ckv = _rms(proj[:, 256:384], kvnw_ref[...], KV_LORA)
    ckv_ref[...] = ckv
    kslot = proj[:, 384:512]
    kpe_ref[...] = kslot
    cb = ckv.astype(BF16)
    knf = _dot(cb, wukn_ref[...])
    v_ref[...] = _dot(cb, wuv_ref[...]).astype(BF16)
    qnw, knw = qnw_ref[...], knw_ref[...]
    for h in range(MLA_HEADS):
        sl = slice(HEAD_PAD * h, HEAD_PAD * (h + 1))
        q_ref[:, sl] = rope_q(_rms(qf[:, sl], qnw, QK_HEAD)).astype(BF16)
        k_ref[:, sl] = rope_q(_rms(knf[:, sl] + kslot, knw, QK_HEAD)).astype(BF16)
    for p in range(4):
        sl = slice(LANES * p, LANES * (p + 1))
        rq_ref[:, sl] = rope_r(proj[:, 512 + LANES * p:512 + LANES * (p + 1)]).astype(BF16)
        rk_ref[:, sl] = (rope_r(proj[:, 1024 + LANES * p:1024 + LANES * (p + 1)]) * (RET_DK ** -0.5)).astype(BF16)
    rv_ref[...] = proj[:, 1536:2048].astype(BF16)
    rg_ref[...] = proj[:, 2048:2560]


def _mixer(x_all, wts, tabs):
    n = x_all.shape[0]
    tm = MIX_TM
    row = lambda w: pl.BlockSpec((tm, w), lambda i: (i, 0))
    ins = [row(D_MODEL), _full((1, D_MODEL)), _full((D_MODEL, 2560)), _full((1, Q_LORA)), _full((Q_LORA, 1024)),
           _full((1, LANES)), _full((1, KV_LORA)), _full((KV_LORA, 1024)), _full((KV_LORA, 512)), _full((1, LANES)),
           row(LANES), row(LANES), row(LANES), row(LANES)]
    outs = [row(1024), row(1024), row(512), row(KV_LORA), row(LANES), row(512), row(512), row(512), row(512)]
    shapes = [jax.ShapeDtypeStruct((n, 1024), BF16), jax.ShapeDtypeStruct((n, 1024), BF16),
              jax.ShapeDtypeStruct((n, 512), BF16), jax.ShapeDtypeStruct((n, KV_LORA), F32),
              jax.ShapeDtypeStruct((n, LANES), F32), jax.ShapeDtypeStruct((n, 512), BF16),
              jax.ShapeDtypeStruct((n, 512), BF16), jax.ShapeDtypeStruct((n, 512), BF16),
              jax.ShapeDtypeStruct((n, 512), F32)]
    return pl.pallas_call(
        _mixer_body, grid=(n // tm,), in_specs=ins, out_specs=outs, out_shape=shapes,
        compiler_params=_cparams(("arbitrary",), 48), name="mixer",
    )(x_all, wts["anw"], wts["win"], wts["qanw"], wts["wuq"], wts["qnw"], wts["kvnw"], wts["wukn"], wts["wuv"],
      wts["knw"], tabs["cq"], tabs["sq"], tabs["cr"], tabs["sr"])


def _pattn_body(q_ref, k_ref, v_ref, o_ref, m_sc, l_sc, acc_sc):
    qi = pl.program_id(2)
    t = ATT_T
    row = lax.broadcasted_iota(jnp.int32, (t, t), 0)
    col = lax.broadcasted_iota(jnp.int32, (t, t), 1)
    causal = col <= row
    outs = []
    for hh in range(2):
        lsl = slice(HEAD_PAD * hh, HEAD_PAD * (hh + 1))
        q = q_ref[:, lsl]
        m_sc[...] = jnp.full(m_sc.shape, NEG_INF, F32)
        l_sc[...] = jnp.zeros(l_sc.shape, F32)
        acc_sc[...] = jnp.zeros(acc_sc.shape, F32)

        def update(j, masked):
            start = pl.multiple_of(j * t, t)
            kb = k_ref[pl.ds(start, t), lsl]
            vb = v_ref[pl.ds(start, t), :]
            s = _dot_nt(q, kb)
            if masked:
                s = jnp.where(causal, s, NEG_INF)
            m_prev = m_sc[...]
            m_new = jnp.maximum(m_prev, jnp.max(s, -1, keepdims=True))
            alpha = jnp.exp(m_prev - m_new)
            p = jnp.exp(s - m_new)
            l_sc[...] = alpha * l_sc[...] + jnp.sum(p, -1, keepdims=True)
            acc_sc[...] = alpha * acc_sc[...] + _dot(p.astype(BF16), vb)
            m_sc[...] = m_new

        def body(j, c):
            update(j, False)
            return c

        lax.fori_loop(0, qi, body, 0)
        update(qi, True)
        outs.append(acc_sc[...] / l_sc[...])
    lane = lax.broadcasted_iota(jnp.int32, (t, LANES), 1)
    o_ref[...] = jnp.where(lane < V_HEAD, outs[0], outs[1]).astype(o_ref.dtype)


def _pattn(q_all, k_all, v_all, batch, seq):
    t = ATT_T
    nq = seq // t
    return pl.pallas_call(
        _pattn_body, grid=(batch, MLA_HEADS // 2, nq),
        in_specs=[pl.BlockSpec((t, 2 * HEAD_PAD), lambda b, hp, qi: (b * nq + qi, hp)),
                  pl.BlockSpec((seq, 2 * HEAD_PAD), lambda b, hp, qi: (b, hp)),
                  pl.BlockSpec((seq, LANES), lambda b, hp, qi: (b, hp))],
        out_specs=pl.BlockSpec((t, LANES), lambda b, hp, qi: (b * nq + qi, hp)),
        out_shape=jax.ShapeDtypeStruct((batch * seq, MLA_HEADS * V_HEAD), BF16),
        scratch_shapes=[pltpu.VMEM((t, 1), F32), pltpu.VMEM((t, 1), F32), pltpu.VMEM((t, LANES), F32)],
        compiler_params=_cparams(("arbitrary", "arbitrary", "arbitrary"), 48), name="pattn",
    )(q_all, k_all, v_all)


def _ret_log_decay():
    return jnp.log1p(-jnp.power(2.0, -5.0 - jnp.arange(RET_HEADS, dtype=F32)))


def _ret_tables(c):
    lg = _ret_log_decay()
    i = jnp.arange(c, dtype=F32)
    diff = i[:, None] - i[None, :]
    dmat = jnp.where(diff >= 0, jnp.exp(lg[:, None, None] * jnp.maximum(diff, 0.0)), 0.0)
    lane_head = lambda a: jnp.repeat(a.reshape(a.shape[0], 4, 2), RET_DV, axis=2).transpose(1, 0, 2)
    cross = lane_head(jnp.exp(lg[None, :] * (i[:, None] + 1.0)))
    kdec = lane_head(jnp.exp(lg[None, :] * (c - 1.0 - i[:, None])))
    gc = lane_head(jnp.exp(lg * c)[None, :])
    return dmat[0::2], dmat[1::2], cross, kdec, gc


def _pret_body(rq_ref, rk_ref, rv_ref, de_ref, do_ref, cross_ref, kdec_ref, gc_ref, o_ref, rout_ref, r_sc):
    c = pl.program_id(1)

    @pl.when(c == 0)
    def _():
        r_sc[...] = jnp.zeros(r_sc.shape, F32)

    n = rq_ref.shape[0]
    lane = lax.broadcasted_iota(jnp.int32, (n, LANES), 1)
    even = lane < RET_DK
    rr = lax.broadcasted_iota(jnp.int32, (LANES, LANES), 0)
    cc = lax.broadcasted_iota(jnp.int32, (LANES, LANES), 1)
    same_head = (rr < RET_DK) == (cc < RET_DV)
    for p in range(4):
        sl = slice(LANES * p, LANES * (p + 1))
        q, k, v = rq_ref[:, sl], rk_ref[:, sl], rv_ref[:, sl]
        zero = jnp.zeros_like(q)
        se = _dot_nt(jnp.where(even, q, zero), k)
        so = _dot_nt(jnp.where(even, zero, q), k)
        oe = _dot((se * de_ref[p]).astype(BF16), v)
        oo = _dot((so * do_ref[p]).astype(BF16), v)
        r = r_sc[p]
        cross = _dot(q, r.astype(BF16)) * cross_ref[p]
        o_ref[:, sl] = jnp.where(even, oe, oo) + cross
        kd = (k.astype(F32) * kdec_ref[p]).astype(BF16)
        upd = _dot_tn(kd, v)
        r_sc[p] = r * gc_ref[p] + jnp.where(same_head, upd, 0.0)

    @pl.when(c == pl.num_programs(1) - 1)
    def _():
        rout_ref[0] = r_sc[...]


def _pret(rq, rk, rv, batch, seq):
    c = RET_CHUNK
    nc = seq // c
    tabs = _ret_tables(c)
    blk = pl.BlockSpec((c, 512), lambda b, i: (b * nc + i, 0))
    return pl.pallas_call(
        _pret_body, grid=(batch, nc),
        in_specs=[blk, blk, blk, _full((4, c, c)), _full((4, c, c)), _full((4, c, LANES)), _full((4, c, LANES)),
                  _full((4, 1, LANES))],
        out_specs=[blk, pl.BlockSpec((1, 4, LANES, LANES), lambda b, i: (b, 0, 0, 0))],
        out_shape=[jax.ShapeDtypeStruct((batch * seq, 512), F32),
                   jax.ShapeDtypeStruct((batch, 4, LANES, LANES), F32)],
        scratch_shapes=[pltpu.VMEM((4, LANES, LANES), F32)],
        compiler_params=_cparams(("arbitrary", "arbitrary"), 32), name="pret",
    )(rq, rk, rv, *tabs)


def _sabs_body(q_ref, knw_ref, wpe_ref, wknt_ref, qa_ref, qc_ref, qs_ref):
    q = q_ref[...].astype(F32)
    qa_ref[...] = _dot((q * knw_ref[...]).astype(BF16), wknt_ref[0]).astype(BF16)
    lane = lax.broadcasted_iota(jnp.int32, q.shape, 1)
    qpe = pltpu.roll(q, LANES - QK_NOPE, 1)
    half = QK_ROPE // 2
    swapped = jnp.where(lane < half, pltpu.roll(qpe, LANES - half, 1), -pltpu.roll(qpe, half, 1))
    keep = lane < QK_ROPE
    wpe = wpe_ref[...]
    qc_ref[...] = jnp.where(keep, qpe * wpe, 0.0).astype(BF16)
    qs_ref[...] = jnp.where(keep, swapped * wpe, 0.0).astype(BF16)


def _sabs(q_s, knw, wpe, wknt):
    n = q_s.shape[0]
    blk = pl.BlockSpec((n, HEAD_PAD), lambda h: (0, h))
    shp = jax.ShapeDtypeStruct((n, MLA_HEADS * HEAD_PAD), BF16)
    return pl.pallas_call(
        _sabs_body, grid=(MLA_HEADS,),
        in_specs=[blk, _full((1, LANES)), _full((1, LANES)), pl.BlockSpec((1, LANES, KV_LORA), lambda h: (h, 0, 0))],
        out_specs=[blk, blk, blk], out_shape=[shp, shp, shp],
        compiler_params=_cparams(("arbitrary",), 32), name="sabs",
    )(q_s, knw, wpe, wknt)


def _sattn_body(pt_ref, qa_ref, qc_ref, qs_ref, wkn_ref, g_ref, cosd_ref, sind_ref,
                cnew_ref, pnew_ref, cosn_ref, sinn_ref, *rest, n_pages, dec_seq):
    lat_refs = rest[:n_pages]
    pe_refs = rest[n_pages:2 * n_pages]
    pc_ref = rest[2 * n_pages]
    cbuf, xc, xs, x2, m_sc, l_sc, acc_sc = rest[2 * n_pages + 1:]
    s_id = pl.program_id(0)
    j = pl.program_id(1)
    nrow = qa_ref.shape[1]

    @pl.when(jnp.logical_and(s_id == 0, j == 0))
    def _():
        xc[...] = jnp.zeros(xc.shape, BF16)
        xs[...] = jnp.zeros(xs.shape, BF16)
        x2[...] = jnp.zeros(x2.shape, BF16)

    @pl.when(j == 0)
    def _():
        m_sc[...] = jnp.full(m_sc.shape, NEG_INF, F32)
        l_sc[...] = jnp.zeros(l_sc.shape, F32)
        acc_sc[...] = jnp.zeros(acc_sc.shape, F32)

    qa, qc, qs = qa_ref[0], qc_ref[0], qs_ref[0]
    wkn, g = wkn_ref[...], g_ref[...]
    ones = jnp.ones((nrow, LANES), BF16)

    def attend(nk, mask):
        c = cbuf[0:nk, :]
        kn = _dot(c, wkn)
        ss = _dot_nt(g, (kn * kn).astype(BF16)) + _dot_nt(ones, x2[0:nk, :])
        rn = lax.rsqrt(ss * (1.0 / QK_HEAD) + EPS)
        s = (_dot_nt(qa, c) + _dot_nt(qc, xc[0:nk, :]) + _dot_nt(qs, xs[0:nk, :])) * rn
        if mask is not None:
            s = jnp.where(mask, s, NEG_INF)
        m_prev = m_sc[...]
        m_new = jnp.maximum(m_prev, jnp.max(s, -1, keepdims=True))
        alpha = jnp.exp(m_prev - m_new)
        p = jnp.exp(s - m_new)
        l_sc[...] = alpha * l_sc[...] + jnp.sum(p, -1, keepdims=True)
        acc_sc[...] = alpha * acc_sc[...] + _dot(p.astype(BF16), c)
        m_sc[...] = m_new

    def stage(i, lat, pe, cos, sin):
        rows = slice(PAGE * i, PAGE * (i + 1))
        cbuf[rows, :] = lat.astype(BF16)
        xc[rows, 0:QK_ROPE] = (pe * cos).astype(BF16)
        xs[rows, 0:QK_ROPE] = (pe * sin).astype(BF16)
        x2[rows, 0:QK_ROPE] = (pe * pe).astype(BF16)

    for i in range(n_pages):
        rows = slice(PAGE * i, PAGE * (i + 1))
        stage(i, lat_refs[i][0], pe_refs[i][0], cosd_ref[rows, :], sind_ref[rows, :])
    attend(PAGE * n_pages, None)

    @pl.when(j == pl.num_programs(1) - 1)
    def _():
        stage(0, cnew_ref[0], pnew_ref[0], cosn_ref[...], sinn_ref[...])
        qrow = lax.broadcasted_iota(jnp.int32, (nrow, PAGE), 0) % dec_seq
        kcol = lax.broadcasted_iota(jnp.int32, (nrow, PAGE), 1)
        attend(PAGE, jnp.logical_and(kcol <= qrow, kcol < dec_seq))
        pc_ref[0] = acc_sc[...] / l_sc[...]


def _sattn(page_table, qa, qc, qs, wkn, g, cosd, sind, cnew, pnew, cosn, sinn, cache_lat, cache_pe, dec_seq):
    n_seq, nrow, _ = qa.shape
    n_pages_total = page_table.shape[1]
    npg = SAMP_PAGES
    steps = n_pages_total // npg
    qblk = pl.BlockSpec((1, nrow, LANES), lambda s, j, pt: (s, 0, 0))
    const2 = lambda shape: pl.BlockSpec(shape, lambda s, j, pt: (0, 0))
    tab = pl.BlockSpec((npg * PAGE, QK_ROPE), lambda s, j, pt: (j, 0))
    in_specs = [qblk, qblk, qblk, const2((KV_LORA, 512)), const2((nrow, 512)), tab, tab,
                pl.BlockSpec((1, PAGE, KV_LORA), lambda s, j, pt: (s, 0, 0)),
                pl.BlockSpec((1, PAGE, QK_ROPE), lambda s, j, pt: (s, 0, 0)),
                const2((PAGE, QK_ROPE)), const2((PAGE, QK_ROPE))]
    for i in range(npg):
        in_specs.append(pl.BlockSpec((1, PAGE, KV_LORA), lambda s, j, pt, i=i: (pt[s, j * npg + i], 0, 0)))
    for i in range(npg):
        in_specs.append(pl.BlockSpec((1, PAGE, QK_ROPE), lambda s, j, pt, i=i: (pt[s, j * npg + i], 0, 0)))
    tk = npg * PAGE
    gs = pltpu.PrefetchScalarGridSpec(
        num_scalar_prefetch=1, grid=(n_seq, steps), in_specs=in_specs,
        out_specs=pl.BlockSpec((1, nrow, LANES), lambda s, j, pt: (s, 0, 0)),
        scratch_shapes=[pltpu.VMEM((tk, LANES), BF16), pltpu.VMEM((tk, LANES), BF16), pltpu.VMEM((tk, LANES), BF16),
                        pltpu.VMEM((tk, LANES), BF16), pltpu.VMEM((nrow, 1), F32), pltpu.VMEM((nrow, 1), F32),
                        pltpu.VMEM((nrow, LANES), F32)])
    return pl.pallas_call(
        functools.partial(_sattn_body, n_pages=npg, dec_seq=dec_seq), grid_spec=gs,
        out_shape=jax.ShapeDtypeStruct((n_seq, nrow, LANES), F32),
        compiler_params=_cparams(("arbitrary", "arbitrary"), 48), name="sattn",
    )(page_table, qa, qc, qs, wkn, g, cosd, sind, cnew, pnew, cosn, sinn,
      *([cache_lat] * npg), *([cache_pe] * npg))


def _sup_body(pc_ref, we_ref, wo_ref, o_ref):
    o_ref[...] = (_dot(pc_ref[0].astype(BF16), we_ref[0]) + _dot(pc_ref[1].astype(BF16), wo_ref[0])).astype(o_ref.dtype)


def _sup(pc_h, wve, wvo):
    n = pc_h.shape[1]
    return pl.pallas_call(
        _sup_body, grid=(MLA_HEADS // 2,),
        in_specs=[pl.BlockSpec((2, n, KV_LORA), lambda p: (p, 0, 0)),
                  pl.BlockSpec((1, KV_LORA, LANES), lambda p: (p, 0, 0)),
                  pl.BlockSpec((1, KV_LORA, LANES), lambda p: (p, 0, 0))],
        out_specs=pl.BlockSpec((n, LANES), lambda p: (0, p)),
        out_shape=jax.ShapeDtypeStruct((n, MLA_HEADS * V_HEAD), BF16),
        compiler_params=_cparams(("arbitrary",), 32), name="sup",
    )(pc_h, wve, wvo)


def _sret_body(rq_ref, rk_ref, rv_ref, r0_ref, d_ref, cross_ref, kdec_ref, g_ref, o_ref, r_ref):
    def one(si, carry):
        q8, k8, v8 = rq_ref[si], rk_ref[si], rv_ref[si]
        for h in range(RET_HEADS):
            sl = slice(RET_DK * h, RET_DK * (h + 1))
            q, k, v = q8[:, sl], k8[:, sl], v8[:, sl]
            r = r0_ref[si, h]
            inner = _dot_nt(q, k) * d_ref[h]
            o_ref[si, :, sl] = _dot(inner, v) + _dot(q, r) * cross_ref[h]
            r_ref[si, h] = g_ref[h] * r + _dot_tn(k * kdec_ref[h], v)
        return carry

    lax.fori_loop(0, rq_ref.shape[0], one, 0)


def _sret(rq, rk, rv, r0, dec_seq):
    n_seq = rq.shape[0]
    sb = SRET_SB
    lg = _ret_log_decay()
    i = jnp.arange(8, dtype=F32)
    valid = i < dec_seq
    diff = i[:, None] - i[None, :]
    dmat = jnp.where((diff >= 0) & valid[:, None] & valid[None, :],
                     jnp.exp(lg[:, None, None] * jnp.maximum(diff, 0.0)), 0.0)
    cross = jnp.exp(lg[:, None, None] * (i[None, :, None] + 1.0)) * jnp.ones((1, 1, RET_DV), F32)
    kdec = jnp.where(valid[None, :, None], jnp.exp(lg[:, None, None] * (dec_seq - 1.0 - i[None, :, None])), 0.0)
    kdec = kdec * jnp.ones((1, 1, RET_DK), F32)
    gdec = jnp.exp(lg * dec_seq)[:, None, None] * jnp.ones((1, RET_DK, RET_DV), F32)
    blk = pl.BlockSpec((sb, 8, 512), lambda s: (s, 0, 0))
    rblk = pl.BlockSpec((sb, RET_HEADS, RET_DK, RET_DV), lambda s: (s, 0, 0, 0))
    return pl.pallas_call(
        _sret_body, grid=(n_seq // sb,),
        in_specs=[blk, blk, blk, rblk, _full((RET_HEADS, 8, 8)), _full((RET_HEADS, 8, RET_DV)),
                  _full((RET_HEADS, 8, RET_DK)), _full((RET_HEADS, RET_DK, RET_DV))],
        out_specs=[blk, rblk],
        out_shape=[jax.ShapeDtypeStruct((n_seq, 8, 512), F32),
                   jax.ShapeDtypeStruct((n_seq, RET_HEADS, RET_DK, RET_DV), F32)],
        compiler_params=_cparams(("arbitrary",), 32), name="sret",
    )(rq, rk, rv, r0, dmat, cross, kdec, gdec)


def _post_body(x_ref, mla_ref, reto_ref, rg_ref, rnw_ref, wo1_ref, wo2_ref, fnw_ref, wpq_ref, k1_ref, k2_ref,
               h_ref, hn_ref, s1_ref, s2_ref):
    tm = x_ref.shape[0]
    lane = lax.broadcasted_iota(jnp.int32, (tm, LANES), 1)
    lo = lane < RET_DV

    def group(a):
        s_lo = jnp.sum(jnp.where(lo, a, 0.0), -1, keepdims=True)
        s_hi = jnp.sum(jnp.where(lo, 0.0, a), -1, keepdims=True)
        return jnp.where(lo, s_lo, s_hi)

    parts = []
    for p in range(4):
        sl = slice(LANES * p, LANES * (p + 1))
        o = reto_ref[:, sl]
        d = o - group(o) * (1.0 / RET_DV)
        y = d * lax.rsqrt(group(d * d) * (1.0 / RET_DV) + EPS) * rnw_ref[:, sl]
        rg = rg_ref[:, sl]
        parts.append((rg * jax.nn.sigmoid(rg) * y).astype(BF16))
    ret_out = jnp.concatenate(parts, axis=1)
    h = x_ref[...] + _dot(mla_ref[...], wo1_ref[...]) + _dot(ret_out, wo2_ref[...])
    h_ref[...] = h
    hn = _rms(h, fnw_ref[...], D_MODEL).astype(BF16)
    hn_ref[...] = hn
    pq = _dot(hn, wpq_ref[...])
    half = PEER_QDIM // 2
    for hd in range(PEER_HEADS):
        q1 = pq[:, PEER_QDIM * hd:PEER_QDIM * hd + half].astype(BF16)
        q2 = pq[:, PEER_QDIM * hd + half:PEER_QDIM * (hd + 1)].astype(BF16)
        s1_ref[hd] = _dot_nt(k1_ref[...], q1)
        s2_ref[hd] = _dot_nt(k2_ref[...], q2)


def _post(x_all, mla, reto, rg, wts):
    n = x_all.shape[0]
    tm = MIX_TM
    row = lambda w: pl.BlockSpec((tm, w), lambda i: (i, 0))
    sblk = pl.BlockSpec((PEER_HEADS, PEER_KEYS, tm), lambda i: (0, 0, i))
    half = PEER_QDIM // 2
    return pl.pallas_call(
        _post_body, grid=(n // tm,),
        in_specs=[row(D_MODEL), row(512), row(512), row(512), _full((1, 512)), _full((512, D_MODEL)),
                  _full((512, D_MODEL)), _full((1, D_MODEL)), _full((D_MODEL, PEER_HEADS * PEER_QDIM)),
                  _full((PEER_KEYS, half)), _full((PEER_KEYS, half))],
        out_specs=[row(D_MODEL), row(D_MODEL), sblk, sblk],
        out_shape=[jax.ShapeDtypeStruct((n, D_MODEL), F32), jax.ShapeDtypeStruct((n, D_MODEL), BF16),
                   jax.ShapeDtypeStruct((PEER_HEADS, PEER_KEYS, n), F32),
                   jax.ShapeDtypeStruct((PEER_HEADS, PEER_KEYS, n), F32)],
        compiler_params=_cparams(("arbitrary",), 48), name="post",
    )(x_all, mla, reto, rg, wts["rnw"], wts["wo1"], wts["wo2"], wts["fnw"], wts["wpq"], wts["k1"], wts["k2"])


def _topk_body(s1_ref, s2_ref, e1_ref, e2_ref, g_ref):
    kk = PEER_TOPK
    shift = kk.bit_length() - 1
    assert kk == 1 << shift
    tt = s1_ref.shape[2]
    row16 = lax.broadcasted_iota(jnp.int32, (kk, tt), 0)

    def take_top(x):
        nrow = x.shape[0]
        row = lax.broadcasted_iota(jnp.int32, x.shape, 0)
        vals = jnp.zeros((kk, tt), F32)
        idxs = jnp.zeros((kk, tt), jnp.int32)
        for r in range(kk):
            m = jnp.max(x, axis=0, keepdims=True)
            idx = jnp.min(jnp.where(x == m, row, nrow), axis=0, keepdims=True)
            vals = jnp.where(row16 == r, m, vals)
            idxs = jnp.where(row16 == r, idx, idxs)
            x = jnp.where(row == idx, NEG_INF, x)
        return vals, idxs

    def pick(table, sel):
        return jnp.sum(jnp.where(row16 == sel, table, 0), axis=0, keepdims=True)

    v1, i1 = take_top(s1_ref[0])
    v2, i2 = take_top(s2_ref[0])
    cand = jnp.concatenate([v1[a:a + 1, :] + v2 for a in range(kk)], axis=0)
    vs, ic = take_top(cand)
    e1 = jnp.zeros((kk, tt), jnp.int32)
    e2 = jnp.zeros((kk, tt), jnp.int32)
    for r in range(kk):
        f = ic[r:r + 1, :]
        e1 = jnp.where(row16 == r, pick(i1, lax.shift_right_logical(f, shift)), e1)
        e2 = jnp.where(row16 == r, pick(i2, f & (kk - 1)), e2)
    p = jnp.exp(vs - jnp.max(vs, axis=0, keepdims=True))
    g_ref[...] = p / jnp.sum(p, axis=0, keepdims=True)
    e1_ref[...] = e1.astype(F32)
    e2_ref[...] = e2.astype(F32)


def _topk(s1t, s2t):
    n = s1t.shape[2]
    tt = TOPK_TT
    sblk = pl.BlockSpec((1, PEER_KEYS, tt), lambda i, h: (h, 0, i))
    oblk = pl.BlockSpec((PEER_TOPK, tt), lambda i, h: (h, i))
    shp = jax.ShapeDtypeStruct((PEER_HEADS * PEER_TOPK, n), F32)
    return pl.pallas_call(
        _topk_body, grid=(n // tt, PEER_HEADS), in_specs=[sblk, sblk], out_specs=[oblk, oblk, oblk],
        out_shape=[shp, shp, shp],
        compiler_params=_cparams(("arbitrary", "arbitrary"), 32), name="topk",
    )(s1t, s2t)


def _peer_body(e1_ref, e2_ref, g_ref, hn_ref, h_ref, ut_ref, v_ref, o_ref, e1_sc, e2_sc, g_sc, w_sc, acc_sc):
    eb = pl.program_id(1)
    tn = hn_ref.shape[0]

    @pl.when(eb == 0)
    def _():
        e1_sc[...] = e1_ref[...].T
        e2_sc[...] = e2_ref[...].T
        g_sc[...] = g_ref[...].T
        acc_sc[...] = jnp.zeros(acc_sc.shape, F32)
        iot = lax.broadcasted_iota(jnp.int32, (PEER_KEYS, LANES), 0).astype(F32)

        def body(t, c):
            a = jnp.where(e1_sc[pl.ds(t, 1), :] == iot, 1.0, 0.0).astype(BF16)
            b = jnp.where(e2_sc[pl.ds(t, 1), :] == iot, g_sc[pl.ds(t, 1), :], 0.0).astype(BF16)
            w_sc[pl.ds(pl.multiple_of(t * PEER_KEYS, PEER_KEYS), PEER_KEYS), :] = _dot_nt(a, b)
            return c

        lax.fori_loop(0, tn, body, 0)

    act = _dot(hn_ref[...], ut_ref[...])
    gl = 0.5 * act * (1.0 + lax.erf(act * INV_SQRT2))
    zs = []
    for c in range(PEER_EB // PEER_KEYS):
        wsl = w_sc[pl.ds(eb * (PEER_EB // PEER_KEYS) + c, tn, stride=PEER_KEYS), :]
        zs.append((gl[:, LANES * c:LANES * (c + 1)] * wsl).astype(BF16))
    acc_sc[...] += _dot(jnp.concatenate(zs, axis=1), v_ref[...])

    @pl.when(eb == pl.num_programs(1) - 1)
    def _():
        o_ref[...] = h_ref[...] + acc_sc[...]


def _peer(e1t, e2t, gt, hn, h, ut, vt):
    n = hn.shape[0]
    tn = PEER_TN
    n_exp = ut.shape[1]
    jn = PEER_HEADS * PEER_TOPK
    sel = pl.BlockSpec((jn, tn), lambda i, e: (0, i))
    row = pl.BlockSpec((tn, D_MODEL), lambda i, e: (i, 0))
    return pl.pallas_call(
        _peer_body, grid=(n // tn, n_exp // PEER_EB),
        in_specs=[sel, sel, sel, row, row, pl.BlockSpec((D_MODEL, PEER_EB), lambda i, e: (0, e)),
                  pl.BlockSpec((PEER_EB, D_MODEL), lambda i, e: (e, 0))],
        out_specs=row, out_shape=jax.ShapeDtypeStruct((n, D_MODEL), F32),
        scratch_shapes=[pltpu.VMEM((tn, jn), F32), pltpu.VMEM((tn, jn), F32), pltpu.VMEM((tn, jn), F32),
                        pltpu.VMEM((tn * PEER_KEYS, LANES), F32), pltpu.VMEM((tn, D_MODEL), F32)],
        compiler_params=_cparams(("arbitrary", "arbitrary"), 56), name="peer",
    )(e1t, e2t, gt, hn, h, ut, vt)


def _pad_last(a, n):
    return jnp.pad(a, [(0, 0)] * (a.ndim - 1) + [(0, n - a.shape[-1])])


def _layer_weights(attn_norm_w, w_in, q_a_norm_w, w_uq, q_norm_w, kv_a_norm_w, w_ukv, k_norm_w,
                   ret_norm_w, w_o, ffn_norm_w, w_peer_q, sub_keys1, sub_keys2, peer_u, peer_v):
    o = np.cumsum([0, Q_LORA, KV_LORA, QK_ROPE, 512, 512, 512, 512])
    zeros = lambda w: jnp.zeros((D_MODEL, w), F32)
    win = jnp.concatenate([w_in[:, o[0]:o[2]], zeros(QK_NOPE), w_in[:, o[2]:o[3]], zeros(LANES - QK_HEAD),
                           w_in[:, o[3]:o[7]]], axis=1)
    wuq = _pad_last(w_uq.reshape(Q_LORA, MLA_HEADS, QK_HEAD), HEAD_PAD).reshape(Q_LORA, MLA_HEADS * HEAD_PAD)
    ukv = w_ukv.reshape(KV_LORA, MLA_HEADS, QK_NOPE + V_HEAD)
    ukn, uv = ukv[:, :, :QK_NOPE], ukv[:, :, QK_NOPE:]
    knw = _pad_last(k_norm_w, LANES)[None]
    wknt = _pad_last(ukn, HEAD_PAD).transpose(1, 2, 0)
    uvp = uv.reshape(KV_LORA, 4, 2, V_HEAD)
    zv = jnp.zeros((KV_LORA, 4, V_HEAD), F32)
    wve = jnp.concatenate([uvp[:, :, 0], zv], -1).transpose(1, 0, 2)
    wvo = jnp.concatenate([zv, uvp[:, :, 1]], -1).transpose(1, 0, 2)
    return dict(
        anw=attn_norm_w[None], win=win.astype(BF16), qanw=q_a_norm_w[None], wuq=wuq.astype(BF16),
        qnw=(_pad_last(q_norm_w, LANES) * (QK_HEAD ** -0.5))[None], kvnw=kv_a_norm_w[None],
        wukn=_pad_last(ukn, HEAD_PAD).reshape(KV_LORA, MLA_HEADS * HEAD_PAD).astype(BF16),
        wuv=uv.reshape(KV_LORA, MLA_HEADS * V_HEAD).astype(BF16), knw=knw,
        wpe=_pad_last(k_norm_w[QK_NOPE:], LANES)[None], wknt=wknt.astype(BF16),
        wkn=ukn.reshape(KV_LORA, MLA_HEADS * QK_NOPE).astype(BF16), wve=wve.astype(BF16), wvo=wvo.astype(BF16),
        rnw=ret_norm_w[None], wo1=w_o[:512].astype(BF16), wo2=w_o[512:].astype(BF16), fnw=ffn_norm_w[None],
        wpq=w_peer_q.astype(BF16), k1=sub_keys1.astype(BF16), k2=sub_keys2.astype(BF16),
        ut=peer_u.T.astype(BF16), vt=peer_v.astype(BF16))


def _rope_tables(pos):
    posf = pos.astype(F32)[:, None]
    n = pos.shape[0]
    hq = QK_ROPE // 2
    ang = posf * (ROPE_THETA ** (-jnp.arange(hq, dtype=F32) / hq))[None, :]
    c, s = jnp.cos(ang), jnp.sin(ang)
    one, zero = jnp.ones((n, QK_NOPE), F32), jnp.zeros((n, QK_NOPE), F32)
    cq = jnp.concatenate([one, c, c, one[:, :LANES - QK_HEAD]], 1)
    sq = jnp.concatenate([zero, -s, s, zero[:, :LANES - QK_HEAD]], 1)
    hr = RET_DK // 2
    angr = posf * (ROPE_THETA ** (-jnp.arange(hr, dtype=F32) / hr))[None, :]
    c2, s2 = jnp.cos(angr), jnp.sin(angr)
    cr = jnp.concatenate([c2, c2, c2, c2], 1)
    sr = jnp.concatenate([-s2, s2, -s2, s2], 1)
    return dict(cq=cq, sq=sq, cr=cr, sr=sr, cosd=jnp.concatenate([c, c], 1), sind=jnp.concatenate([s, s], 1))


def _to_rows(a, n_seq, dec_seq):
    return a.reshape(n_seq, dec_seq, MLA_HEADS, HEAD_PAD).transpose(0, 2, 1, 3).reshape(n_seq, MLA_HEADS * dec_seq, HEAD_PAD)


def kernel(x_prompt, x_sample, cache_kv_latent, cache_k_rope, state_ret, page_table, attn_norm_w, w_in, q_a_norm_w, w_uq, q_norm_w, kv_a_norm_w, w_ukv, k_norm_w, ret_norm_w, w_o, ffn_norm_w, w_peer_q, peer_sub_keys1, peer_sub_keys2, peer_u, peer_v):
    batch, seq, _ = x_prompt.shape
    n_seq, dec_seq, _ = x_sample.shape
    depth = attn_norm_w.shape[0]
    past = page_table.shape[1] * PAGE
    n_p, n_s = batch * seq, n_seq * dec_seq
    nrow = MLA_HEADS * dec_seq

    pos = jnp.concatenate([jnp.tile(jnp.arange(seq), batch), jnp.tile(past + jnp.arange(dec_seq), n_seq)])
    tabs = _rope_tables(pos)
    past_tabs = _rope_tables(jnp.arange(past))
    new_tabs = _rope_tables(past + jnp.arange(PAGE))
    g_rows = (jnp.arange(nrow)[:, None] // dec_seq == jnp.arange(MLA_HEADS * QK_NOPE)[None, :] // QK_NOPE).astype(BF16)

    h_all = jnp.concatenate([x_prompt.reshape(n_p, D_MODEL), x_sample.reshape(n_s, D_MODEL)], 0)
    outs = [[] for _ in range(6)]
    for l in range(depth):
        wts = _layer_weights(attn_norm_w[l], w_in[l], q_a_norm_w[l], w_uq[l], q_norm_w[l], kv_a_norm_w[l], w_ukv[l],
                             k_norm_w[l], ret_norm_w[l], w_o[l], ffn_norm_w[l], w_peer_q[l], peer_sub_keys1[l],
                             peer_sub_keys2[l], peer_u[l], peer_v[l])
        q, k, v, ckv, kslot, rq, rk, rv, rg = _mixer(h_all, wts, tabs)
        kpe = kslot[:, QK_NOPE:QK_HEAD]

        mla_p = _pattn(q, k, v, batch, seq)
        reto_p, r_p = _pret(rq, rk, rv, batch, seq)
        r_p = jnp.stack([r_p[:, :, :RET_DK, :RET_DV], r_p[:, :, RET_DK:, RET_DV:]], 2).reshape(batch, RET_HEADS, RET_DK, RET_DV)

        qa, qc, qs = _sabs(q[n_p:], wts["knw"], wts["wpe"], wts["wknt"])
        pad_rows = lambda a: jnp.pad(a.reshape(n_seq, dec_seq, -1), ((0, 0), (0, PAGE - dec_seq), (0, 0)))
        pc = _sattn(page_table, _to_rows(qa, n_seq, dec_seq), _to_rows(qc, n_seq, dec_seq), _to_rows(qs, n_seq, dec_seq),
                    wts["wkn"], g_rows, past_tabs["cosd"], past_tabs["sind"], pad_rows(ckv[n_p:]), pad_rows(kpe[n_p:]),
                    new_tabs["cosd"], new_tabs["sind"], cache_kv_latent[l], cache_k_rope[l], dec_seq)
        pc_h = pc.reshape(n_seq, MLA_HEADS, dec_seq, KV_LORA).transpose(1, 0, 2, 3).reshape(MLA_HEADS, n_s, KV_LORA)
        mla_s = _sup(pc_h, wts["wve"], wts["wvo"])
        pad8 = lambda a: jnp.pad(a.astype(F32).reshape(n_seq, dec_seq, -1), ((0, 0), (0, 8 - dec_seq), (0, 0)))
        reto_s, r_s = _sret(pad8(rq[n_p:]), pad8(rk[n_p:]), pad8(rv[n_p:]), state_ret[l].astype(F32), dec_seq)
        reto_s = reto_s[:, :dec_seq].reshape(n_s, 512)

        mla = jnp.concatenate([mla_p, mla_s], 0)
        reto = jnp.concatenate([reto_p, reto_s], 0)
        h_mid, hn, s1t, s2t = _post(h_all, mla, reto, rg, wts)
        e1t, e2t, gt = _topk(s1t, s2t)
        h_all = _peer(e1t, e2t, gt, hn, h_mid, wts["ut"], wts["vt"])

        outs[0].append(ckv[:n_p].reshape(batch, seq, KV_LORA))
        outs[1].append(kpe[:n_p].reshape(batch, seq, QK_ROPE))
        outs[2].append(r_p)
        outs[3].append(ckv[n_p:].reshape(n_seq, dec_seq, KV_LORA))
        outs[4].append(kpe[n_p:].reshape(n_seq, dec_seq, QK_ROPE))
        outs[5].append(r_s)
    return (h_all[:n_p].reshape(batch, seq, D_MODEL), h_all[n_p:].reshape(n_seq, dec_seq, D_MODEL),
            *[jnp.stack(o) for o in outs])
```

```python
import functools

import numpy as np
import jax
import jax.numpy as jnp
from jax import lax
from jax.experimental import pallas as pl
from jax.experimental.pallas import tpu as pltpu

F32 = jnp.float32
BF16 = jnp.bfloat16

D_MODEL = 1024
PAGE = 128
MLA_HEADS = 8
QK_NOPE = 64
QK_ROPE = 32
QK_HEAD = QK_NOPE + QK_ROPE
V_HEAD = 64
Q_LORA = 256
KV_LORA = 128
RET_HEADS = 8
RET_DK = 64
RET_DV = 64
RET_CHUNK = 128
ROPE_THETA = 10000.0
PEER_KEYS = 128
PEER_HEADS = 8
PEER_QDIM = 256
PEER_TOPK = 16
EPS = 1e-6
LANES = 128
SUBLANES = 8
HEAD_PAD = 128
NEG_INF = float("-inf")
INV_SQRT2 = 0.7071067811865476
LOG2E = 1.4426950408889634

MIX_TM = 256
ATT_T = 512
ATT_CH = 32
SAMP_PAGES = 16
SRET_SB = 8
TOPK_TT = 256
PEER_TN = 256
PEER_EB = 1024


def _cparams(sem, vmem_mb):
    return pltpu.CompilerParams(dimension_semantics=sem, vmem_limit_bytes=vmem_mb << 20)


def _full(shape):
    n = len(shape)
    return pl.BlockSpec(shape, lambda *_: (0,) * n)


def _rms(x, w, n):
    return x * lax.rsqrt(jnp.sum(x * x, -1, keepdims=True) * (1.0 / n) + EPS) * w


def _dot(a, b):
    return jnp.dot(a, b, preferred_element_type=F32)


def _dot_nt(a, b):
    return lax.dot_general(a, b, (((1,), (1,)), ((), ())), preferred_element_type=F32)


def _dot_tn(a, b):
    return lax.dot_general(a, b, (((0,), (0,)), ((), ())), preferred_element_type=F32)


def _mixer_body(x_ref, anw_ref, win_ref, qanw_ref, wuq_ref, qnw_ref, kvnw_ref, wukn_ref, wuv_ref, knw_ref,
                cq_ref, sq_ref, cr_ref, sr_ref,
                q_ref, k_ref, vt_ref, ckv_ref, kpe_ref, rq_ref, rk_ref, rv_ref, rg_ref):
    x = x_ref[...]
    xn = _rms(x, anw_ref[...], D_MODEL)
    proj = _dot(xn.astype(BF16), win_ref[...])
    lane = lax.broadcasted_iota(jnp.int32, (x.shape[0], LANES), 1)
    cq, sq, cr, sr = cq_ref[...], sq_ref[...], cr_ref[...], sr_ref[...]
    lo_q = lane < QK_NOPE + QK_ROPE // 2
    lo_r = (lane & (RET_DK - 1)) < RET_DK // 2

    def rope_q(v):
        return v * cq + jnp.where(lo_q, pltpu.roll(v, LANES - 16, 1), pltpu.roll(v, 16, 1)) * sq

    def rope_r(v):
        return v * cr + jnp.where(lo_r, pltpu.roll(v, LANES - 32, 1), pltpu.roll(v, 32, 1)) * sr

    q_lat = _rms(proj[:, 0:256], qanw_ref[...], Q_LORA)
    qf = _dot(q_lat.astype(BF16), wuq_ref[...])
    ckv = _rms(proj[:, 256:384], kvnw_ref[...], KV_LORA)
    ckv_ref[...] = ckv
    kslot = proj[:, 384:512]
    kpe_ref[...] = kslot
    cb = ckv.astype(BF16)
    knf = _dot(cb, wukn_ref[...])
    vt_ref[...] = _dot(cb, wuv_ref[...]).T.astype(BF16)
    qnw, knw = qnw_ref[...], knw_ref[...]
    for h in range(MLA_HEADS):
        sl = slice(HEAD_PAD * h, HEAD_PAD * (h + 1))
        q_ref[:, sl] = rope_q(_rms(qf[:, sl], qnw, QK_HEAD)).astype(BF16)
        k_ref[:, sl] = rope_q(_rms(knf[:, sl] + kslot, knw, QK_HEAD)).astype(BF16)
    for p in range(4):
        sl = slice(LANES * p, LANES * (p + 1))
        rq_ref[:, sl] = rope_r(proj[:, 512 + LANES * p:512 + LANES * (p + 1)]).astype(BF16)
        rk_ref[:, sl] = (rope_r(proj[:, 1024 + LANES * p:1024 + LANES * (p + 1)]) * (RET_DK ** -0.5)).astype(BF16)
    rv_ref[...] = proj[:, 1536:2048].astype(BF16)
    rg_ref[...] = proj[:, 2048:2560]


def _mixer(x_all, wts, tabs):
    n = x_all.shape[0]
    tm = MIX_TM
    row = lambda w: pl.BlockSpec((tm, w), lambda i: (i, 0))
    ins = [row(D_MODEL), _full((1, D_MODEL)), _full((D_MODEL, 2560)), _full((1, Q_LORA)), _full((Q_LORA, 1024)),
           _full((1, LANES)), _full((1, KV_LORA)), _full((KV_LORA, 1024)), _full((KV_LORA, 512)), _full((1, LANES)),
           row(LANES), row(LANES), row(LANES), row(LANES)]
    outs = [row(1024), row(1024), pl.BlockSpec((512, tm), lambda i: (0, i)), row(KV_LORA), row(LANES), row(512),
            row(512), row(512), row(512)]
    shapes = [jax.ShapeDtypeStruct((n, 1024), BF16), jax.ShapeDtypeStruct((n, 1024), BF16),
              jax.ShapeDtypeStruct((512, n), BF16), jax.ShapeDtypeStruct((n, KV_LORA), F32),
              jax.ShapeDtypeStruct((n, LANES), F32), jax.ShapeDtypeStruct((n, 512), BF16),
              jax.ShapeDtypeStruct((n, 512), BF16), jax.ShapeDtypeStruct((n, 512), BF16),
              jax.ShapeDtypeStruct((n, 512), F32)]
    return pl.pallas_call(
        _mixer_body, grid=(n // tm,), in_specs=ins, out_specs=outs, out_shape=shapes,
        compiler_params=_cparams(("arbitrary",), 48), name="mixer",
    )(x_all, wts["anw"], wts["win"], wts["qanw"], wts["wuq"], wts["qnw"], wts["kvnw"], wts["wukn"], wts["wuv"],
      wts["knw"], tabs["cq"], tabs["sq"], tabs["cr"], tabs["sr"])


def _pattn_body(q_ref, k_ref, vt_ref, o_ref, st0, st1, pt0, pt1, acc_sc):
    qi = pl.program_id(2)
    t = ATT_T
    ch = ATT_CH
    krow = lax.broadcasted_iota(jnp.int32, (ch, t), 0)
    qcol = lax.broadcasted_iota(jnp.int32, (ch, t), 1)
    acc_sc[...] = jnp.zeros(acc_sc.shape, F32)
    st_sc, pt_sc = (st0, st1), (pt0, pt1)

    def scores(j, slot):
        start = pl.multiple_of(j * t, t)
        for hh in range(2):
            lsl = slice(HEAD_PAD * hh, HEAD_PAD * (hh + 1))
            st_sc[hh][slot] = _dot_nt(k_ref[pl.ds(start, t), lsl], q_ref[:, lsl])

    def update(j, slot, carry, masked):
        start = pl.multiple_of(j * t, t)
        out = []
        for hh in range(2):
            m_prev, l_prev = carry[2 * hh], carry[2 * hh + 1]

            def chunk(r):
                blk = st_sc[hh][slot, ch * r:ch * (r + 1), :]
                return jnp.where(krow + ch * r <= qcol, blk, NEG_INF) if masked else blk

            mc = chunk(0)
            for r in range(1, t // ch):
                mc = jnp.maximum(mc, chunk(r))
            m_new = jnp.maximum(m_prev, jnp.max(mc, 0, keepdims=True))
            alpha = jnp.exp2(m_prev - m_new)
            ls = jnp.zeros((ch, t), F32)
            for r in range(t // ch):
                p = jnp.exp2(chunk(r) - m_new)
                ls = ls + p
                pt_sc[hh][ch * r:ch * (r + 1), :] = p.astype(BF16)
            acc_sc[hh] = alpha * acc_sc[hh] + _dot(vt_ref[:, pl.ds(start, t)], pt_sc[hh][...])
            out += [m_new, alpha * l_prev + jnp.sum(ls, 0, keepdims=True)]
        return tuple(out)

    def step(j, slot, carry):
        scores(j + 1, 1 - slot)
        return update(j, slot, carry, False)

    def pair(i, carry):
        return step(2 * i + 1, 1, step(2 * i, 0, carry))

    scores(0, 0)
    init = (jnp.full((1, t), NEG_INF, F32), jnp.zeros((1, t), F32)) * 2
    carry = lax.fori_loop(0, qi // 2, pair, init)
    _, l0, _, l1 = lax.cond(qi % 2 == 1,
                            lambda c: update(qi, 1, step(qi - 1, 0, c), True),
                            lambda c: update(qi, 0, c, True), carry)
    drow = lax.broadcasted_iota(jnp.int32, (LANES, t), 0)
    o_ref[...] = jnp.where(drow < V_HEAD, acc_sc[0] / l0, acc_sc[1] / l1).T.astype(o_ref.dtype)


def _pattn(q_all, k_all, vt_all, batch, seq):
    t = ATT_T
    nq = seq // t
    return pl.pallas_call(
        _pattn_body, grid=(batch, MLA_HEADS // 2, nq),
        in_specs=[pl.BlockSpec((t, 2 * HEAD_PAD), lambda b, hp, qi: (b * nq + qi, hp)),
                  pl.BlockSpec((seq, 2 * HEAD_PAD), lambda b, hp, qi: (b, hp)),
                  pl.BlockSpec((LANES, seq), lambda b, hp, qi: (hp, b))],
        out_specs=pl.BlockSpec((t, LANES), lambda b, hp, qi: (b * nq + qi, hp)),
        out_shape=jax.ShapeDtypeStruct((batch * seq, MLA_HEADS * V_HEAD), BF16),
        scratch_shapes=[pltpu.VMEM((2, t, t), F32), pltpu.VMEM((2, t, t), F32), pltpu.VMEM((t, t), BF16),
                        pltpu.VMEM((t, t), BF16), pltpu.VMEM((2, LANES, t), F32)],
        compiler_params=_cparams(("arbitrary", "arbitrary", "arbitrary"), 48), name="pattn",
    )(q_all, k_all, vt_all)


def _ret_log_decay():
    return jnp.log1p(-jnp.power(2.0, -5.0 - jnp.arange(RET_HEADS, dtype=F32)))


def _ret_tables(c):
    lg = _ret_log_decay()
    i = jnp.arange(c, dtype=F32)
    diff = i[:, None] - i[None, :]
    dmat = jnp.where(diff >= 0, jnp.exp(lg[:, None, None] * jnp.maximum(diff, 0.0)), 0.0)
    lane_head = lambda a: jnp.repeat(a.reshape(a.shape[0], 4, 2), RET_DV, axis=2).transpose(1, 0, 2)
    cross = lane_head(jnp.exp(lg[None, :] * (i[:, None] + 1.0)))
    kdec = lane_head(jnp.exp(lg[None, :] * (c - 1.0 - i[:, None])))
    gc = lane_head(jnp.exp(lg * c)[None, :])
    return dmat[0::2], dmat[1::2], cross, kdec, gc


def _pret_body(rq_ref, rk_ref, rv_ref, de_ref, do_ref, cross_ref, kdec_ref, gc_ref, o_ref, rout_ref, r_sc):
    c = pl.program_id(1)

    @pl.when(c == 0)
    def _():
        r_sc[...] = jnp.zeros(r_sc.shape, F32)

    n = rq_ref.shape[0]
    lane = lax.broadcasted_iota(jnp.int32, (n, LANES), 1)
    even = lane < RET_DK
    rr = lax.broadcasted_iota(jnp.int32, (LANES, LANES), 0)
    cc = lax.broadcasted_iota(jnp.int32, (LANES, LANES), 1)
    same_head = (rr < RET_DK) == (cc < RET_DV)
    for p in range(4):
        sl = slice(LANES * p, LANES * (p + 1))
        q, k, v = rq_ref[:, sl], rk_ref[:, sl], rv_ref[:, sl]
        zero = jnp.zeros_like(q)
        se = _dot_nt(jnp.where(even, q, zero), k)
        so = _dot_nt(jnp.where(even, zero, q), k)
        oe = _dot((se * de_ref[p]).astype(BF16), v)
        oo = _dot((so * do_ref[p]).astype(BF16), v)
        r = r_sc[p]
        cross = _dot(q, r.astype(BF16)) * cross_ref[p]
        o_ref[:, sl] = jnp.where(even, oe, oo) + cross
        kd = (k.astype(F32) * kdec_ref[p]).astype(BF16)
        upd = _dot_tn(kd, v)
        r_sc[p] = r * gc_ref[p] + jnp.where(same_head, upd, 0.0)

    @pl.when(c == pl.num_programs(1) - 1)
    def _():
        rout_ref[0] = r_sc[...]


def _pret(rq, rk, rv, batch, seq):
    c = RET_CHUNK
    nc = seq // c
    tabs = _ret_tables(c)
    blk = pl.BlockSpec((c, 512), lambda b, i: (b * nc + i, 0))
    return pl.pallas_call(
        _pret_body, grid=(batch, nc),
        in_specs=[blk, blk, blk, _full((4, c, c)), _full((4, c, c)), _full((4, c, LANES)), _full((4, c, LANES)),
                  _full((4, 1, LANES))],
        out_specs=[blk, pl.BlockSpec((1, 4, LANES, LANES), lambda b, i: (b, 0, 0, 0))],
        out_shape=[jax.ShapeDtypeStruct((batch * seq, 512), F32),
                   jax.ShapeDtypeStruct((batch, 4, LANES, LANES), F32)],
        scratch_shapes=[pltpu.VMEM((4, LANES, LANES), F32)],
        compiler_params=_cparams(("arbitrary", "arbitrary"), 32), name="pret",
    )(rq, rk, rv, *tabs)


def _sabs_body(q_ref, knw_ref, wpe_ref, wknt_ref, qa_ref, qc_ref, qs_ref):
    q = q_ref[...].astype(F32)
    qa_ref[...] = _dot((q * knw_ref[...]).astype(BF16), wknt_ref[0]).astype(BF16)
    lane = lax.broadcasted_iota(jnp.int32, q.shape, 1)
    qpe = pltpu.roll(q, LANES - QK_NOPE, 1)
    half = QK_ROPE // 2
    swapped = jnp.where(lane < half, pltpu.roll(qpe, LANES - half, 1), -pltpu.roll(qpe, half, 1))
    keep = lane < QK_ROPE
    wpe = wpe_ref[...]
    qc_ref[...] = jnp.where(keep, qpe * wpe, 0.0).astype(BF16)
    qs_ref[...] = jnp.where(keep, swapped * wpe, 0.0).astype(BF16)


def _sabs(q_s, knw, wpe, wknt):
    n = q_s.shape[0]
    blk = pl.BlockSpec((n, HEAD_PAD), lambda h: (0, h))
    shp = jax.ShapeDtypeStruct((n, MLA_HEADS * HEAD_PAD), BF16)
    return pl.pallas_call(
        _sabs_body, grid=(MLA_HEADS,),
        in_specs=[blk, _full((1, LANES)), _full((1, LANES)), pl.BlockSpec((1, LANES, KV_LORA), lambda h: (h, 0, 0))],
        out_specs=[blk, blk, blk], out_shape=[shp, shp, shp],
        compiler_params=_cparams(("arbitrary",), 32), name="sabs",
    )(q_s, knw, wpe, wknt)


def _sattn_body(pt_ref, qa_ref, qc_ref, qs_ref, wkn_ref, g_ref, cosd_ref, sind_ref,
                cnew_ref, pnew_ref, cosn_ref, sinn_ref, *rest, n_pages, dec_seq):
    lat_refs = rest[:n_pages]
    pe_refs = rest[n_pages:2 * n_pages]
    pc_ref = rest[2 * n_pages]
    cbuf, xc, xs, x2, m_sc, l_sc, acc_sc = rest[2 * n_pages + 1:]
    s_id = pl.program_id(0)
    j = pl.program_id(1)
    nrow = qa_ref.shape[1]

    @pl.when(jnp.logical_and(s_id == 0, j == 0))
    def _():
        xc[...] = jnp.zeros(xc.shape, BF16)
        xs[...] = jnp.zeros(xs.shape, BF16)
        x2[...] = jnp.zeros(x2.shape, BF16)

    @pl.when(j == 0)
    def _():
        m_sc[...] = jnp.full(m_sc.shape, NEG_INF, F32)
        l_sc[...] = jnp.zeros(l_sc.shape, F32)
        acc_sc[...] = jnp.zeros(acc_sc.shape, F32)

    qa, qc, qs = qa_ref[0], qc_ref[0], qs_ref[0]
    wkn, g = wkn_ref[...], g_ref[...]
    ones = jnp.ones((nrow, LANES), BF16)

    def attend(nk, mask):
        c = cbuf[0:nk, :]
        kn = _dot(c, wkn)
        ss = _dot_nt(g, (kn * kn).astype(BF16)) + _dot_nt(ones, x2[0:nk, :])
        rn = lax.rsqrt(ss * (1.0 / QK_HEAD) + EPS)
        s = (_dot_nt(qa, c) + _dot_nt(qc, xc[0:nk, :]) + _dot_nt(qs, xs[0:nk, :])) * rn
        if mask is not None:
            s = jnp.where(mask, s, NEG_INF)
        m_prev = m_sc[...]
        m_new = jnp.maximum(m_prev, jnp.max(s, -1, keepdims=True))
        alpha = jnp.exp2(m_prev - m_new)
        p = jnp.exp2(s - m_new)
        l_sc[...] = alpha * l_sc[...] + jnp.sum(p, -1, keepdims=True)
        acc_sc[...] = alpha * acc_sc[...] + _dot(p.astype(BF16), c)
        m_sc[...] = m_new

    def stage(i, lat, pe, cos, sin):
        rows = slice(PAGE * i, PAGE * (i + 1))
        cbuf[rows, :] = lat.astype(BF16)
        xc[rows, 0:QK_ROPE] = (pe * cos).astype(BF16)
        xs[rows, 0:QK_ROPE] = (pe * sin).astype(BF16)
        x2[rows, 0:QK_ROPE] = (pe * pe).astype(BF16)

    for i in range(n_pages):
        rows = slice(PAGE * i, PAGE * (i + 1))
        stage(i, lat_refs[i][0], pe_refs[i][0], cosd_ref[rows, :], sind_ref[rows, :])
    attend(PAGE * n_pages, None)

    @pl.when(j == pl.num_programs(1) - 1)
    def _():
        stage(0, cnew_ref[0], pnew_ref[0], cosn_ref[...], sinn_ref[...])
        qrow = lax.broadcasted_iota(jnp.int32, (nrow, PAGE), 0) % dec_seq
        kcol = lax.broadcasted_iota(jnp.int32, (nrow, PAGE), 1)
        attend(PAGE, jnp.logical_and(kcol <= qrow, kcol < dec_seq))
        pc_ref[0] = acc_sc[...] / l_sc[...]


def _sattn(page_table, qa, qc, qs, wkn, g, cosd, sind, cnew, pnew, cosn, sinn, cache_lat, cache_pe, dec_seq):
    n_seq, nrow, _ = qa.shape
    n_pages_total = page_table.shape[1]
    npg = SAMP_PAGES
    steps = n_pages_total // npg
    qblk = pl.BlockSpec((1, nrow, LANES), lambda s, j, pt: (s, 0, 0))
    const2 = lambda shape: pl.BlockSpec(shape, lambda s, j, pt: (0, 0))
    tab = pl.BlockSpec((npg * PAGE, QK_ROPE), lambda s, j, pt: (j, 0))
    in_specs = [qblk, qblk, qblk, const2((KV_LORA, 512)), const2((nrow, 512)), tab, tab,
                pl.BlockSpec((1, PAGE, KV_LORA), lambda s, j, pt: (s, 0, 0)),
                pl.BlockSpec((1, PAGE, QK_ROPE), lambda s, j, pt: (s, 0, 0)),
                const2((PAGE, QK_ROPE)), const2((PAGE, QK_ROPE))]
    for i in range(npg):
        in_specs.append(pl.BlockSpec((1, PAGE, KV_LORA), lambda s, j, pt, i=i: (pt[s, j * npg + i], 0, 0)))
    for i in range(npg):
        in_specs.append(pl.BlockSpec((1, PAGE, QK_ROPE), lambda s, j, pt, i=i: (pt[s, j * npg + i], 0, 0)))
    tk = npg * PAGE
    gs = pltpu.PrefetchScalarGridSpec(
        num_scalar_prefetch=1, grid=(n_seq, steps), in_specs=in_specs,
        out_specs=pl.BlockSpec((1, nrow, LANES), lambda s, j, pt: (s, 0, 0)),
        scratch_shapes=[pltpu.VMEM((tk, LANES), BF16), pltpu.VMEM((tk, LANES), BF16), pltpu.VMEM((tk, LANES), BF16),
                        pltpu.VMEM((tk, LANES), BF16), pltpu.VMEM((nrow, 1), F32), pltpu.VMEM((nrow, 1), F32),
                        pltpu.VMEM((nrow, LANES), F32)])
    return pl.pallas_call(
        functools.partial(_sattn_body, n_pages=npg, dec_seq=dec_seq), grid_spec=gs,
        out_shape=jax.ShapeDtypeStruct((n_seq, nrow, LANES), F32),
        compiler_params=_cparams(("arbitrary", "arbitrary"), 48), name="sattn",
    )(page_table, qa, qc, qs, wkn, g, cosd, sind, cnew, pnew, cosn, sinn,
      *([cache_lat] * npg), *([cache_pe] * npg))


def _sup_body(pc_ref, we_ref, wo_ref, o_ref):
    o_ref[...] = (_dot(pc_ref[0].astype(BF16), we_ref[0]) + _dot(pc_ref[1].astype(BF16), wo_ref[0])).astype(o_ref.dtype)


def _sup(pc_h, wve, wvo):
    n = pc_h.shape[1]
    return pl.pallas_call(
        _sup_body, grid=(MLA_HEADS // 2,),
        in_specs=[pl.BlockSpec((2, n, KV_LORA), lambda p: (p, 0, 0)),
                  pl.BlockSpec((1, KV_LORA, LANES), lambda p: (p, 0, 0)),
                  pl.BlockSpec((1, KV_LORA, LANES), lambda p: (p, 0, 0))],
        out_specs=pl.BlockSpec((n, LANES), lambda p: (0, p)),
        out_shape=jax.ShapeDtypeStruct((n, MLA_HEADS * V_HEAD), BF16),
        compiler_params=_cparams(("arbitrary",), 32), name="sup",
    )(pc_h, wve, wvo)


def _sret_body(rq_ref, rk_ref, rv_ref, r0_ref, d_ref, cross_ref, kdec_ref, g_ref, o_ref, r_ref):
    def one(si, carry):
        q8, k8, v8 = rq_ref[si], rk_ref[si], rv_ref[si]
        for h in range(RET_HEADS):
            sl = slice(RET_DK * h, RET_DK * (h + 1))
            q, k, v = q8[:, sl], k8[:, sl], v8[:, sl]
            r = r0_ref[si, h]
            inner = _dot_nt(q, k) * d_ref[h]
            o_ref[si, :, sl] = _dot(inner, v) + _dot(q, r) * cross_ref[h]
            r_ref[si, h] = g_ref[h] * r + _dot_tn(k * kdec_ref[h], v)
        return carry

    lax.fori_loop(0, rq_ref.shape[0], one, 0)


def _sret(rq, rk, rv, r0, dec_seq):
    n_seq = rq.shape[0]
    sb = SRET_SB
    lg = _ret_log_decay()
    i = jnp.arange(8, dtype=F32)
    valid = i < dec_seq
    diff = i[:, None] - i[None, :]
    dmat = jnp.where((diff >= 0) & valid[:, None] & valid[None, :],
                     jnp.exp(lg[:, None, None] * jnp.maximum(diff, 0.0)), 0.0)
    cross = jnp.exp(lg[:, None, None] * (i[None, :, None] + 1.0)) * jnp.ones((1, 1, RET_DV), F32)
    kdec = jnp.where(valid[None, :, None], jnp.exp(lg[:, None, None] * (dec_seq - 1.0 - i[None, :, None])), 0.0)
    kdec = kdec * jnp.ones((1, 1, RET_DK), F32)
    gdec = jnp.exp(lg * dec_seq)[:, None, None] * jnp.ones((1, RET_DK, RET_DV), F32)
    blk = pl.BlockSpec((sb, 8, 512), lambda s: (s, 0, 0))
    rblk = pl.BlockSpec((sb, RET_HEADS, RET_DK, RET_DV), lambda s: (s, 0, 0, 0))
    return pl.pallas_call(
        _sret_body, grid=(n_seq // sb,),
        in_specs=[blk, blk, blk, rblk, _full((RET_HEADS, 8, 8)), _full((RET_HEADS, 8, RET_DV)),
                  _full((RET_HEADS, 8, RET_DK)), _full((RET_HEADS, RET_DK, RET_DV))],
        out_specs=[blk, rblk],
        out_shape=[jax.ShapeDtypeStruct((n_seq, 8, 512), F32),
                   jax.ShapeDtypeStruct((n_seq, RET_HEADS, RET_DK, RET_DV), F32)],
        compiler_params=_cparams(("arbitrary",), 32), name="sret",
    )(rq, rk, rv, r0, dmat, cross, kdec, gdec)


def _post_body(x_ref, mla_ref, reto_ref, rg_ref, rnw_ref, wo1_ref, wo2_ref, fnw_ref, wpq_ref, k1_ref, k2_ref,
               h_ref, hn_ref, s1_ref, s2_ref):
    tm = x_ref.shape[0]
    lane = lax.broadcasted_iota(jnp.int32, (tm, LANES), 1)
    lo = lane < RET_DV

    def group(a):
        s_lo = jnp.sum(jnp.where(lo, a, 0.0), -1, keepdims=True)
        s_hi = jnp.sum(jnp.where(lo, 0.0, a), -1, keepdims=True)
        return jnp.where(lo, s_lo, s_hi)

    parts = []
    for p in range(4):
        sl = slice(LANES * p, LANES * (p + 1))
        o = reto_ref[:, sl]
        d = o - group(o) * (1.0 / RET_DV)
        y = d * lax.rsqrt(group(d * d) * (1.0 / RET_DV) + EPS) * rnw_ref[:, sl]
        rg = rg_ref[:, sl]
        parts.append((rg * jax.nn.sigmoid(rg) * y).astype(BF16))
    ret_out = jnp.concatenate(parts, axis=1)
    h = x_ref[...] + _dot(mla_ref[...], wo1_ref[...]) + _dot(ret_out, wo2_ref[...])
    h_ref[...] = h
    hn = _rms(h, fnw_ref[...], D_MODEL).astype(BF16)
    hn_ref[...] = hn
    pq = _dot(hn, wpq_ref[...])
    half = PEER_QDIM // 2
    for hd in range(PEER_HEADS):
        q1 = pq[:, PEER_QDIM * hd:PEER_QDIM * hd + half].astype(BF16)
        q2 = pq[:, PEER_QDIM * hd + half:PEER_QDIM * (hd + 1)].astype(BF16)
        s1_ref[hd] = _dot_nt(k1_ref[...], q1)
        s2_ref[hd] = _dot_nt(k2_ref[...], q2)


def _post(x_all, mla, reto, rg, wts):
    n = x_all.shape[0]
    tm = MIX_TM
    row = lambda w: pl.BlockSpec((tm, w), lambda i: (i, 0))
    sblk = pl.BlockSpec((PEER_HEADS, PEER_KEYS, tm), lambda i: (0, 0, i))
    half = PEER_QDIM // 2
    return pl.pallas_call(
        _post_body, grid=(n // tm,),
        in_specs=[row(D_MODEL), row(512), row(512), row(512), _full((1, 512)), _full((512, D_MODEL)),
                  _full((512, D_MODEL)), _full((1, D_MODEL)), _full((D_MODEL, PEER_HEADS * PEER_QDIM)),
                  _full((PEER_KEYS, half)), _full((PEER_KEYS, half))],
        out_specs=[row(D_MODEL), row(D_MODEL), sblk, sblk],
        out_shape=[jax.ShapeDtypeStruct((n, D_MODEL), F32), jax.ShapeDtypeStruct((n, D_MODEL), BF16),
                   jax.ShapeDtypeStruct((PEER_HEADS, PEER_KEYS, n), F32),
                   jax.ShapeDtypeStruct((PEER_HEADS, PEER_KEYS, n), F32)],
        compiler_params=_cparams(("arbitrary",), 48), name="post",
    )(x_all, mla, reto, rg, wts["rnw"], wts["wo1"], wts["wo2"], wts["fnw"], wts["wpq"], wts["k1"], wts["k2"])


def _topk_body(s1_ref, s2_ref, e1_ref, e2_ref, g_ref):
    kk = PEER_TOPK
    shift = kk.bit_length() - 1
    assert kk == 1 << shift
    tt = s1_ref.shape[2]
    row16 = lax.broadcasted_iota(jnp.int32, (kk, tt), 0)

    def take_top(x):
        nrow = x.shape[0]
        row = lax.broadcasted_iota(jnp.int32, x.shape, 0)
        vals = jnp.zeros((kk, tt), F32)
        idxs = jnp.zeros((kk, tt), jnp.int32)
        for r in range(kk):
            m = jnp.max(x, axis=0, keepdims=True)
            idx = jnp.min(jnp.where(x == m, row, nrow), axis=0, keepdims=True)
            vals = jnp.where(row16 == r, m, vals)
            idxs = jnp.where(row16 == r, idx, idxs)
            x = jnp.where(row == idx, NEG_INF, x)
        return vals, idxs

    def pick(table, sel):
        return jnp.sum(jnp.where(row16 == sel, table, 0), axis=0, keepdims=True)

    v1, i1 = take_top(s1_ref[0])
    v2, i2 = take_top(s2_ref[0])
    cand = jnp.concatenate([v1[a:a + 1, :] + v2 for a in range(kk)], axis=0)
    vs, ic = take_top(cand)
    e1 = jnp.zeros((kk, tt), jnp.int32)
    e2 = jnp.zeros((kk, tt), jnp.int32)
    for r in range(kk):
        f = ic[r:r + 1, :]
        e1 = jnp.where(row16 == r, pick(i1, lax.shift_right_logical(f, shift)), e1)
        e2 = jnp.where(row16 == r, pick(i2, f & (kk - 1)), e2)
    p = jnp.exp(vs - jnp.max(vs, axis=0, keepdims=True))
    g_ref[...] = p / jnp.sum(p, axis=0, keepdims=True)
    e1_ref[...] = e1.astype(F32)
    e2_ref[...] = e2.astype(F32)


def _topk(s1t, s2t):
    n = s1t.shape[2]
    tt = TOPK_TT
    sblk = pl.BlockSpec((1, PEER_KEYS, tt), lambda i, h: (h, 0, i))
    oblk = pl.BlockSpec((PEER_TOPK, tt), lambda i, h: (h, i))
    shp = jax.ShapeDtypeStruct((PEER_HEADS * PEER_TOPK, n), F32)
    return pl.pallas_call(
        _topk_body, grid=(n // tt, PEER_HEADS), in_specs=[sblk, sblk], out_specs=[oblk, oblk, oblk],
        out_shape=[shp, shp, shp],
        compiler_params=_cparams(("arbitrary", "arbitrary"), 32), name="topk",
    )(s1t, s2t)


def _peer_body(e1_ref, e2_ref, g_ref, hn_ref, h_ref, u_ref, v_ref, o_ref, e1_sc, e2_sc, g_sc, w_sc, acc_sc):
    eb = pl.program_id(1)
    tn = hn_ref.shape[0]

    @pl.when(eb == 0)
    def _():
        e1_sc[...] = e1_ref[...].T
        e2_sc[...] = e2_ref[...].T
        g_sc[...] = g_ref[...].T
        acc_sc[...] = jnp.zeros(acc_sc.shape, F32)
        iot = lax.broadcasted_iota(jnp.int32, (PEER_KEYS, LANES), 0).astype(F32)

        def body(grp, c):
            rows = pl.ds(pl.multiple_of(grp * SUBLANES, SUBLANES), SUBLANES)
            e1g, e2g, gg = e1_sc[rows, :], e2_sc[rows, :], g_sc[rows, :]
            for t in range(SUBLANES):
                a = jnp.where(e1g[t:t + 1, :] == iot, 1.0, 0.0).astype(BF16)
                b = jnp.where(e2g[t:t + 1, :] == iot, gg[t:t + 1, :], 0.0).astype(BF16)
                w_sc[grp, pl.ds(t, PEER_KEYS, stride=SUBLANES), :] = _dot_nt(a, b)
            return c

        lax.fori_loop(0, tn // SUBLANES, body, 0)

    act = _dot_nt(hn_ref[...], u_ref[...])
    gl = 0.5 * act * (1.0 + lax.erf(act * INV_SQRT2))
    zs = []
    for c in range(PEER_EB // PEER_KEYS):
        e1 = eb * (PEER_EB // PEER_KEYS) + c
        wsl = w_sc[:, pl.ds(pl.multiple_of(e1 * SUBLANES, SUBLANES), SUBLANES), :].reshape(tn, LANES)
        zs.append((gl[:, LANES * c:LANES * (c + 1)] * wsl).astype(BF16))
    acc_sc[...] += _dot(jnp.concatenate(zs, axis=1), v_ref[...])

    @pl.when(eb == pl.num_programs(1) - 1)
    def _():
        o_ref[...] = h_ref[...] + acc_sc[...]


def _peer(e1t, e2t, gt, hn, h, u, v):
    n = hn.shape[0]
    tn = PEER_TN
    n_exp = u.shape[0]
    jn = PEER_HEADS * PEER_TOPK
    sel = pl.BlockSpec((jn, tn), lambda i, e: (0, i))
    row = pl.BlockSpec((tn, D_MODEL), lambda i, e: (i, 0))
    wblk = pl.BlockSpec((PEER_EB, D_MODEL), lambda i, e: (e, 0))
    return pl.pallas_call(
        _peer_body, grid=(n // tn, n_exp // PEER_EB),
        in_specs=[sel, sel, sel, row, row, wblk, wblk],
        out_specs=row, out_shape=jax.ShapeDtypeStruct((n, D_MODEL), F32),
        scratch_shapes=[pltpu.VMEM((tn, jn), F32), pltpu.VMEM((tn, jn), F32), pltpu.VMEM((tn, jn), F32),
                        pltpu.VMEM((tn // SUBLANES, PEER_KEYS * SUBLANES, LANES), F32),
                        pltpu.VMEM((tn, D_MODEL), F32)],
        compiler_params=_cparams(("arbitrary", "arbitrary"), 56), name="peer",
    )(e1t, e2t, gt, hn, h, u, v)


def _pad_last(a, n):
    return jnp.pad(a, [(0, 0)] * (a.ndim - 1) + [(0, n - a.shape[-1])])


def _layer_weights(attn_norm_w, w_in, q_a_norm_w, w_uq, q_norm_w, kv_a_norm_w, w_ukv, k_norm_w,
                   ret_norm_w, w_o, ffn_norm_w, w_peer_q, sub_keys1, sub_keys2, peer_u, peer_v):
    o = np.cumsum([0, Q_LORA, KV_LORA, QK_ROPE, 512, 512, 512, 512])
    zeros = lambda w: jnp.zeros((D_MODEL, w), F32)
    win = jnp.concatenate([w_in[:, o[0]:o[2]], zeros(QK_NOPE), w_in[:, o[2]:o[3]], zeros(LANES - QK_HEAD),
                           w_in[:, o[3]:o[7]]], axis=1)
    wuq = _pad_last(w_uq.reshape(Q_LORA, MLA_HEADS, QK_HEAD), HEAD_PAD).reshape(Q_LORA, MLA_HEADS * HEAD_PAD)
    ukv = w_ukv.reshape(KV_LORA, MLA_HEADS, QK_NOPE + V_HEAD)
    ukn, uv = ukv[:, :, :QK_NOPE], ukv[:, :, QK_NOPE:]
    knw = _pad_last(k_norm_w, LANES)[None]
    wknt = _pad_last(ukn, HEAD_PAD).transpose(1, 2, 0)
    uvp = uv.reshape(KV_LORA, 4, 2, V_HEAD)
    zv = jnp.zeros((KV_LORA, 4, V_HEAD), F32)
    wve = jnp.concatenate([uvp[:, :, 0], zv], -1).transpose(1, 0, 2)
    wvo = jnp.concatenate([zv, uvp[:, :, 1]], -1).transpose(1, 0, 2)
    return dict(
        anw=attn_norm_w[None], win=win.astype(BF16), qanw=q_a_norm_w[None], wuq=wuq.astype(BF16),
        qnw=(_pad_last(q_norm_w, LANES) * (QK_HEAD ** -0.5 * LOG2E))[None], kvnw=kv_a_norm_w[None],
        wukn=_pad_last(ukn, HEAD_PAD).reshape(KV_LORA, MLA_HEADS * HEAD_PAD).astype(BF16),
        wuv=uv.reshape(KV_LORA, MLA_HEADS * V_HEAD).astype(BF16), knw=knw,
        wpe=_pad_last(k_norm_w[QK_NOPE:], LANES)[None], wknt=wknt.astype(BF16),
        wkn=ukn.reshape(KV_LORA, MLA_HEADS * QK_NOPE).astype(BF16), wve=wve.astype(BF16), wvo=wvo.astype(BF16),
        rnw=ret_norm_w[None], wo1=w_o[:512].astype(BF16), wo2=w_o[512:].astype(BF16), fnw=ffn_norm_w[None],
        wpq=w_peer_q.astype(BF16), k1=sub_keys1.astype(BF16), k2=sub_keys2.astype(BF16),
        pu=peer_u.astype(BF16), pv=peer_v.astype(BF16))


def _rope_tables(pos):
    posf = pos.astype(F32)[:, None]
    n = pos.shape[0]
    hq = QK_ROPE // 2
    ang = posf * (ROPE_THETA ** (-jnp.arange(hq, dtype=F32) / hq))[None, :]
    c, s = jnp.cos(ang), jnp.sin(ang)
    one, zero = jnp.ones((n, QK_NOPE), F32), jnp.zeros((n, QK_NOPE), F32)
    cq = jnp.concatenate([one, c, c, one[:, :LANES - QK_HEAD]], 1)
    sq = jnp.concatenate([zero, -s, s, zero[:, :LANES - QK_HEAD]], 1)
    hr = RET_DK // 2
    angr = posf * (ROPE_THETA ** (-jnp.arange(hr, dtype=F32) / hr))[None, :]
    c2, s2 = jnp.cos(angr), jnp.sin(angr)
    cr = jnp.concatenate([c2, c2, c2, c2], 1)
    sr = jnp.concatenate([-s2, s2, -s2, s2], 1)
    return dict(cq=cq, sq=sq, cr=cr, sr=sr, cosd=jnp.concatenate([c, c], 1), sind=jnp.concatenate([s, s], 1))


def _to_rows(a, n_seq, dec_seq):
    return a.reshape(n_seq, dec_seq, MLA_HEADS, HEAD_PAD).transpose(0, 2, 1, 3).reshape(n_seq, MLA_HEADS * dec_seq, HEAD_PAD)


def kernel(x_prompt, x_sample, cache_kv_latent, cache_k_rope, state_ret, page_table, attn_norm_w, w_in, q_a_norm_w, w_uq, q_norm_w, kv_a_norm_w, w_ukv, k_norm_w, ret_norm_w, w_o, ffn_norm_w, w_peer_q, peer_sub_keys1, peer_sub_keys2, peer_u, peer_v):
    batch, seq, _ = x_prompt.shape
    n_seq, dec_seq, _ = x_sample.shape
    depth = attn_norm_w.shape[0]
    past = page_table.shape[1] * PAGE
    n_p, n_s = batch * seq, n_seq * dec_seq
    nrow = MLA_HEADS * dec_seq

    pos = jnp.concatenate([jnp.tile(jnp.arange(seq), batch), jnp.tile(past + jnp.arange(dec_seq), n_seq)])
    tabs = _rope_tables(pos)
    past_tabs = _rope_tables(jnp.arange(past))
    new_tabs = _rope_tables(past + jnp.arange(PAGE))
    g_rows = (jnp.arange(nrow)[:, None] // dec_seq == jnp.arange(MLA_HEADS * QK_NOPE)[None, :] // QK_NOPE).astype(BF16)

    h_all = jnp.concatenate([x_prompt.reshape(n_p, D_MODEL), x_sample.reshape(n_s, D_MODEL)], 0)
    outs = [[] for _ in range(6)]
    for l in range(depth):
        wts = _layer_weights(attn_norm_w[l], w_in[l], q_a_norm_w[l], w_uq[l], q_norm_w[l], kv_a_norm_w[l], w_ukv[l],
                             k_norm_w[l], ret_norm_w[l], w_o[l], ffn_norm_w[l], w_peer_q[l], peer_sub_keys1[l],
                             peer_sub_keys2[l], peer_u[l], peer_v[l])
        q, k, vt, ckv, kslot, rq, rk, rv, rg = _mixer(h_all, wts, tabs)
        kpe = kslot[:, QK_NOPE:QK_HEAD]

        mla_p = _pattn(q, k, vt, batch, seq)
        reto_p, r_p = _pret(rq, rk, rv, batch, seq)
        r_p = jnp.stack([r_p[:, :, :RET_DK, :RET_DV], r_p[:, :, RET_DK:, RET_DV:]], 2).reshape(batch, RET_HEADS, RET_DK, RET_DV)

        qa, qc, qs = _sabs(q[n_p:], wts["knw"], wts["wpe"], wts["wknt"])
        pad_rows = lambda a: jnp.pad(a.reshape(n_seq, dec_seq, -1), ((0, 0), (0, PAGE - dec_seq), (0, 0)))
        pc = _sattn(page_table, _to_rows(qa, n_seq, dec_seq), _to_rows(qc, n_seq, dec_seq), _to_rows(qs, n_seq, dec_seq),
                    wts["wkn"], g_rows, past_tabs["cosd"], past_tabs["sind"], pad_rows(ckv[n_p:]), pad_rows(kpe[n_p:]),
                    new_tabs["cosd"], new_tabs["sind"], cache_kv_latent[l], cache_k_rope[l], dec_seq)
        pc_h = pc.reshape(n_seq, MLA_HEADS, dec_seq, KV_LORA).transpose(1, 0, 2, 3).reshape(MLA_HEADS, n_s, KV_LORA)
        mla_s = _sup(pc_h, wts["wve"], wts["wvo"])
        pad8 = lambda a: jnp.pad(a.astype(F32).reshape(n_seq, dec_seq, -1), ((0, 0), (0, 8 - dec_seq), (0, 0)))
        reto_s, r_s = _sret(pad8(rq[n_p:]), pad8(rk[n_p:]), pad8(rv[n_p:]), state_ret[l].astype(F32), dec_seq)
        reto_s = reto_s[:, :dec_seq].reshape(n_s, 512)

        mla = jnp.concatenate([mla_p, mla_s], 0)
        reto = jnp.concatenate([reto_p, reto_s], 0)
        h_mid, hn, s1t, s2t = _post(h_all, mla, reto, rg, wts)
        e1t, e2t, gt = _topk(s1t, s2t)
        h_all = _peer(e1t, e2t, gt, hn, h_mid, wts["pu"], wts["pv"])

        outs[0].append(ckv[:n_p].reshape(batch, seq, KV_LORA))
        outs[1].append(kpe[:n_p].reshape(batch, seq, QK_ROPE))
        outs[2].append(r_p)
        outs[3].append(ckv[n_p:].reshape(n_seq, dec_seq, KV_LORA))
        outs[4].append(kpe[n_p:].reshape(n_seq, dec_seq, QK_ROPE))
        outs[5].append(r_s)
    return (h_all[:n_p].reshape(batch, seq, D_MODEL), h_all[n_p:].reshape(n_seq, dec_seq, D_MODEL),
            *[jnp.stack(o) for o in outs])
```

```python
import functools

import numpy as np
import jax
import jax.numpy as jnp
from jax import lax
from jax.experimental import pallas as pl
from jax.experimental.pallas import tpu as pltpu

F32 = jnp.float32
BF16 = jnp.bfloat16

D_MODEL = 1024
PAGE = 128
MLA_HEADS = 8
QK_NOPE = 64
QK_ROPE = 32
QK_HEAD = QK_NOPE + QK_ROPE
V_HEAD = 64
Q_LORA = 256
KV_LORA = 128
RET_HEADS = 8
RET_DK = 64
RET_DV = 64
RET_CHUNK = 128
ROPE_THETA = 10000.0
PEER_KEYS = 128
PEER_HEADS = 8
PEER_QDIM = 256
PEER_TOPK = 16
EPS = 1e-6
LANES = 128
SUBLANES = 8
HEAD_PAD = 128
NEG_INF = float("-inf")
INV_SQRT2 = 0.7071067811865476
LOG2E = 1.4426950408889634

MIX_TM = 256
ATT_T = 512
ATT_CH = 32
SAMP_PAGES = 32
SAMP_CHUNK = 512
TOPK_TT = 256
PEER_TN = 256
PEER_EB = 1024


def _cparams(sem, vmem_mb):
    return pltpu.CompilerParams(dimension_semantics=sem, vmem_limit_bytes=vmem_mb << 20)


def _full(shape):
    n = len(shape)
    return pl.BlockSpec(shape, lambda *_: (0,) * n)


def _rms(x, w, n):
    return x * lax.rsqrt(jnp.sum(x * x, -1, keepdims=True) * (1.0 / n) + EPS) * w


def _dot(a, b):
    return jnp.dot(a, b, preferred_element_type=F32)


def _dot_nt(a, b):
    return lax.dot_general(a, b, (((1,), (1,)), ((), ())), preferred_element_type=F32)


def _dot_tn(a, b):
    return lax.dot_general(a, b, (((0,), (0,)), ((), ())), preferred_element_type=F32)


def _mixer_body(x_ref, anw_ref, win_ref, qanw_ref, wuq_ref, qnw_ref, kvnw_ref, wukn_ref, wuv_ref, knw_ref,
                cq_ref, sq_ref, cr_ref, sr_ref,
                q_ref, k_ref, vt_ref, ckv_ref, kpe_ref, rq_ref, rk_ref, rv_ref, rg_ref):
    x = x_ref[...]
    xn = _rms(x, anw_ref[...], D_MODEL)
    proj = _dot(xn.astype(BF16), win_ref[...])
    lane = lax.broadcasted_iota(jnp.int32, (x.shape[0], LANES), 1)
    cq, sq, cr, sr = cq_ref[...], sq_ref[...], cr_ref[...], sr_ref[...]
    lo_q = lane < QK_NOPE + QK_ROPE // 2
    lo_r = (lane & (RET_DK - 1)) < RET_DK // 2

    def rope_q(v):
        return v * cq + jnp.where(lo_q, pltpu.roll(v, LANES - 16, 1), pltpu.roll(v, 16, 1)) * sq

    def rope_r(v):
        return v * cr + jnp.where(lo_r, pltpu.roll(v, LANES - 32, 1), pltpu.roll(v, 32, 1)) * sr

    q_lat = _rms(proj[:, 0:256], qanw_ref[...], Q_LORA)
    qf = _dot(q_lat.astype(BF16), wuq_ref[...])
    ckv = _rms(proj[:, 256:384], kvnw_ref[...], KV_LORA)
    ckv_ref[...] = ckv
    kslot = proj[:, 384:512]
    kpe_ref[...] = kslot.T[QK_NOPE:QK_HEAD, :]
    cb = ckv.astype(BF16)
    knf = _dot(cb, wukn_ref[...])
    vt_ref[...] = _dot(cb, wuv_ref[...]).T.astype(BF16)
    qnw, knw = qnw_ref[...], knw_ref[...]
    for h in range(MLA_HEADS):
        sl = slice(HEAD_PAD * h, HEAD_PAD * (h + 1))
        q_ref[:, sl] = rope_q(_rms(qf[:, sl], qnw, QK_HEAD)).astype(BF16)
        k_ref[:, sl] = rope_q(_rms(knf[:, sl] + kslot, knw, QK_HEAD)).astype(BF16)
    for p in range(4):
        sl = slice(LANES * p, LANES * (p + 1))
        rq_ref[:, sl] = rope_r(proj[:, 512 + LANES * p:512 + LANES * (p + 1)]).astype(BF16)
        rk_ref[:, sl] = (rope_r(proj[:, 1024 + LANES * p:1024 + LANES * (p + 1)]) * (RET_DK ** -0.5)).astype(BF16)
    rv_ref[...] = proj[:, 1536:2048].astype(BF16)
    rg_ref[...] = proj[:, 2048:2560]


def _mixer(x_all, wts, tabs):
    n = x_all.shape[0]
    tm = MIX_TM
    row = lambda w: pl.BlockSpec((tm, w), lambda i: (i, 0))
    ins = [row(D_MODEL), _full((1, D_MODEL)), _full((D_MODEL, 2560)), _full((1, Q_LORA)), _full((Q_LORA, 1024)),
           _full((1, LANES)), _full((1, KV_LORA)), _full((KV_LORA, 1024)), _full((KV_LORA, 512)), _full((1, LANES)),
           row(LANES), row(LANES), row(LANES), row(LANES)]
    outs = [row(1024), row(1024), pl.BlockSpec((512, tm), lambda i: (0, i)), row(KV_LORA),
            pl.BlockSpec((QK_ROPE, tm), lambda i: (0, i)), row(512),
            row(512), row(512), row(512)]
    shapes = [jax.ShapeDtypeStruct((n, 1024), BF16), jax.ShapeDtypeStruct((n, 1024), BF16),
              jax.ShapeDtypeStruct((512, n), BF16), jax.ShapeDtypeStruct((n, KV_LORA), F32),
              jax.ShapeDtypeStruct((QK_ROPE, n), F32), jax.ShapeDtypeStruct((n, 512), BF16),
              jax.ShapeDtypeStruct((n, 512), BF16), jax.ShapeDtypeStruct((n, 512), BF16),
              jax.ShapeDtypeStruct((n, 512), F32)]
    return pl.pallas_call(
        _mixer_body, grid=(n // tm,), in_specs=ins, out_specs=outs, out_shape=shapes,
        compiler_params=_cparams(("arbitrary",), 48), name="mixer",
    )(x_all, wts["anw"], wts["win"], wts["qanw"], wts["wuq"], wts["qnw"], wts["kvnw"], wts["wukn"], wts["wuv"],
      wts["knw"], tabs["cq"], tabs["sq"], tabs["cr"], tabs["sr"])


def _pattn_body(q_ref, k_ref, vt_ref, o_ref, st0, st1, pt0, pt1, acc_sc):
    qi = pl.program_id(2)
    t = ATT_T
    ch = ATT_CH
    krow = lax.broadcasted_iota(jnp.int32, (ch, t), 0)
    qcol = lax.broadcasted_iota(jnp.int32, (ch, t), 1)
    acc_sc[...] = jnp.zeros(acc_sc.shape, F32)
    st_sc, pt_sc = (st0, st1), (pt0, pt1)

    def scores(j, slot):
        start = pl.multiple_of(j * t, t)
        for hh in range(2):
            lsl = slice(HEAD_PAD * hh, HEAD_PAD * (hh + 1))
            st_sc[hh][slot] = _dot_nt(k_ref[pl.ds(start, t), lsl], q_ref[:, lsl])

    def update(j, slot, carry, masked):
        start = pl.multiple_of(j * t, t)
        out = []
        for hh in range(2):
            m_prev, l_prev = carry[2 * hh], carry[2 * hh + 1]

            def chunk(r):
                blk = st_sc[hh][slot, ch * r:ch * (r + 1), :]
                return jnp.where(krow + ch * r <= qcol, blk, NEG_INF) if masked else blk

            mc = chunk(0)
            for r in range(1, t // ch):
                mc = jnp.maximum(mc, chunk(r))
            m_new = jnp.maximum(m_prev, jnp.max(mc, 0, keepdims=True))
            alpha = jnp.exp2(m_prev - m_new)
            ls = jnp.zeros((ch, t), F32)
            for r in range(t // ch):
                p = jnp.exp2(chunk(r) - m_new)
                ls = ls + p
                pt_sc[hh][ch * r:ch * (r + 1), :] = p.astype(BF16)
            acc_sc[hh] = alpha * acc_sc[hh] + _dot(vt_ref[:, pl.ds(start, t)], pt_sc[hh][...])
            out += [m_new, alpha * l_prev + jnp.sum(ls, 0, keepdims=True)]
        return tuple(out)

    def step(j, slot, carry):
        scores(j + 1, 1 - slot)
        return update(j, slot, carry, False)

    def pair(i, carry):
        return step(2 * i + 1, 1, step(2 * i, 0, carry))

    scores(0, 0)
    init = (jnp.full((1, t), NEG_INF, F32), jnp.zeros((1, t), F32)) * 2
    carry = lax.fori_loop(0, qi // 2, pair, init)
    _, l0, _, l1 = lax.cond(qi % 2 == 1,
                            lambda c: update(qi, 1, step(qi - 1, 0, c), True),
                            lambda c: update(qi, 0, c, True), carry)
    drow = lax.broadcasted_iota(jnp.int32, (LANES, t), 0)
    o_ref[...] = jnp.where(drow < V_HEAD, acc_sc[0] / l0, acc_sc[1] / l1).T.astype(o_ref.dtype)


def _pattn(q_all, k_all, vt_all, batch, seq):
    t = ATT_T
    nq = seq // t
    return pl.pallas_call(
        _pattn_body, grid=(batch, MLA_HEADS // 2, nq),
        in_specs=[pl.BlockSpec((t, 2 * HEAD_PAD), lambda b, hp, qi: (b * nq + qi, hp)),
                  pl.BlockSpec((seq, 2 * HEAD_PAD), lambda b, hp, qi: (b, hp)),
                  pl.BlockSpec((LANES, seq), lambda b, hp, qi: (hp, b))],
        out_specs=pl.BlockSpec((t, LANES), lambda b, hp, qi: (b * nq + qi, hp)),
        out_shape=jax.ShapeDtypeStruct((batch * seq, MLA_HEADS * V_HEAD), BF16),
        scratch_shapes=[pltpu.VMEM((2, t, t), F32), pltpu.VMEM((2, t, t), F32), pltpu.VMEM((t, t), BF16),
                        pltpu.VMEM((t, t), BF16), pltpu.VMEM((2, LANES, t), F32)],
        compiler_params=_cparams(("arbitrary", "arbitrary", "arbitrary"), 48), name="pattn",
    )(q_all, k_all, vt_all)


def _ret_log_decay():
    return jnp.log1p(-jnp.power(2.0, -5.0 - jnp.arange(RET_HEADS, dtype=F32)))


def _ret_tables(c):
    lg = _ret_log_decay()
    i = jnp.arange(c, dtype=F32)
    diff = i[:, None] - i[None, :]
    dmat = jnp.where(diff >= 0, jnp.exp(lg[:, None, None] * jnp.maximum(diff, 0.0)), 0.0)
    lane_head = lambda a: jnp.repeat(a.reshape(a.shape[0], 4, 2), RET_DV, axis=2).transpose(1, 0, 2)
    cross = lane_head(jnp.exp(lg[None, :] * (i[:, None] + 1.0)))
    kdec = lane_head(jnp.exp(lg[None, :] * (c - 1.0 - i[:, None])))
    gc = lane_head(jnp.exp(lg * c)[None, :])
    return dmat[0::2], dmat[1::2], cross, kdec, gc


def _pret_body(rq_ref, rk_ref, rv_ref, de_ref, do_ref, cross_ref, kdec_ref, gc_ref, o_ref, rout_ref, r_sc):
    c = pl.program_id(1)

    @pl.when(c == 0)
    def _():
        r_sc[...] = jnp.zeros(r_sc.shape, F32)

    n = rq_ref.shape[0]
    lane = lax.broadcasted_iota(jnp.int32, (n, LANES), 1)
    even = lane < RET_DK
    rr = lax.broadcasted_iota(jnp.int32, (LANES, LANES), 0)
    cc = lax.broadcasted_iota(jnp.int32, (LANES, LANES), 1)
    same_head = (rr < RET_DK) == (cc < RET_DV)
    for p in range(4):
        sl = slice(LANES * p, LANES * (p + 1))
        q, k, v = rq_ref[:, sl], rk_ref[:, sl], rv_ref[:, sl]
        zero = jnp.zeros_like(q)
        se = _dot_nt(jnp.where(even, q, zero), k)
        so = _dot_nt(jnp.where(even, zero, q), k)
        oe = _dot((se * de_ref[p]).astype(BF16), v)
        oo = _dot((so * do_ref[p]).astype(BF16), v)
        r = r_sc[p]
        cross = _dot(q, r.astype(BF16)) * cross_ref[p]
        o_ref[:, sl] = jnp.where(even, oe, oo) + cross
        kd = (k.astype(F32) * kdec_ref[p]).astype(BF16)
        upd = _dot_tn(kd, v)
        r_sc[p] = r * gc_ref[p] + jnp.where(same_head, upd, 0.0)

    @pl.when(c == pl.num_programs(1) - 1)
    def _():
        rout_ref[0] = r_sc[...]


def _pret(rq, rk, rv, batch, seq):
    c = RET_CHUNK
    nc = seq // c
    tabs = _ret_tables(c)
    blk = pl.BlockSpec((c, 512), lambda b, i: (b * nc + i, 0))
    return pl.pallas_call(
        _pret_body, grid=(batch, nc),
        in_specs=[blk, blk, blk, _full((4, c, c)), _full((4, c, c)), _full((4, c, LANES)), _full((4, c, LANES)),
                  _full((4, 1, LANES))],
        out_specs=[blk, pl.BlockSpec((1, 4, LANES, LANES), lambda b, i: (b, 0, 0, 0))],
        out_shape=[jax.ShapeDtypeStruct((batch * seq, 512), F32),
                   jax.ShapeDtypeStruct((batch, 4, LANES, LANES), F32)],
        scratch_shapes=[pltpu.VMEM((4, LANES, LANES), F32)],
        compiler_params=_cparams(("arbitrary", "arbitrary"), 32), name="pret",
    )(rq, rk, rv, *tabs)


def _sabs_body(q_ref, knw_ref, wpe_ref, wknt_ref, qa_ref, qcs_ref):
    q = q_ref[...].astype(F32)
    qa_ref[...] = _dot((q * knw_ref[...]).astype(BF16), wknt_ref[0]).astype(BF16)
    lane = lax.broadcasted_iota(jnp.int32, q.shape, 1)
    qpe = pltpu.roll(q, LANES - QK_NOPE, 1)
    half = QK_ROPE // 2
    swapped = jnp.where(lane < half, pltpu.roll(qpe, LANES - half, 1), -pltpu.roll(qpe, half, 1))
    keep = lane < QK_ROPE
    wpe = wpe_ref[...]
    qc = jnp.where(keep, qpe * wpe, 0.0)
    qs = jnp.where(keep, swapped * wpe, 0.0)
    qcs_ref[...] = (qc + pltpu.roll(qs, QK_ROPE, 1)).astype(BF16)


def _sabs(q_s, knw, wpe, wknt):
    n = q_s.shape[0]
    blk = pl.BlockSpec((n, HEAD_PAD), lambda h: (0, h))
    shp = jax.ShapeDtypeStruct((n, MLA_HEADS * HEAD_PAD), BF16)
    return pl.pallas_call(
        _sabs_body, grid=(MLA_HEADS,),
        in_specs=[blk, _full((1, LANES)), _full((1, LANES)), pl.BlockSpec((1, LANES, KV_LORA), lambda h: (h, 0, 0))],
        out_specs=[blk, blk], out_shape=[shp, shp],
        compiler_params=_cparams(("arbitrary",), 32), name="sabs",
    )(q_s, knw, wpe, wknt)


def _sattn_body(pt_ref, qa_ref, qcs_ref, wkn_ref, cost_ref, sint_ref, cnew_ref, pnew_ref, cosn_ref, sinn_ref,
                *rest, n_pages, dec_seq):
    lat_refs = rest[:n_pages]
    pe_refs = rest[n_pages:2 * n_pages]
    pc_ref = rest[2 * n_pages]
    cbuf, lhs_sc, xcs, m_sc, l_sc, acc_sc = rest[2 * n_pages + 1:]
    s_id = pl.program_id(0)
    j = pl.program_id(1)
    nrow = qa_ref.shape[1]
    nkn = wkn_ref.shape[0]

    @pl.when(jnp.logical_and(s_id == 0, j == 0))
    def _():
        xcs[...] = jnp.zeros(xcs.shape, BF16)
        lhs_sc[0:nkn, :] = wkn_ref[...]

    @pl.when(j == 0)
    def _():
        lhs_sc[nkn:nkn + nrow, :] = qa_ref[0]
        m_sc[...] = jnp.full(m_sc.shape, NEG_INF, F32)
        l_sc[...] = jnp.zeros(l_sc.shape, F32)
        acc_sc[...] = jnp.zeros(acc_sc.shape, F32)

    row8 = lax.broadcasted_iota(jnp.int32, (MLA_HEADS, 1), 0)

    def attend(nk, pe_t, cos_t, sin_t, mask):
        xcs[0:QK_ROPE, 0:nk] = (pe_t * cos_t).astype(BF16)
        xcs[QK_ROPE:2 * QK_ROPE, 0:nk] = (pe_t * sin_t).astype(BF16)
        pe2 = jnp.sum(pe_t * pe_t, 0, keepdims=True)
        lhs = lhs_sc[...]
        qcs = qcs_ref[0]
        parts = []
        ck = min(nk, SAMP_CHUNK)
        for c0 in range(0, nk, ck):
            c = cbuf[c0:c0 + ck, :]
            big = _dot_nt(lhs, c)
            ss = pe2[:, c0:c0 + ck] + jnp.zeros((MLA_HEADS, ck), F32)
            for h in range(MLA_HEADS):
                kh = big[QK_NOPE * h:QK_NOPE * (h + 1), :]
                ss = ss + jnp.where(row8 == h, jnp.sum(kh * kh, 0, keepdims=True), 0.0)
            rn = lax.rsqrt(ss * (1.0 / QK_HEAD) + EPS)
            sc = big[nkn:nkn + nrow, :] + _dot(qcs, xcs[:, c0:c0 + ck])
            parts.append((sc.reshape(dec_seq, MLA_HEADS, ck) * rn[None]).reshape(nrow, ck))
        s = parts[0] if len(parts) == 1 else jnp.concatenate(parts, axis=1)
        if mask is not None:
            s = jnp.where(mask, s, NEG_INF)
        m_prev = m_sc[...]
        m_new = jnp.maximum(m_prev, jnp.max(s, -1, keepdims=True))
        alpha = jnp.exp2(m_prev - m_new)
        p = jnp.exp2(s - m_new)
        l_sc[...] = alpha * l_sc[...] + jnp.sum(p, -1, keepdims=True)
        acc_sc[...] = alpha * acc_sc[...] + _dot(p.astype(BF16), cbuf[0:nk, :])
        m_sc[...] = m_new

    for i in range(n_pages):
        cbuf[PAGE * i:PAGE * (i + 1), :] = lat_refs[i][0].astype(BF16)
    pe_all = jnp.concatenate([pe_refs[i][0] for i in range(n_pages)], axis=1)
    attend(PAGE * n_pages, pe_all, cost_ref[...], sint_ref[...], None)

    @pl.when(j == pl.num_programs(1) - 1)
    def _():
        cbuf[0:PAGE, :] = cnew_ref[0].astype(BF16)
        qtok = lax.shift_right_logical(lax.broadcasted_iota(jnp.int32, (nrow, PAGE), 0), 3)
        kcol = lax.broadcasted_iota(jnp.int32, (nrow, PAGE), 1)
        attend(PAGE, pnew_ref[0], cosn_ref[...], sinn_ref[...], jnp.logical_and(kcol <= qtok, kcol < dec_seq))
        pc_ref[0] = acc_sc[...] / l_sc[...]


def _sattn(page_table, qa, qcs, wkn_t, cos_t, sin_t, cnew, pnew_t, cosn_t, sinn_t, cache_lat, cache_pe_t, dec_seq):
    n_seq, nrow, _ = qa.shape
    n_pages_total = page_table.shape[1]
    npg = SAMP_PAGES
    steps = n_pages_total // npg
    tk = npg * PAGE
    nkn = wkn_t.shape[0]
    assert MLA_HEADS == SUBLANES and nrow == dec_seq * MLA_HEADS
    qblk = pl.BlockSpec((1, nrow, LANES), lambda s, j, pt: (s, 0, 0))
    const2 = lambda shape: pl.BlockSpec(shape, lambda s, j, pt: (0, 0))
    tab = pl.BlockSpec((QK_ROPE, tk), lambda s, j, pt: (0, j))
    in_specs = [qblk, qblk, const2((nkn, KV_LORA)), tab, tab,
                pl.BlockSpec((1, PAGE, KV_LORA), lambda s, j, pt: (s, 0, 0)),
                pl.BlockSpec((1, QK_ROPE, PAGE), lambda s, j, pt: (s, 0, 0)),
                const2((QK_ROPE, PAGE)), const2((QK_ROPE, PAGE))]
    for i in range(npg):
        in_specs.append(pl.BlockSpec((1, PAGE, KV_LORA), lambda s, j, pt, i=i: (pt[s, j * npg + i], 0, 0)))
    for i in range(npg):
        in_specs.append(pl.BlockSpec((1, QK_ROPE, PAGE), lambda s, j, pt, i=i: (pt[s, j * npg + i], 0, 0)))
    gs = pltpu.PrefetchScalarGridSpec(
        num_scalar_prefetch=1, grid=(n_seq, steps), in_specs=in_specs,
        out_specs=pl.BlockSpec((1, nrow, LANES), lambda s, j, pt: (s, 0, 0)),
        scratch_shapes=[pltpu.VMEM((tk, LANES), BF16), pltpu.VMEM((nkn + nrow, KV_LORA), BF16),
                        pltpu.VMEM((LANES, tk), BF16), pltpu.VMEM((nrow, 1), F32), pltpu.VMEM((nrow, 1), F32),
                        pltpu.VMEM((nrow, LANES), F32)])
    return pl.pallas_call(
        functools.partial(_sattn_body, n_pages=npg, dec_seq=dec_seq), grid_spec=gs,
        out_shape=jax.ShapeDtypeStruct((n_seq, nrow, LANES), F32),
        compiler_params=_cparams(("arbitrary", "arbitrary"), 48), name="sattn",
    )(page_table, qa, qcs, wkn_t, cos_t, sin_t, cnew, pnew_t, cosn_t, sinn_t,
      *([cache_lat] * npg), *([cache_pe_t] * npg))


def _sup_body(pc_ref, we_ref, wo_ref, o_ref):
    o_ref[...] = (_dot(pc_ref[0].astype(BF16), we_ref[0]) + _dot(pc_ref[1].astype(BF16), wo_ref[0])).astype(o_ref.dtype)


def _sup(pc_h, wve, wvo):
    n = pc_h.shape[1]
    return pl.pallas_call(
        _sup_body, grid=(MLA_HEADS // 2,),
        in_specs=[pl.BlockSpec((2, n, KV_LORA), lambda p: (p, 0, 0)),
                  pl.BlockSpec((1, KV_LORA, LANES), lambda p: (p, 0, 0)),
                  pl.BlockSpec((1, KV_LORA, LANES), lambda p: (p, 0, 0))],
        out_specs=pl.BlockSpec((n, LANES), lambda p: (0, p)),
        out_shape=jax.ShapeDtypeStruct((n, MLA_HEADS * V_HEAD), BF16),
        compiler_params=_cparams(("arbitrary",), 32), name="sup",
    )(pc_h, wve, wvo)


def _sret_body(q_ref, k_ref, v_ref, r0_ref, d_ref, cross_ref, kdec_ref, g_ref, o_ref, r_ref):
    nt = q_ref.shape[0]
    outs = []
    for i in range(nt):
        acc = jnp.zeros(v_ref.shape[1:], F32)
        for j in range(i + 1):
            sij = jnp.sum(q_ref[i] * k_ref[j], 0, keepdims=True) * d_ref[0, i * nt + j:i * nt + j + 1, :]
            acc = acc + sij * v_ref[j]
        outs.append(acc)

    def body(d, carry):
        r = r0_ref[0, d]
        row = pl.ds(d, 1)
        rn = g_ref[0] * r
        new = []
        for i in range(nt):
            new.append(carry[i] + (q_ref[i, row, :] * cross_ref[0, i:i + 1, :]) * r)
            rn = rn + (k_ref[i, row, :] * kdec_ref[0, i:i + 1, :]) * v_ref[i]
        r_ref[0, d] = rn
        return tuple(new)

    outs = lax.fori_loop(0, RET_DK, body, tuple(outs))
    for i in range(nt):
        o_ref[i] = outs[i]


def _sret(q_t, k_t, v_t, r0_t, dec_seq):
    n_seq = q_t.shape[2]
    lg = _ret_log_decay()
    i = jnp.arange(dec_seq, dtype=F32)
    lanes = jnp.ones((1, 1, n_seq), F32)
    diff = i[:, None] - i[None, :]
    dmat = jnp.where(diff >= 0, jnp.exp(lg[:, None, None] * jnp.maximum(diff, 0.0)), 0.0)
    dtab = dmat.reshape(RET_HEADS, dec_seq * dec_seq, 1) * lanes
    cross = jnp.exp(lg[:, None] * (i[None, :] + 1.0))[:, :, None] * lanes
    kdec = jnp.exp(lg[:, None] * (dec_seq - 1.0 - i[None, :]))[:, :, None] * lanes
    gdec = jnp.exp(lg * dec_seq)[:, None, None] * lanes
    blk = pl.BlockSpec((dec_seq, RET_DK, n_seq), lambda h: (0, h, 0))
    rblk = pl.BlockSpec((1, RET_DK, RET_DV, n_seq), lambda h: (h, 0, 0, 0))
    tab = lambda r: pl.BlockSpec((1, r, n_seq), lambda h: (h, 0, 0))
    return pl.pallas_call(
        _sret_body, grid=(RET_HEADS,),
        in_specs=[blk, blk, blk, rblk, tab(dec_seq * dec_seq), tab(dec_seq), tab(dec_seq), tab(1)],
        out_specs=[blk, rblk],
        out_shape=[jax.ShapeDtypeStruct((dec_seq, RET_HEADS * RET_DV, n_seq), F32),
                   jax.ShapeDtypeStruct((RET_HEADS, RET_DK, RET_DV, n_seq), F32)],
        compiler_params=_cparams(("arbitrary",), 32), name="sret",
    )(q_t, k_t, v_t, r0_t, dtab, cross, kdec, gdec)


def _post_body(x_ref, mla_ref, reto_ref, rg_ref, rnw_ref, wo1_ref, wo2_ref, fnw_ref, wpq_ref, k1_ref, k2_ref,
               h_ref, hn_ref, s1_ref, s2_ref):
    tm = x_ref.shape[0]
    lane = lax.broadcasted_iota(jnp.int32, (tm, LANES), 1)
    lo = lane < RET_DV

    def group(a):
        s_lo = jnp.sum(jnp.where(lo, a, 0.0), -1, keepdims=True)
        s_hi = jnp.sum(jnp.where(lo, 0.0, a), -1, keepdims=True)
        return jnp.where(lo, s_lo, s_hi)

    parts = []
    for p in range(4):
        sl = slice(LANES * p, LANES * (p + 1))
        o = reto_ref[:, sl]
        d = o - group(o) * (1.0 / RET_DV)
        y = d * lax.rsqrt(group(d * d) * (1.0 / RET_DV) + EPS) * rnw_ref[:, sl]
        rg = rg_ref[:, sl]
        parts.append((rg * jax.nn.sigmoid(rg) * y).astype(BF16))
    ret_out = jnp.concatenate(parts, axis=1)
    h = x_ref[...] + _dot(mla_ref[...], wo1_ref[...]) + _dot(ret_out, wo2_ref[...])
    h_ref[...] = h
    hn = _rms(h, fnw_ref[...], D_MODEL).astype(BF16)
    hn_ref[...] = hn
    pq = _dot(hn, wpq_ref[...])
    half = PEER_QDIM // 2
    for hd in range(PEER_HEADS):
        q1 = pq[:, PEER_QDIM * hd:PEER_QDIM * hd + half].astype(BF16)
        q2 = pq[:, PEER_QDIM * hd + half:PEER_QDIM * (hd + 1)].astype(BF16)
        s1_ref[hd] = _dot_nt(k1_ref[...], q1)
        s2_ref[hd] = _dot_nt(k2_ref[...], q2)


def _post(x_all, mla, reto, rg, wts):
    n = x_all.shape[0]
    tm = MIX_TM
    row = lambda w: pl.BlockSpec((tm, w), lambda i: (i, 0))
    sblk = pl.BlockSpec((PEER_HEADS, PEER_KEYS, tm), lambda i: (0, 0, i))
    half = PEER_QDIM // 2
    return pl.pallas_call(
        _post_body, grid=(n // tm,),
        in_specs=[row(D_MODEL), row(512), row(512), row(512), _full((1, 512)), _full((512, D_MODEL)),
                  _full((512, D_MODEL)), _full((1, D_MODEL)), _full((D_MODEL, PEER_HEADS * PEER_QDIM)),
                  _full((PEER_KEYS, half)), _full((PEER_KEYS, half))],
        out_specs=[row(D_MODEL), row(D_MODEL), sblk, sblk],
        out_shape=[jax.ShapeDtypeStruct((n, D_MODEL), F32), jax.ShapeDtypeStruct((n, D_MODEL), BF16),
                   jax.ShapeDtypeStruct((PEER_HEADS, PEER_KEYS, n), F32),
                   jax.ShapeDtypeStruct((PEER_HEADS, PEER_KEYS, n), F32)],
        compiler_params=_cparams(("arbitrary",), 48), name="post",
    )(x_all, mla, reto, rg, wts["rnw"], wts["wo1"], wts["wo2"], wts["fnw"], wts["wpq"], wts["k1"], wts["k2"])


def _topk_body(s1_ref, s2_ref, e1_ref, e2_ref, g_ref):
    kk = PEER_TOPK
    shift = kk.bit_length() - 1
    assert kk == 1 << shift
    tt = s1_ref.shape[2]
    row16 = lax.broadcasted_iota(jnp.int32, (kk, tt), 0)

    def take_top(x):
        nrow = x.shape[0]
        row = lax.broadcasted_iota(jnp.int32, x.shape, 0)
        vals = jnp.zeros((kk, tt), F32)
        idxs = jnp.zeros((kk, tt), jnp.int32)
        for r in range(kk):
            m = jnp.max(x, axis=0, keepdims=True)
            idx = jnp.min(jnp.where(x == m, row, nrow), axis=0, keepdims=True)
            vals = jnp.where(row16 == r, m, vals)
            idxs = jnp.where(row16 == r, idx, idxs)
            x = jnp.where(row == idx, NEG_INF, x)
        return vals, idxs

    def pick(table, sel):
        return jnp.sum(jnp.where(row16 == sel, table, 0), axis=0, keepdims=True)

    v1, i1 = take_top(s1_ref[0])
    v2, i2 = take_top(s2_ref[0])
    cand = jnp.concatenate([v1[a:a + 1, :] + v2 for a in range(kk)], axis=0)
    vs, ic = take_top(cand)
    e1 = jnp.zeros((kk, tt), jnp.int32)
    e2 = jnp.zeros((kk, tt), jnp.int32)
    for r in range(kk):
        f = ic[r:r + 1, :]
        e1 = jnp.where(row16 == r, pick(i1, lax.shift_right_logical(f, shift)), e1)
        e2 = jnp.where(row16 == r, pick(i2, f & (kk - 1)), e2)
    p = jnp.exp(vs - jnp.max(vs, axis=0, keepdims=True))
    g_ref[...] = p / jnp.sum(p, axis=0, keepdims=True)
    e1_ref[...] = e1.astype(F32)
    e2_ref[...] = e2.astype(F32)


def _topk(s1t, s2t):
    n = s1t.shape[2]
    tt = TOPK_TT
    sblk = pl.BlockSpec((1, PEER_KEYS, tt), lambda i, h: (h, 0, i))
    oblk = pl.BlockSpec((PEER_TOPK, tt), lambda i, h: (h, i))
    shp = jax.ShapeDtypeStruct((PEER_HEADS * PEER_TOPK, n), F32)
    return pl.pallas_call(
        _topk_body, grid=(n // tt, PEER_HEADS), in_specs=[sblk, sblk], out_specs=[oblk, oblk, oblk],
        out_shape=[shp, shp, shp],
        compiler_params=_cparams(("arbitrary", "arbitrary"), 32), name="topk",
    )(s1t, s2t)


def _peer_body(e1_ref, e2_ref, g_ref, hn_ref, h_ref, ut_ref, v_ref, o_ref, e1_sc, e2_sc, g_sc, w_sc, acc_sc):
    eb = pl.program_id(1)
    tn = hn_ref.shape[0]

    @pl.when(eb == 0)
    def _():
        e1_sc[...] = e1_ref[...].T
        e2_sc[...] = e2_ref[...].T
        g_sc[...] = g_ref[...].T
        acc_sc[...] = jnp.zeros(acc_sc.shape, F32)
        iot = lax.broadcasted_iota(jnp.int32, (PEER_KEYS, LANES), 0).astype(F32)

        def body(grp, c):
            rows = pl.ds(pl.multiple_of(grp * SUBLANES, SUBLANES), SUBLANES)
            e1g, e2g, gg = e1_sc[rows, :], e2_sc[rows, :], g_sc[rows, :]
            for t in range(SUBLANES):
                a = jnp.where(e1g[t:t + 1, :] == iot, 1.0, 0.0).astype(BF16)
                b = jnp.where(e2g[t:t + 1, :] == iot, gg[t:t + 1, :], 0.0).astype(BF16)
                w_sc[grp, pl.ds(t, PEER_KEYS, stride=SUBLANES), :] = _dot_nt(a, b)
            return c

        lax.fori_loop(0, tn // SUBLANES, body, 0)

    act = _dot(hn_ref[...], ut_ref[...])
    gl = 0.5 * act * (1.0 + lax.erf(act * INV_SQRT2))
    zs = []
    for c in range(PEER_EB // PEER_KEYS):
        e1 = eb * (PEER_EB // PEER_KEYS) + c
        wsl = w_sc[:, pl.ds(pl.multiple_of(e1 * SUBLANES, SUBLANES), SUBLANES), :].reshape(tn, LANES)
        zs.append((gl[:, LANES * c:LANES * (c + 1)] * wsl).astype(BF16))
    acc_sc[...] += _dot(jnp.concatenate(zs, axis=1), v_ref[...])

    @pl.when(eb == pl.num_programs(1) - 1)
    def _():
        o_ref[...] = h_ref[...] + acc_sc[...]


def _peer(e1t, e2t, gt, hn, h, ut, v):
    n = hn.shape[0]
    tn = PEER_TN
    n_exp = ut.shape[1]
    jn = PEER_HEADS * PEER_TOPK
    sel = pl.BlockSpec((jn, tn), lambda i, e: (0, i))
    row = pl.BlockSpec((tn, D_MODEL), lambda i, e: (i, 0))
    wblk = pl.BlockSpec((PEER_EB, D_MODEL), lambda i, e: (e, 0))
    return pl.pallas_call(
        _peer_body, grid=(n // tn, n_exp // PEER_EB),
        in_specs=[sel, sel, sel, row, row, pl.BlockSpec((D_MODEL, PEER_EB), lambda i, e: (0, e)), wblk],
        out_specs=row, out_shape=jax.ShapeDtypeStruct((n, D_MODEL), F32),
        scratch_shapes=[pltpu.VMEM((tn, jn), F32), pltpu.VMEM((tn, jn), F32), pltpu.VMEM((tn, jn), F32),
                        pltpu.VMEM((tn // SUBLANES, PEER_KEYS * SUBLANES, LANES), F32),
                        pltpu.VMEM((tn, D_MODEL), F32)],
        compiler_params=_cparams(("arbitrary", "arbitrary"), 56), name="peer",
    )(e1t, e2t, gt, hn, h, ut, v)


def _pad_last(a, n):
    return jnp.pad(a, [(0, 0)] * (a.ndim - 1) + [(0, n - a.shape[-1])])


def _layer_weights(attn_norm_w, w_in, q_a_norm_w, w_uq, q_norm_w, kv_a_norm_w, w_ukv, k_norm_w,
                   ret_norm_w, w_o, ffn_norm_w, w_peer_q, sub_keys1, sub_keys2, peer_u, peer_v):
    o = np.cumsum([0, Q_LORA, KV_LORA, QK_ROPE, 512, 512, 512, 512])
    zeros = lambda w: jnp.zeros((D_MODEL, w), F32)
    win = jnp.concatenate([w_in[:, o[0]:o[2]], zeros(QK_NOPE), w_in[:, o[2]:o[3]], zeros(LANES - QK_HEAD),
                           w_in[:, o[3]:o[7]]], axis=1)
    wuq = _pad_last(w_uq.reshape(Q_LORA, MLA_HEADS, QK_HEAD), HEAD_PAD).reshape(Q_LORA, MLA_HEADS * HEAD_PAD)
    ukv = w_ukv.reshape(KV_LORA, MLA_HEADS, QK_NOPE + V_HEAD)
    ukn, uv = ukv[:, :, :QK_NOPE], ukv[:, :, QK_NOPE:]
    knw = _pad_last(k_norm_w, LANES)[None]
    wknt = _pad_last(ukn, HEAD_PAD).transpose(1, 2, 0)
    uvp = uv.reshape(KV_LORA, 4, 2, V_HEAD)
    zv = jnp.zeros((KV_LORA, 4, V_HEAD), F32)
    wve = jnp.concatenate([uvp[:, :, 0], zv], -1).transpose(1, 0, 2)
    wvo = jnp.concatenate([zv, uvp[:, :, 1]], -1).transpose(1, 0, 2)
    return dict(
        anw=attn_norm_w[None], win=win.astype(BF16), qanw=q_a_norm_w[None], wuq=wuq.astype(BF16),
        qnw=(_pad_last(q_norm_w, LANES) * (QK_HEAD ** -0.5 * LOG2E))[None], kvnw=kv_a_norm_w[None],
        wukn=_pad_last(ukn, HEAD_PAD).reshape(KV_LORA, MLA_HEADS * HEAD_PAD).astype(BF16),
        wuv=uv.reshape(KV_LORA, MLA_HEADS * V_HEAD).astype(BF16), knw=knw,
        wpe=_pad_last(k_norm_w[QK_NOPE:], LANES)[None], wknt=wknt.astype(BF16),
        wkn_t=ukn.transpose(1, 2, 0).reshape(MLA_HEADS * QK_NOPE, KV_LORA).astype(BF16), wve=wve.astype(BF16), wvo=wvo.astype(BF16),
        rnw=ret_norm_w[None], wo1=w_o[:512].astype(BF16), wo2=w_o[512:].astype(BF16), fnw=ffn_norm_w[None],
        wpq=w_peer_q.astype(BF16), k1=sub_keys1.astype(BF16), k2=sub_keys2.astype(BF16),
        put=peer_u.T.astype(BF16), pv=peer_v.astype(BF16))


def _rope_tables(pos):
    posf = pos.astype(F32)[:, None]
    n = pos.shape[0]
    hq = QK_ROPE // 2
    ang = posf * (ROPE_THETA ** (-jnp.arange(hq, dtype=F32) / hq))[None, :]
    c, s = jnp.cos(ang), jnp.sin(ang)
    one, zero = jnp.ones((n, QK_NOPE), F32), jnp.zeros((n, QK_NOPE), F32)
    cq = jnp.concatenate([one, c, c, one[:, :LANES - QK_HEAD]], 1)
    sq = jnp.concatenate([zero, -s, s, zero[:, :LANES - QK_HEAD]], 1)
    hr = RET_DK // 2
    angr = posf * (ROPE_THETA ** (-jnp.arange(hr, dtype=F32) / hr))[None, :]
    c2, s2 = jnp.cos(angr), jnp.sin(angr)
    cr = jnp.concatenate([c2, c2, c2, c2], 1)
    sr = jnp.concatenate([-s2, s2, -s2, s2], 1)
    return dict(cq=cq, sq=sq, cr=cr, sr=sr, cos_t=jnp.concatenate([c, c], 1).T, sin_t=jnp.concatenate([s, s], 1).T)


def kernel(x_prompt, x_sample, cache_kv_latent, cache_k_rope, state_ret, page_table, attn_norm_w, w_in, q_a_norm_w, w_uq, q_norm_w, kv_a_norm_w, w_ukv, k_norm_w, ret_norm_w, w_o, ffn_norm_w, w_peer_q, peer_sub_keys1, peer_sub_keys2, peer_u, peer_v):
    batch, seq, _ = x_prompt.shape
    n_seq, dec_seq, _ = x_sample.shape
    depth = attn_norm_w.shape[0]
    past = page_table.shape[1] * PAGE
    n_p, n_s = batch * seq, n_seq * dec_seq
    nrow = MLA_HEADS * dec_seq

    pos = jnp.concatenate([jnp.tile(jnp.arange(seq), batch), jnp.tile(past + jnp.arange(dec_seq), n_seq)])
    tabs = _rope_tables(pos)
    past_tabs = _rope_tables(jnp.arange(past))
    new_tabs = _rope_tables(past + jnp.arange(PAGE))

    h_all = jnp.concatenate([x_prompt.reshape(n_p, D_MODEL), x_sample.reshape(n_s, D_MODEL)], 0)
    outs = [[] for _ in range(6)]
    for l in range(depth):
        wts = _layer_weights(attn_norm_w[l], w_in[l], q_a_norm_w[l], w_uq[l], q_norm_w[l], kv_a_norm_w[l], w_ukv[l],
                             k_norm_w[l], ret_norm_w[l], w_o[l], ffn_norm_w[l], w_peer_q[l], peer_sub_keys1[l],
                             peer_sub_keys2[l], peer_u[l], peer_v[l])
        q, k, vt, ckv, kpe_t, rq, rk, rv, rg = _mixer(h_all, wts, tabs)
        kpe_s = kpe_t[:, n_p:].reshape(QK_ROPE, n_seq, dec_seq)

        mla_p = _pattn(q, k, vt, batch, seq)
        reto_p, r_p = _pret(rq, rk, rv, batch, seq)
        r_p = jnp.stack([r_p[:, :, :RET_DK, :RET_DV], r_p[:, :, RET_DK:, RET_DV:]], 2).reshape(batch, RET_HEADS, RET_DK, RET_DV)

        qa, qcs = _sabs(q[n_p:], wts["knw"], wts["wpe"], wts["wknt"])
        cnew = jnp.pad(ckv[n_p:].reshape(n_seq, dec_seq, KV_LORA), ((0, 0), (0, PAGE - dec_seq), (0, 0)))
        pnew_t = jnp.pad(kpe_s.transpose(1, 0, 2), ((0, 0), (0, 0), (0, PAGE - dec_seq)))
        pc = _sattn(page_table, qa.reshape(n_seq, nrow, HEAD_PAD), qcs.reshape(n_seq, nrow, HEAD_PAD), wts["wkn_t"],
                    past_tabs["cos_t"], past_tabs["sin_t"], cnew, pnew_t, new_tabs["cos_t"], new_tabs["sin_t"],
                    cache_kv_latent[l], cache_k_rope[l].transpose(0, 2, 1), dec_seq)
        pc_h = pc.reshape(n_seq, dec_seq, MLA_HEADS, KV_LORA).transpose(2, 0, 1, 3).reshape(MLA_HEADS, n_s, KV_LORA)
        mla_s = _sup(pc_h, wts["wve"], wts["wvo"])
        seq_last = lambda a: a[n_p:].astype(F32).reshape(n_seq, dec_seq, 512).transpose(1, 2, 0)
        reto_s, r_s = _sret(seq_last(rq), seq_last(rk), seq_last(rv), state_ret[l].astype(F32).transpose(1, 2, 3, 0),
                            dec_seq)
        reto_s = reto_s.transpose(2, 0, 1).reshape(n_s, 512)
        r_s = r_s.transpose(3, 0, 1, 2)

        mla = jnp.concatenate([mla_p, mla_s], 0)
        reto = jnp.concatenate([reto_p, reto_s], 0)
        h_mid, hn, s1t, s2t = _post(h_all, mla, reto, rg, wts)
        e1t, e2t, gt = _topk(s1t, s2t)
        h_all = _peer(e1t, e2t, gt, hn, h_mid, wts["put"], wts["pv"])

        outs[0].append(ckv[:n_p].reshape(batch, seq, KV_LORA))
        outs[1].append(kpe_t[:, :n_p].reshape(QK_ROPE, batch, seq).transpose(1, 2, 0))
        outs[2].append(r_p)
        outs[3].append(ckv[n_p:].reshape(n_seq, dec_seq, KV_LORA))
        outs[4].append(kpe_s.transpose(1, 2, 0))
        outs[5].append(r_s)
    return (h_all[:n_p].reshape(batch, seq, D_MODEL), h_all[n_p:].reshape(n_seq, dec_seq, D_MODEL),
            *[jnp.stack(o) for o in outs])
```

```python
import functools

import numpy as np
import jax
import jax.numpy as jnp
from jax import lax
from jax.experimental import pallas as pl
from jax.experimental.pallas import tpu as pltpu

F32 = jnp.float32
BF16 = jnp.bfloat16

D_MODEL = 1024
PAGE = 128
MLA_HEADS = 8
QK_NOPE = 64
QK_ROPE = 32
QK_HEAD = QK_NOPE + QK_ROPE
V_HEAD = 64
Q_LORA = 256
KV_LORA = 128
RET_HEADS = 8
RET_DK = 64
RET_DV = 64
RET_CHUNK = 128
ROPE_THETA = 10000.0
PEER_KEYS = 128
PEER_HEADS = 8
PEER_QDIM = 256
PEER_TOPK = 16
EPS = 1e-6
LANES = 128
SUBLANES = 8
HEAD_PAD = 128
NEG_INF = float("-inf")
INV_SQRT2 = 0.7071067811865476
LOG2E = 1.4426950408889634

MIX_TM = 256
ATT_T = 512
ATT_CH = 32
SAMP_PAGES = 32
SAMP_CHUNK = 512
TOPK_TT = 256
PEER_TN = 512
PEER_EB = 1024


def _cparams(sem, vmem_mb):
    return pltpu.CompilerParams(dimension_semantics=sem, vmem_limit_bytes=vmem_mb << 20)


def _full(shape):
    n = len(shape)
    return pl.BlockSpec(shape, lambda *_: (0,) * n)


def _rms(x, w, n):
    return x * lax.rsqrt(jnp.sum(x * x, -1, keepdims=True) * (1.0 / n) + EPS) * w


def _dot(a, b):
    return jnp.dot(a, b, preferred_element_type=F32)


def _dot_nt(a, b):
    return lax.dot_general(a, b, (((1,), (1,)), ((), ())), preferred_element_type=F32)


def _dot_tn(a, b):
    return lax.dot_general(a, b, (((0,), (0,)), ((), ())), preferred_element_type=F32)


def _mixer_body(x_ref, anw_ref, win_ref, qanw_ref, wuq_ref, qnw_ref, kvnw_ref, wukn_ref, wuv_ref, knw_ref,
                cq_ref, sq_ref, cr_ref, sr_ref,
                q_ref, k_ref, vt_ref, ckv_ref, kpe_ref, rq_ref, rk_ref, rv_ref, rg_ref):
    x = x_ref[...]
    xn = _rms(x, anw_ref[...], D_MODEL)
    proj = _dot(xn.astype(BF16), win_ref[...])
    lane = lax.broadcasted_iota(jnp.int32, (x.shape[0], LANES), 1)
    cq, sq, cr, sr = cq_ref[...], sq_ref[...], cr_ref[...], sr_ref[...]
    lo_q = lane < QK_NOPE + QK_ROPE // 2
    lo_r = (lane & (RET_DK - 1)) < RET_DK // 2

    def rope_q(v):
        return v * cq + jnp.where(lo_q, pltpu.roll(v, LANES - 16, 1), pltpu.roll(v, 16, 1)) * sq

    def rope_r(v):
        return v * cr + jnp.where(lo_r, pltpu.roll(v, LANES - 32, 1), pltpu.roll(v, 32, 1)) * sr

    q_lat = _rms(proj[:, 0:256], qanw_ref[...], Q_LORA)
    qf = _dot(q_lat.astype(BF16), wuq_ref[...])
    ckv = _rms(proj[:, 256:384], kvnw_ref[...], KV_LORA)
    ckv_ref[...] = ckv
    kslot = proj[:, 384:512]
    kpe_ref[...] = kslot.T[QK_NOPE:QK_HEAD, :]
    cb = ckv.astype(BF16)
    knf = _dot(cb, wukn_ref[...])
    vt_ref[...] = _dot(cb, wuv_ref[...]).T.astype(BF16)
    qnw, knw = qnw_ref[...], knw_ref[...]
    for h in range(MLA_HEADS):
        sl = slice(HEAD_PAD * h, HEAD_PAD * (h + 1))
        q_ref[:, sl] = rope_q(_rms(qf[:, sl], qnw, QK_HEAD)).astype(BF16)
        k_ref[:, sl] = rope_q(_rms(knf[:, sl] + kslot, knw, QK_HEAD)).astype(BF16)
    for p in range(4):
        sl = slice(LANES * p, LANES * (p + 1))
        rq_ref[:, sl] = rope_r(proj[:, 512 + LANES * p:512 + LANES * (p + 1)]).astype(BF16)
        rk_ref[:, sl] = (rope_r(proj[:, 1024 + LANES * p:1024 + LANES * (p + 1)]) * (RET_DK ** -0.5)).astype(BF16)
    rv_ref[...] = proj[:, 1536:2048].astype(BF16)
    rg_ref[...] = proj[:, 2048:2560]


def _mixer(x_all, wts, tabs):
    n = x_all.shape[0]
    tm = MIX_TM
    row = lambda w: pl.BlockSpec((tm, w), lambda i: (i, 0))
    ins = [row(D_MODEL), _full((1, D_MODEL)), _full((D_MODEL, 2560)), _full((1, Q_LORA)), _full((Q_LORA, 1024)),
           _full((1, LANES)), _full((1, KV_LORA)), _full((KV_LORA, 1024)), _full((KV_LORA, 512)), _full((1, LANES)),
           row(LANES), row(LANES), row(LANES), row(LANES)]
    outs = [row(1024), row(1024), pl.BlockSpec((512, tm), lambda i: (0, i)), row(KV_LORA),
            pl.BlockSpec((QK_ROPE, tm), lambda i: (0, i)), row(512),
            row(512), row(512), row(512)]
    shapes = [jax.ShapeDtypeStruct((n, 1024), BF16), jax.ShapeDtypeStruct((n, 1024), BF16),
              jax.ShapeDtypeStruct((512, n), BF16), jax.ShapeDtypeStruct((n, KV_LORA), F32),
              jax.ShapeDtypeStruct((QK_ROPE, n), F32), jax.ShapeDtypeStruct((n, 512), BF16),
              jax.ShapeDtypeStruct((n, 512), BF16), jax.ShapeDtypeStruct((n, 512), BF16),
              jax.ShapeDtypeStruct((n, 512), F32)]
    return pl.pallas_call(
        _mixer_body, grid=(n // tm,), in_specs=ins, out_specs=outs, out_shape=shapes,
        compiler_params=_cparams(("arbitrary",), 48), name="mixer",
    )(x_all, wts["anw"], wts["win"], wts["qanw"], wts["wuq"], wts["qnw"], wts["kvnw"], wts["wukn"], wts["wuv"],
      wts["knw"], tabs["cq"], tabs["sq"], tabs["cr"], tabs["sr"])


def _pattn_body(q_ref, k_ref, vt_ref, o_ref, st0, st1, pt0, pt1, acc_sc):
    qi = pl.program_id(2)
    t = ATT_T
    ch = ATT_CH
    krow = lax.broadcasted_iota(jnp.int32, (ch, t), 0)
    qcol = lax.broadcasted_iota(jnp.int32, (ch, t), 1)
    acc_sc[...] = jnp.zeros(acc_sc.shape, F32)
    st_sc, pt_sc = (st0, st1), (pt0, pt1)

    def scores(j, slot):
        start = pl.multiple_of(j * t, t)
        for hh in range(2):
            lsl = slice(HEAD_PAD * hh, HEAD_PAD * (hh + 1))
            st_sc[hh][slot] = _dot_nt(k_ref[pl.ds(start, t), lsl], q_ref[:, lsl])

    def update(j, slot, carry, masked):
        start = pl.multiple_of(j * t, t)
        out = []
        for hh in range(2):
            m_prev, l_prev = carry[2 * hh], carry[2 * hh + 1]

            def chunk(r):
                blk = st_sc[hh][slot, ch * r:ch * (r + 1), :]
                return jnp.where(krow + ch * r <= qcol, blk, NEG_INF) if masked else blk

            mc = chunk(0)
            for r in range(1, t // ch):
                mc = jnp.maximum(mc, chunk(r))
            m_new = jnp.maximum(m_prev, jnp.max(mc, 0, keepdims=True))
            alpha = jnp.exp2(m_prev - m_new)
            ls = jnp.zeros((ch, t), F32)
            for r in range(t // ch):
                p = jnp.exp2(chunk(r) - m_new)
                ls = ls + p
                pt_sc[hh][ch * r:ch * (r + 1), :] = p.astype(BF16)
            acc_sc[hh] = alpha * acc_sc[hh] + _dot(vt_ref[:, pl.ds(start, t)], pt_sc[hh][...])
            out += [m_new, alpha * l_prev + jnp.sum(ls, 0, keepdims=True)]
        return tuple(out)

    def step(j, slot, carry):
        scores(j + 1, 1 - slot)
        return update(j, slot, carry, False)

    def pair(i, carry):
        return step(2 * i + 1, 1, step(2 * i, 0, carry))

    scores(0, 0)
    init = (jnp.full((1, t), NEG_INF, F32), jnp.zeros((1, t), F32)) * 2
    carry = lax.fori_loop(0, qi // 2, pair, init)
    _, l0, _, l1 = lax.cond(qi % 2 == 1,
                            lambda c: update(qi, 1, step(qi - 1, 0, c), True),
                            lambda c: update(qi, 0, c, True), carry)
    drow = lax.broadcasted_iota(jnp.int32, (LANES, t), 0)
    o_ref[...] = jnp.where(drow < V_HEAD, acc_sc[0] / l0, acc_sc[1] / l1).T.astype(o_ref.dtype)


def _pattn(q_all, k_all, vt_all, batch, seq):
    t = ATT_T
    nq = seq // t
    return pl.pallas_call(
        _pattn_body, grid=(batch, MLA_HEADS // 2, nq),
        in_specs=[pl.BlockSpec((t, 2 * HEAD_PAD), lambda b, hp, qi: (b * nq + qi, hp)),
                  pl.BlockSpec((seq, 2 * HEAD_PAD), lambda b, hp, qi: (b, hp)),
                  pl.BlockSpec((LANES, seq), lambda b, hp, qi: (hp, b))],
        out_specs=pl.BlockSpec((t, LANES), lambda b, hp, qi: (b * nq + qi, hp)),
        out_shape=jax.ShapeDtypeStruct((batch * seq, MLA_HEADS * V_HEAD), BF16),
        scratch_shapes=[pltpu.VMEM((2, t, t), F32), pltpu.VMEM((2, t, t), F32), pltpu.VMEM((t, t), BF16),
                        pltpu.VMEM((t, t), BF16), pltpu.VMEM((2, LANES, t), F32)],
        compiler_params=_cparams(("arbitrary", "arbitrary", "arbitrary"), 48), name="pattn",
    )(q_all, k_all, vt_all)


def _ret_log_decay():
    return jnp.log1p(-jnp.power(2.0, -5.0 - jnp.arange(RET_HEADS, dtype=F32)))


def _ret_tables(c):
    lg = _ret_log_decay()
    i = jnp.arange(c, dtype=F32)
    diff = i[:, None] - i[None, :]
    dmat = jnp.where(diff >= 0, jnp.exp(lg[:, None, None] * jnp.maximum(diff, 0.0)), 0.0)
    lane_head = lambda a: jnp.repeat(a.reshape(a.shape[0], 4, 2), RET_DV, axis=2).transpose(1, 0, 2)
    cross = lane_head(jnp.exp(lg[None, :] * (i[:, None] + 1.0)))
    kdec = lane_head(jnp.exp(lg[None, :] * (c - 1.0 - i[:, None])))
    gc = lane_head(jnp.exp(lg * c)[None, :])
    return dmat[0::2], dmat[1::2], cross, kdec, gc


def _pret_body(rq_ref, rk_ref, rv_ref, de_ref, do_ref, cross_ref, kdec_ref, gc_ref, o_ref, rout_ref, r_sc):
    c = pl.program_id(1)

    @pl.when(c == 0)
    def _():
        r_sc[...] = jnp.zeros(r_sc.shape, F32)

    n = rq_ref.shape[0]
    lane = lax.broadcasted_iota(jnp.int32, (n, LANES), 1)
    even = lane < RET_DK
    rr = lax.broadcasted_iota(jnp.int32, (LANES, LANES), 0)
    cc = lax.broadcasted_iota(jnp.int32, (LANES, LANES), 1)
    same_head = (rr < RET_DK) == (cc < RET_DV)
    for p in range(4):
        sl = slice(LANES * p, LANES * (p + 1))
        q, k, v = rq_ref[:, sl], rk_ref[:, sl], rv_ref[:, sl]
        zero = jnp.zeros_like(q)
        se = _dot_nt(jnp.where(even, q, zero), k)
        so = _dot_nt(jnp.where(even, zero, q), k)
        oe = _dot((se * de_ref[p]).astype(BF16), v)
        oo = _dot((so * do_ref[p]).astype(BF16), v)
        r = r_sc[p]
        cross = _dot(q, r.astype(BF16)) * cross_ref[p]
        o_ref[:, sl] = jnp.where(even, oe, oo) + cross
        kd = (k.astype(F32) * kdec_ref[p]).astype(BF16)
        upd = _dot_tn(kd, v)
        r_sc[p] = r * gc_ref[p] + jnp.where(same_head, upd, 0.0)

    @pl.when(c == pl.num_programs(1) - 1)
    def _():
        rout_ref[0] = r_sc[...]


def _pret(rq, rk, rv, batch, seq):
    c = RET_CHUNK
    nc = seq // c
    tabs = _ret_tables(c)
    blk = pl.BlockSpec((c, 512), lambda b, i: (b * nc + i, 0))
    return pl.pallas_call(
        _pret_body, grid=(batch, nc),
        in_specs=[blk, blk, blk, _full((4, c, c)), _full((4, c, c)), _full((4, c, LANES)), _full((4, c, LANES)),
                  _full((4, 1, LANES))],
        out_specs=[blk, pl.BlockSpec((1, 4, LANES, LANES), lambda b, i: (b, 0, 0, 0))],
        out_shape=[jax.ShapeDtypeStruct((batch * seq, 512), F32),
                   jax.ShapeDtypeStruct((batch, 4, LANES, LANES), F32)],
        scratch_shapes=[pltpu.VMEM((4, LANES, LANES), F32)],
        compiler_params=_cparams(("arbitrary", "arbitrary"), 32), name="pret",
    )(rq, rk, rv, *tabs)


def _sabs_body(q_ref, knw_ref, wpe_ref, wknt_ref, qa_ref, qcs_ref):
    q = q_ref[...].astype(F32)
    qa_ref[...] = _dot((q * knw_ref[...]).astype(BF16), wknt_ref[0]).astype(BF16)
    lane = lax.broadcasted_iota(jnp.int32, q.shape, 1)
    qpe = pltpu.roll(q, LANES - QK_NOPE, 1)
    half = QK_ROPE // 2
    swapped = jnp.where(lane < half, pltpu.roll(qpe, LANES - half, 1), -pltpu.roll(qpe, half, 1))
    keep = lane < QK_ROPE
    wpe = wpe_ref[...]
    qc = jnp.where(keep, qpe * wpe, 0.0)
    qs = jnp.where(keep, swapped * wpe, 0.0)
    qcs_ref[...] = (qc + pltpu.roll(qs, QK_ROPE, 1)).astype(BF16)


def _sabs(q_s, knw, wpe, wknt):
    n = q_s.shape[0]
    blk = pl.BlockSpec((n, HEAD_PAD), lambda h: (0, h))
    shp = jax.ShapeDtypeStruct((n, MLA_HEADS * HEAD_PAD), BF16)
    return pl.pallas_call(
        _sabs_body, grid=(MLA_HEADS,),
        in_specs=[blk, _full((1, LANES)), _full((1, LANES)), pl.BlockSpec((1, LANES, KV_LORA), lambda h: (h, 0, 0))],
        out_specs=[blk, blk], out_shape=[shp, shp],
        compiler_params=_cparams(("arbitrary",), 32), name="sabs",
    )(q_s, knw, wpe, wknt)


def _sattn_body(pt_ref, qa_ref, qcs_ref, wkn_ref, cost_ref, sint_ref, cnew_ref, pnew_ref, cosn_ref, sinn_ref,
                *rest, n_pages, dec_seq):
    lat_refs = rest[:n_pages]
    pe_refs = rest[n_pages:2 * n_pages]
    pc_ref = rest[2 * n_pages]
    cbuf, lhs_sc, xcs, m_sc, l_sc, acc_sc = rest[2 * n_pages + 1:]
    s_id = pl.program_id(0)
    j = pl.program_id(1)
    nrow = qa_ref.shape[1]
    nkn = wkn_ref.shape[0]

    @pl.when(jnp.logical_and(s_id == 0, j == 0))
    def _():
        xcs[...] = jnp.zeros(xcs.shape, BF16)
        lhs_sc[0:nkn, :] = wkn_ref[...]

    @pl.when(j == 0)
    def _():
        lhs_sc[nkn:nkn + nrow, :] = qa_ref[0]
        m_sc[...] = jnp.full(m_sc.shape, NEG_INF, F32)
        l_sc[...] = jnp.zeros(l_sc.shape, F32)
        acc_sc[...] = jnp.zeros(acc_sc.shape, F32)

    row8 = lax.broadcasted_iota(jnp.int32, (MLA_HEADS, 1), 0)

    def attend(nk, pe_t, cos_t, sin_t, mask):
        xcs[0:QK_ROPE, 0:nk] = (pe_t * cos_t).astype(BF16)
        xcs[QK_ROPE:2 * QK_ROPE, 0:nk] = (pe_t * sin_t).astype(BF16)
        pe2 = jnp.sum(pe_t * pe_t, 0, keepdims=True)
        lhs = lhs_sc[...]
        qcs = qcs_ref[0]
        parts = []
        ck = min(nk, SAMP_CHUNK)
        for c0 in range(0, nk, ck):
            c = cbuf[c0:c0 + ck, :]
            big = _dot_nt(lhs, c)
            ss = pe2[:, c0:c0 + ck] + jnp.zeros((MLA_HEADS, ck), F32)
            for h in range(MLA_HEADS):
                kh = big[QK_NOPE * h:QK_NOPE * (h + 1), :]
                ss = ss + jnp.where(row8 == h, jnp.sum(kh * kh, 0, keepdims=True), 0.0)
            rn = lax.rsqrt(ss * (1.0 / QK_HEAD) + EPS)
            sc = big[nkn:nkn + nrow, :] + _dot(qcs, xcs[:, c0:c0 + ck])
            parts.append((sc.reshape(dec_seq, MLA_HEADS, ck) * rn[None]).reshape(nrow, ck))
        s = parts[0] if len(parts) == 1 else jnp.concatenate(parts, axis=1)
        if mask is not None:
            s = jnp.where(mask, s, NEG_INF)
        m_prev = m_sc[...]
        m_new = jnp.maximum(m_prev, jnp.max(s, -1, keepdims=True))
        alpha = jnp.exp2(m_prev - m_new)
        p = jnp.exp2(s - m_new)
        l_sc[...] = alpha * l_sc[...] + jnp.sum(p, -1, keepdims=True)
        acc_sc[...] = alpha * acc_sc[...] + _dot(p.astype(BF16), cbuf[0:nk, :])
        m_sc[...] = m_new

    for i in range(n_pages):
        cbuf[PAGE * i:PAGE * (i + 1), :] = lat_refs[i][0].astype(BF16)
    pe_all = jnp.concatenate([pe_refs[i][0] for i in range(n_pages)], axis=1)
    attend(PAGE * n_pages, pe_all, cost_ref[...], sint_ref[...], None)

    @pl.when(j == pl.num_programs(1) - 1)
    def _():
        cbuf[0:PAGE, :] = cnew_ref[0].astype(BF16)
        qtok = lax.shift_right_logical(lax.broadcasted_iota(jnp.int32, (nrow, PAGE), 0), 3)
        kcol = lax.broadcasted_iota(jnp.int32, (nrow, PAGE), 1)
        attend(PAGE, pnew_ref[0], cosn_ref[...], sinn_ref[...], jnp.logical_and(kcol <= qtok, kcol < dec_seq))
        pc_ref[0] = acc_sc[...] / l_sc[...]


def _sattn(page_table, qa, qcs, wkn_t, cos_t, sin_t, cnew, pnew_t, cosn_t, sinn_t, cache_lat, cache_pe_t, dec_seq):
    n_seq, nrow, _ = qa.shape
    n_pages_total = page_table.shape[1]
    npg = SAMP_PAGES
    steps = n_pages_total // npg
    tk = npg * PAGE
    nkn = wkn_t.shape[0]
    assert MLA_HEADS == SUBLANES and nrow == dec_seq * MLA_HEADS
    qblk = pl.BlockSpec((1, nrow, LANES), lambda s, j, pt: (s, 0, 0))
    const2 = lambda shape: pl.BlockSpec(shape, lambda s, j, pt: (0, 0))
    tab = pl.BlockSpec((QK_ROPE, tk), lambda s, j, pt: (0, j))
    in_specs = [qblk, qblk, const2((nkn, KV_LORA)), tab, tab,
                pl.BlockSpec((1, PAGE, KV_LORA), lambda s, j, pt: (s, 0, 0)),
                pl.BlockSpec((1, QK_ROPE, PAGE), lambda s, j, pt: (s, 0, 0)),
                const2((QK_ROPE, PAGE)), const2((QK_ROPE, PAGE))]
    for i in range(npg):
        in_specs.append(pl.BlockSpec((1, PAGE, KV_LORA), lambda s, j, pt, i=i: (pt[s, j * npg + i], 0, 0)))
    for i in range(npg):
        in_specs.append(pl.BlockSpec((1, QK_ROPE, PAGE), lambda s, j, pt, i=i: (pt[s, j * npg + i], 0, 0)))
    gs = pltpu.PrefetchScalarGridSpec(
        num_scalar_prefetch=1, grid=(n_seq, steps), in_specs=in_specs,
        out_specs=pl.BlockSpec((1, nrow, LANES), lambda s, j, pt: (s, 0, 0)),
        scratch_shapes=[pltpu.VMEM((tk, LANES), BF16), pltpu.VMEM((nkn + nrow, KV_LORA), BF16),
                        pltpu.VMEM((LANES, tk), BF16), pltpu.VMEM((nrow, 1), F32), pltpu.VMEM((nrow, 1), F32),
                        pltpu.VMEM((nrow, LANES), F32)])
    return pl.pallas_call(
        functools.partial(_sattn_body, n_pages=npg, dec_seq=dec_seq), grid_spec=gs,
        out_shape=jax.ShapeDtypeStruct((n_seq, nrow, LANES), F32),
        compiler_params=_cparams(("arbitrary", "arbitrary"), 48), name="sattn",
    )(page_table, qa, qcs, wkn_t, cos_t, sin_t, cnew, pnew_t, cosn_t, sinn_t,
      *([cache_lat] * npg), *([cache_pe_t] * npg))


def _sup_body(pc_ref, we_ref, wo_ref, o_ref):
    o_ref[...] = (_dot(pc_ref[0].astype(BF16), we_ref[0]) + _dot(pc_ref[1].astype(BF16), wo_ref[0])).astype(o_ref.dtype)


def _sup(pc_h, wve, wvo):
    n = pc_h.shape[1]
    return pl.pallas_call(
        _sup_body, grid=(MLA_HEADS // 2,),
        in_specs=[pl.BlockSpec((2, n, KV_LORA), lambda p: (p, 0, 0)),
                  pl.BlockSpec((1, KV_LORA, LANES), lambda p: (p, 0, 0)),
                  pl.BlockSpec((1, KV_LORA, LANES), lambda p: (p, 0, 0))],
        out_specs=pl.BlockSpec((n, LANES), lambda p: (0, p)),
        out_shape=jax.ShapeDtypeStruct((n, MLA_HEADS * V_HEAD), BF16),
        compiler_params=_cparams(("arbitrary",), 32), name="sup",
    )(pc_h, wve, wvo)


def _sret_body(q_ref, k_ref, v_ref, r0_ref, d_ref, cross_ref, kdec_ref, g_ref, o_ref, r_ref):
    nt = q_ref.shape[0]
    outs = []
    for i in range(nt):
        acc = jnp.zeros(v_ref.shape[1:], F32)
        for j in range(i + 1):
            sij = jnp.sum(q_ref[i] * k_ref[j], 0, keepdims=True) * d_ref[0, i * nt + j:i * nt + j + 1, :]
            acc = acc + sij * v_ref[j]
        outs.append(acc)

    def body(d, carry):
        r = r0_ref[0, d]
        row = pl.ds(d, 1)
        rn = g_ref[0] * r
        new = []
        for i in range(nt):
            new.append(carry[i] + (q_ref[i, row, :] * cross_ref[0, i:i + 1, :]) * r)
            rn = rn + (k_ref[i, row, :] * kdec_ref[0, i:i + 1, :]) * v_ref[i]
        r_ref[0, d] = rn
        return tuple(new)

    outs = lax.fori_loop(0, RET_DK, body, tuple(outs))
    for i in range(nt):
        o_ref[i] = outs[i]


def _sret(q_t, k_t, v_t, r0_t, dec_seq):
    n_seq = q_t.shape[2]
    lg = _ret_log_decay()
    i = jnp.arange(dec_seq, dtype=F32)
    lanes = jnp.ones((1, 1, n_seq), F32)
    diff = i[:, None] - i[None, :]
    dmat = jnp.where(diff >= 0, jnp.exp(lg[:, None, None] * jnp.maximum(diff, 0.0)), 0.0)
    dtab = dmat.reshape(RET_HEADS, dec_seq * dec_seq, 1) * lanes
    cross = jnp.exp(lg[:, None] * (i[None, :] + 1.0))[:, :, None] * lanes
    kdec = jnp.exp(lg[:, None] * (dec_seq - 1.0 - i[None, :]))[:, :, None] * lanes
    gdec = jnp.exp(lg * dec_seq)[:, None, None] * lanes
    blk = pl.BlockSpec((dec_seq, RET_DK, n_seq), lambda h: (0, h, 0))
    rblk = pl.BlockSpec((1, RET_DK, RET_DV, n_seq), lambda h: (h, 0, 0, 0))
    tab = lambda r: pl.BlockSpec((1, r, n_seq), lambda h: (h, 0, 0))
    return pl.pallas_call(
        _sret_body, grid=(RET_HEADS,),
        in_specs=[blk, blk, blk, rblk, tab(dec_seq * dec_seq), tab(dec_seq), tab(dec_seq), tab(1)],
        out_specs=[blk, rblk],
        out_shape=[jax.ShapeDtypeStruct((dec_seq, RET_HEADS * RET_DV, n_seq), F32),
                   jax.ShapeDtypeStruct((RET_HEADS, RET_DK, RET_DV, n_seq), F32)],
        compiler_params=_cparams(("arbitrary",), 32), name="sret",
    )(q_t, k_t, v_t, r0_t, dtab, cross, kdec, gdec)


def _post_body(x_ref, mla_ref, reto_ref, rg_ref, rnw_ref, wo1_ref, wo2_ref, fnw_ref, wpq_ref, k1_ref, k2_ref,
               h_ref, hn_ref, s1_ref, s2_ref):
    tm = x_ref.shape[0]
    lane = lax.broadcasted_iota(jnp.int32, (tm, LANES), 1)
    lo = lane < RET_DV

    def group(a):
        s_lo = jnp.sum(jnp.where(lo, a, 0.0), -1, keepdims=True)
        s_hi = jnp.sum(jnp.where(lo, 0.0, a), -1, keepdims=True)
        return jnp.where(lo, s_lo, s_hi)

    parts = []
    for p in range(4):
        sl = slice(LANES * p, LANES * (p + 1))
        o = reto_ref[:, sl]
        d = o - group(o) * (1.0 / RET_DV)
        y = d * lax.rsqrt(group(d * d) * (1.0 / RET_DV) + EPS) * rnw_ref[:, sl]
        rg = rg_ref[:, sl]
        parts.append((rg * jax.nn.sigmoid(rg) * y).astype(BF16))
    ret_out = jnp.concatenate(parts, axis=1)
    h = x_ref[...] + _dot(mla_ref[...], wo1_ref[...]) + _dot(ret_out, wo2_ref[...])
    h_ref[...] = h
    hn = _rms(h, fnw_ref[...], D_MODEL).astype(BF16)
    hn_ref[...] = hn
    pq = _dot(hn, wpq_ref[...])
    half = PEER_QDIM // 2
    for hd in range(PEER_HEADS):
        q1 = pq[:, PEER_QDIM * hd:PEER_QDIM * hd + half].astype(BF16)
        q2 = pq[:, PEER_QDIM * hd + half:PEER_QDIM * (hd + 1)].astype(BF16)
        s1_ref[hd] = _dot_nt(k1_ref[...], q1)
        s2_ref[hd] = _dot_nt(k2_ref[...], q2)


def _post(x_all, mla, reto, rg, wts):
    n = x_all.shape[0]
    tm = MIX_TM
    row = lambda w: pl.BlockSpec((tm, w), lambda i: (i, 0))
    sblk = pl.BlockSpec((PEER_HEADS, PEER_KEYS, tm), lambda i: (0, 0, i))
    half = PEER_QDIM // 2
    return pl.pallas_call(
        _post_body, grid=(n // tm,),
        in_specs=[row(D_MODEL), row(512), row(512), row(512), _full((1, 512)), _full((512, D_MODEL)),
                  _full((512, D_MODEL)), _full((1, D_MODEL)), _full((D_MODEL, PEER_HEADS * PEER_QDIM)),
                  _full((PEER_KEYS, half)), _full((PEER_KEYS, half))],
        out_specs=[row(D_MODEL), row(D_MODEL), sblk, sblk],
        out_shape=[jax.ShapeDtypeStruct((n, D_MODEL), F32), jax.ShapeDtypeStruct((n, D_MODEL), BF16),
                   jax.ShapeDtypeStruct((PEER_HEADS, PEER_KEYS, n), F32),
                   jax.ShapeDtypeStruct((PEER_HEADS, PEER_KEYS, n), F32)],
        compiler_params=_cparams(("arbitrary",), 48), name="post",
    )(x_all, mla, reto, rg, wts["rnw"], wts["wo1"], wts["wo2"], wts["fnw"], wts["wpq"], wts["k1"], wts["k2"])


def _topk_body(s1_ref, s2_ref, e1_ref, e2_ref, g_ref):
    kk = PEER_TOPK
    shift = kk.bit_length() - 1
    assert kk == 1 << shift
    tt = s1_ref.shape[2]
    row16 = lax.broadcasted_iota(jnp.int32, (kk, tt), 0)

    def take_top(x):
        nrow = x.shape[0]
        row = lax.broadcasted_iota(jnp.int32, x.shape, 0)
        vals = jnp.zeros((kk, tt), F32)
        idxs = jnp.zeros((kk, tt), jnp.int32)
        for r in range(kk):
            m = jnp.max(x, axis=0, keepdims=True)
            idx = jnp.min(jnp.where(x == m, row, nrow), axis=0, keepdims=True)
            vals = jnp.where(row16 == r, m, vals)
            idxs = jnp.where(row16 == r, idx, idxs)
            x = jnp.where(row == idx, NEG_INF, x)
        return vals, idxs

    def pick(table, sel):
        return jnp.sum(jnp.where(row16 == sel, table, 0), axis=0, keepdims=True)

    v1, i1 = take_top(s1_ref[0])
    v2, i2 = take_top(s2_ref[0])
    cand = jnp.concatenate([v1[a:a + 1, :] + v2 for a in range(kk)], axis=0)
    vs, ic = take_top(cand)
    e1 = jnp.zeros((kk, tt), jnp.int32)
    e2 = jnp.zeros((kk, tt), jnp.int32)
    for r in range(kk):
        f = ic[r:r + 1, :]
        e1 = jnp.where(row16 == r, pick(i1, lax.shift_right_logical(f, shift)), e1)
        e2 = jnp.where(row16 == r, pick(i2, f & (kk - 1)), e2)
    p = jnp.exp(vs - jnp.max(vs, axis=0, keepdims=True))
    g_ref[...] = p / jnp.sum(p, axis=0, keepdims=True)
    e1_ref[...] = e1.astype(F32)
    e2_ref[...] = e2.astype(F32)


def _topk(s1t, s2t):
    n = s1t.shape[2]
    tt = TOPK_TT
    sblk = pl.BlockSpec((1, PEER_KEYS, tt), lambda i, h: (h, 0, i))
    oblk = pl.BlockSpec((PEER_TOPK, tt), lambda i, h: (h, i))
    shp = jax.ShapeDtypeStruct((PEER_HEADS * PEER_TOPK, n), F32)
    return pl.pallas_call(
        _topk_body, grid=(n // tt, PEER_HEADS), in_specs=[sblk, sblk], out_specs=[oblk, oblk, oblk],
        out_shape=[shp, shp, shp],
        compiler_params=_cparams(("arbitrary", "arbitrary"), 32), name="topk",
    )(s1t, s2t)


def _peer_body(e1_ref, e2_ref, g_ref, hn_ref, h_ref, ut_ref, v_ref, o_ref, e1_sc, e2_sc, g_sc, stg_sc, w_sc):
    eb = pl.program_id(1)
    tn = hn_ref.shape[0]
    pack = 2 * SUBLANES

    @pl.when(eb == 0)
    def _():
        e1_sc[...] = e1_ref[...].T
        e2_sc[...] = e2_ref[...].T
        g_sc[...] = g_ref[...].T
        o_ref[...] = h_ref[...]
        iot = lax.broadcasted_iota(jnp.int32, (PEER_KEYS, LANES), 0).astype(F32)

        def body(grp, c):
            for half in range(2):
                rows = pl.ds(pl.multiple_of(grp * pack + half * SUBLANES, SUBLANES), SUBLANES)
                e1g, e2g, gg = e1_sc[rows, :], e2_sc[rows, :], g_sc[rows, :]
                for t in range(SUBLANES):
                    a = jnp.where(e1g[t:t + 1, :] == iot, 1.0, 0.0).astype(BF16)
                    b = jnp.where(e2g[t:t + 1, :] == iot, gg[t:t + 1, :], 0.0).astype(BF16)
                    stg_sc[half, pl.ds(t, PEER_KEYS, stride=SUBLANES), :] = _dot_nt(a, b)
            lo = stg_sc[0].reshape(PEER_KEYS, SUBLANES, LANES)
            hi = stg_sc[1].reshape(PEER_KEYS, SUBLANES, LANES)
            w_sc[:, pl.ds(pl.multiple_of(grp * pack, pack), pack), :] = jnp.concatenate([lo, hi], axis=1).astype(BF16)
            return c

        lax.fori_loop(0, tn // pack, body, 0)

    act = _dot(hn_ref[...], ut_ref[...])
    gl = 0.5 * act * (1.0 + lax.erf(act * INV_SQRT2))
    zs = []
    for c in range(PEER_EB // PEER_KEYS):
        wsl = w_sc[eb * (PEER_EB // PEER_KEYS) + c].astype(F32)
        zs.append((gl[:, LANES * c:LANES * (c + 1)] * wsl).astype(BF16))
    o_ref[...] += _dot(jnp.concatenate(zs, axis=1), v_ref[...])


def _peer(e1t, e2t, gt, hn, h, ut, v):
    n = hn.shape[0]
    tn = PEER_TN
    n_exp = ut.shape[1]
    jn = PEER_HEADS * PEER_TOPK
    sel = pl.BlockSpec((jn, tn), lambda i, e: (0, i))
    row = pl.BlockSpec((tn, D_MODEL), lambda i, e: (i, 0))
    wblk = pl.BlockSpec((PEER_EB, D_MODEL), lambda i, e: (e, 0))
    return pl.pallas_call(
        _peer_body, grid=(n // tn, n_exp // PEER_EB),
        in_specs=[sel, sel, sel, row, row, pl.BlockSpec((D_MODEL, PEER_EB), lambda i, e: (0, e)), wblk],
        out_specs=row, out_shape=jax.ShapeDtypeStruct((n, D_MODEL), F32),
        scratch_shapes=[pltpu.VMEM((tn, jn), F32), pltpu.VMEM((tn, jn), F32), pltpu.VMEM((tn, jn), F32),
                        pltpu.VMEM((2, PEER_KEYS * SUBLANES, LANES), F32),
                        pltpu.VMEM((PEER_KEYS, tn, LANES), BF16)],
        compiler_params=_cparams(("arbitrary", "arbitrary"), 56), name="peer",
    )(e1t, e2t, gt, hn, h, ut, v)


def _pad_last(a, n):
    return jnp.pad(a, [(0, 0)] * (a.ndim - 1) + [(0, n - a.shape[-1])])


def _layer_weights(attn_norm_w, w_in, q_a_norm_w, w_uq, q_norm_w, kv_a_norm_w, w_ukv, k_norm_w,
                   ret_norm_w, w_o, ffn_norm_w, w_peer_q, sub_keys1, sub_keys2, peer_u, peer_v):
    o = np.cumsum([0, Q_LORA, KV_LORA, QK_ROPE, 512, 512, 512, 512])
    zeros = lambda w: jnp.zeros((D_MODEL, w), F32)
    win = jnp.concatenate([w_in[:, o[0]:o[2]], zeros(QK_NOPE), w_in[:, o[2]:o[3]], zeros(LANES - QK_HEAD),
                           w_in[:, o[3]:o[7]]], axis=1)
    wuq = _pad_last(w_uq.reshape(Q_LORA, MLA_HEADS, QK_HEAD), HEAD_PAD).reshape(Q_LORA, MLA_HEADS * HEAD_PAD)
    ukv = w_ukv.reshape(KV_LORA, MLA_HEADS, QK_NOPE + V_HEAD)
    ukn, uv = ukv[:, :, :QK_NOPE], ukv[:, :, QK_NOPE:]
    knw = _pad_last(k_norm_w, LANES)[None]
    wknt = _pad_last(ukn, HEAD_PAD).transpose(1, 2, 0)
    uvp = uv.reshape(KV_LORA, 4, 2, V_HEAD)
    zv = jnp.zeros((KV_LORA, 4, V_HEAD), F32)
    wve = jnp.concatenate([uvp[:, :, 0], zv], -1).transpose(1, 0, 2)
    wvo = jnp.concatenate([zv, uvp[:, :, 1]], -1).transpose(1, 0, 2)
    return dict(
        anw=attn_norm_w[None], win=win.astype(BF16), qanw=q_a_norm_w[None], wuq=wuq.astype(BF16),
        qnw=(_pad_last(q_norm_w, LANES) * (QK_HEAD ** -0.5 * LOG2E))[None], kvnw=kv_a_norm_w[None],
        wukn=_pad_last(ukn, HEAD_PAD).reshape(KV_LORA, MLA_HEADS * HEAD_PAD).astype(BF16),
        wuv=uv.reshape(KV_LORA, MLA_HEADS * V_HEAD).astype(BF16), knw=knw,
        wpe=_pad_last(k_norm_w[QK_NOPE:], LANES)[None], wknt=wknt.astype(BF16),
        wkn_t=ukn.transpose(1, 2, 0).reshape(MLA_HEADS * QK_NOPE, KV_LORA).astype(BF16), wve=wve.astype(BF16), wvo=wvo.astype(BF16),
        rnw=ret_norm_w[None], wo1=w_o[:512].astype(BF16), wo2=w_o[512:].astype(BF16), fnw=ffn_norm_w[None],
        wpq=w_peer_q.astype(BF16), k1=sub_keys1.astype(BF16), k2=sub_keys2.astype(BF16),
        put=peer_u.T.astype(BF16), pv=peer_v.astype(BF16))


def _rope_tables(pos):
    posf = pos.astype(F32)[:, None]
    n = pos.shape[0]
    hq = QK_ROPE // 2
    ang = posf * (ROPE_THETA ** (-jnp.arange(hq, dtype=F32) / hq))[None, :]
    c, s = jnp.cos(ang), jnp.sin(ang)
    one, zero = jnp.ones((n, QK_NOPE), F32), jnp.zeros((n, QK_NOPE), F32)
    cq = jnp.concatenate([one, c, c, one[:, :LANES - QK_HEAD]], 1)
    sq = jnp.concatenate([zero, -s, s, zero[:, :LANES - QK_HEAD]], 1)
    hr = RET_DK // 2
    angr = posf * (ROPE_THETA ** (-jnp.arange(hr, dtype=F32) / hr))[None, :]
    c2, s2 = jnp.cos(angr), jnp.sin(angr)
    cr = jnp.concatenate([c2, c2, c2, c2], 1)
    sr = jnp.concatenate([-s2, s2, -s2, s2], 1)
    return dict(cq=cq, sq=sq, cr=cr, sr=sr, cos_t=jnp.concatenate([c, c], 1).T, sin_t=jnp.concatenate([s, s], 1).T)


def kernel(x_prompt, x_sample, cache_kv_latent, cache_k_rope, state_ret, page_table, attn_norm_w, w_in, q_a_norm_w, w_uq, q_norm_w, kv_a_norm_w, w_ukv, k_norm_w, ret_norm_w, w_o, ffn_norm_w, w_peer_q, peer_sub_keys1, peer_sub_keys2, peer_u, peer_v):
    batch, seq, _ = x_prompt.shape
    n_seq, dec_seq, _ = x_sample.shape
    depth = attn_norm_w.shape[0]
    past = page_table.shape[1] * PAGE
    n_p, n_s = batch * seq, n_seq * dec_seq
    nrow = MLA_HEADS * dec_seq

    pos = jnp.concatenate([jnp.tile(jnp.arange(seq), batch), jnp.tile(past + jnp.arange(dec_seq), n_seq)])
    tabs = _rope_tables(pos)
    past_tabs = _rope_tables(jnp.arange(past))
    new_tabs = _rope_tables(past + jnp.arange(PAGE))

    h_all = jnp.concatenate([x_prompt.reshape(n_p, D_MODEL), x_sample.reshape(n_s, D_MODEL)], 0)
    outs = [[] for _ in range(6)]
    for l in range(depth):
        wts = _layer_weights(attn_norm_w[l], w_in[l], q_a_norm_w[l], w_uq[l], q_norm_w[l], kv_a_norm_w[l], w_ukv[l],
                             k_norm_w[l], ret_norm_w[l], w_o[l], ffn_norm_w[l], w_peer_q[l], peer_sub_keys1[l],
                             peer_sub_keys2[l], peer_u[l], peer_v[l])
        q, k, vt, ckv, kpe_t, rq, rk, rv, rg = _mixer(h_all, wts, tabs)
        kpe_s = kpe_t[:, n_p:].reshape(QK_ROPE, n_seq, dec_seq)

        mla_p = _pattn(q, k, vt, batch, seq)
        reto_p, r_p = _pret(rq, rk, rv, batch, seq)
        r_p = jnp.stack([r_p[:, :, :RET_DK, :RET_DV], r_p[:, :, RET_DK:, RET_DV:]], 2).reshape(batch, RET_HEADS, RET_DK, RET_DV)

        qa, qcs = _sabs(q[n_p:], wts["knw"], wts["wpe"], wts["wknt"])
        cnew = jnp.pad(ckv[n_p:].reshape(n_seq, dec_seq, KV_LORA), ((0, 0), (0, PAGE - dec_seq), (0, 0)))
        pnew_t = jnp.pad(kpe_s.transpose(1, 0, 2), ((0, 0), (0, 0), (0, PAGE - dec_seq)))
        pc = _sattn(page_table, qa.reshape(n_seq, nrow, HEAD_PAD), qcs.reshape(n_seq, nrow, HEAD_PAD), wts["wkn_t"],
                    past_tabs["cos_t"], past_tabs["sin_t"], cnew, pnew_t, new_tabs["cos_t"], new_tabs["sin_t"],
                    cache_kv_latent[l], cache_k_rope[l].transpose(0, 2, 1), dec_seq)
        pc_h = pc.reshape(n_seq, dec_seq, MLA_HEADS, KV_LORA).transpose(2, 0, 1, 3).reshape(MLA_HEADS, n_s, KV_LORA)
        mla_s = _sup(pc_h, wts["wve"], wts["wvo"])
        seq_last = lambda a: a[n_p:].astype(F32).reshape(n_seq, dec_seq, 512).transpose(1, 2, 0)
        reto_s, r_s = _sret(seq_last(rq), seq_last(rk), seq_last(rv), state_ret[l].astype(F32).transpose(1, 2, 3, 0),
                            dec_seq)
        reto_s = reto_s.transpose(2, 0, 1).reshape(n_s, 512)
        r_s = r_s.transpose(3, 0, 1, 2)

        mla = jnp.concatenate([mla_p, mla_s], 0)
        reto = jnp.concatenate([reto_p, reto_s], 0)
        h_mid, hn, s1t, s2t = _post(h_all, mla, reto, rg, wts)
        e1t, e2t, gt = _topk(s1t, s2t)
        h_all = _peer(e1t, e2t, gt, hn, h_mid, wts["put"], wts["pv"])

        outs[0].append(ckv[:n_p].reshape(batch, seq, KV_LORA))
        outs[1].append(kpe_t[:, :n_p].reshape(QK_ROPE, batch, seq).transpose(1, 2, 0))
        outs[2].append(r_p)
        outs[3].append(ckv[n_p:].reshape(n_seq, dec_seq, KV_LORA))
        outs[4].append(kpe_s.transpose(1, 2, 0))
        outs[5].append(r_s)
    return (h_all[:n_p].reshape(batch, seq, D_MODEL), h_all[n_p:].reshape(n_seq, dec_seq, D_MODEL),
            *[jnp.stack(o) for o in outs])
```

```python
import functools

import numpy as np
import jax
import jax.numpy as jnp
from jax import lax
from jax.experimental import pallas as pl
from jax.experimental.pallas import tpu as pltpu

F32 = jnp.float32
BF16 = jnp.bfloat16

D_MODEL = 1024
PAGE = 128
MLA_HEADS = 8
QK_NOPE = 64
QK_ROPE = 32
QK_HEAD = QK_NOPE + QK_ROPE
V_HEAD = 64
Q_LORA = 256
KV_LORA = 128
RET_HEADS = 8
RET_DK = 64
RET_DV = 64
RET_CHUNK = 128
ROPE_THETA = 10000.0
PEER_KEYS = 128
PEER_HEADS = 8
PEER_QDIM = 256
PEER_TOPK = 16
EPS = 1e-6
LANES = 128
SUBLANES = 8
HEAD_PAD = 128
NEG_INF = float("-inf")
INV_SQRT2 = 0.7071067811865476
LOG2E = 1.4426950408889634

MIX_TM = 256
ATT_T = 512
ATT_CH = 32
SAMP_PAGES = 32
SAMP_CHUNK = 512
TOPK_TT = 256
TOPK_HEADS = 2
PEER_TN = 512
PEER_EB = 1024


def _cparams(sem, vmem_mb):
    return pltpu.CompilerParams(dimension_semantics=sem, vmem_limit_bytes=vmem_mb << 20)


def _full(shape):
    n = len(shape)
    return pl.BlockSpec(shape, lambda *_: (0,) * n)


def _rms(x, w, n):
    return x * lax.rsqrt(jnp.sum(x * x, -1, keepdims=True) * (1.0 / n) + EPS) * w


def _dot(a, b):
    return jnp.dot(a, b, preferred_element_type=F32)


def _dot_nt(a, b):
    return lax.dot_general(a, b, (((1,), (1,)), ((), ())), preferred_element_type=F32)


def _dot_tn(a, b):
    return lax.dot_general(a, b, (((0,), (0,)), ((), ())), preferred_element_type=F32)


def _mixer_body(x_ref, anw_ref, win_ref, qanw_ref, wuq_ref, qnw_ref, kvnw_ref, wukn_ref, wuv_ref, knw_ref,
                cq_ref, sq_ref, cr_ref, sr_ref,
                q_ref, k_ref, vt_ref, ckv_ref, kpe_ref, rq_ref, rk_ref, rv_ref, rg_ref):
    x = x_ref[...]
    xn = _rms(x, anw_ref[...], D_MODEL)
    proj = _dot(xn.astype(BF16), win_ref[...])
    lane = lax.broadcasted_iota(jnp.int32, (x.shape[0], LANES), 1)
    cq, sq, cr, sr = cq_ref[...], sq_ref[...], cr_ref[...], sr_ref[...]
    lo_q = lane < QK_NOPE + QK_ROPE // 2
    lo_r = (lane & (RET_DK - 1)) < RET_DK // 2

    def rope_q(v):
        return v * cq + jnp.where(lo_q, pltpu.roll(v, LANES - 16, 1), pltpu.roll(v, 16, 1)) * sq

    def rope_r(v):
        return v * cr + jnp.where(lo_r, pltpu.roll(v, LANES - 32, 1), pltpu.roll(v, 32, 1)) * sr

    q_lat = _rms(proj[:, 0:256], qanw_ref[...], Q_LORA)
    qf = _dot(q_lat.astype(BF16), wuq_ref[...])
    ckv = _rms(proj[:, 256:384], kvnw_ref[...], KV_LORA)
    ckv_ref[...] = ckv
    kslot = proj[:, 384:512]
    kpe_ref[...] = kslot.T[QK_NOPE:QK_HEAD, :]
    cb = ckv.astype(BF16)
    knf = _dot(cb, wukn_ref[...])
    vt_ref[...] = _dot(cb, wuv_ref[...]).T.astype(BF16)
    qnw, knw = qnw_ref[...], knw_ref[...]
    for h in range(MLA_HEADS):
        sl = slice(HEAD_PAD * h, HEAD_PAD * (h + 1))
        q_ref[:, sl] = rope_q(_rms(qf[:, sl], qnw, QK_HEAD)).astype(BF16)
        k_ref[:, sl] = rope_q(_rms(knf[:, sl] + kslot, knw, QK_HEAD)).astype(BF16)
    for p in range(4):
        sl = slice(LANES * p, LANES * (p + 1))
        rq_ref[:, sl] = rope_r(proj[:, 512 + LANES * p:512 + LANES * (p + 1)]).astype(BF16)
        rk_ref[:, sl] = (rope_r(proj[:, 1024 + LANES * p:1024 + LANES * (p + 1)]) * (RET_DK ** -0.5)).astype(BF16)
    rv_ref[...] = proj[:, 1536:2048].astype(BF16)
    rg_ref[...] = proj[:, 2048:2560]


def _mixer(x_all, wts, tabs):
    n = x_all.shape[0]
    tm = MIX_TM
    row = lambda w: pl.BlockSpec((tm, w), lambda i: (i, 0))
    ins = [row(D_MODEL), _full((1, D_MODEL)), _full((D_MODEL, 2560)), _full((1, Q_LORA)), _full((Q_LORA, 1024)),
           _full((1, LANES)), _full((1, KV_LORA)), _full((KV_LORA, 1024)), _full((KV_LORA, 512)), _full((1, LANES)),
           row(LANES), row(LANES), row(LANES), row(LANES)]
    outs = [row(1024), row(1024), pl.BlockSpec((512, tm), lambda i: (0, i)), row(KV_LORA),
            pl.BlockSpec((QK_ROPE, tm), lambda i: (0, i)), row(512),
            row(512), row(512), row(512)]
    shapes = [jax.ShapeDtypeStruct((n, 1024), BF16), jax.ShapeDtypeStruct((n, 1024), BF16),
              jax.ShapeDtypeStruct((512, n), BF16), jax.ShapeDtypeStruct((n, KV_LORA), F32),
              jax.ShapeDtypeStruct((QK_ROPE, n), F32), jax.ShapeDtypeStruct((n, 512), BF16),
              jax.ShapeDtypeStruct((n, 512), BF16), jax.ShapeDtypeStruct((n, 512), BF16),
              jax.ShapeDtypeStruct((n, 512), F32)]
    return pl.pallas_call(
        _mixer_body, grid=(n // tm,), in_specs=ins, out_specs=outs, out_shape=shapes,
        compiler_params=_cparams(("arbitrary",), 48), name="mixer",
    )(x_all, wts["anw"], wts["win"], wts["qanw"], wts["wuq"], wts["qnw"], wts["kvnw"], wts["wukn"], wts["wuv"],
      wts["knw"], tabs["cq"], tabs["sq"], tabs["cr"], tabs["sr"])


def _pattn_body(q_ref, k_ref, vt_ref, o_ref, st0, st1, pt0, pt1, acc_sc):
    qi = pl.program_id(2)
    t = ATT_T
    ch = ATT_CH
    krow = lax.broadcasted_iota(jnp.int32, (ch, t), 0)
    qcol = lax.broadcasted_iota(jnp.int32, (ch, t), 1)
    acc_sc[...] = jnp.zeros(acc_sc.shape, F32)
    st_sc, pt_sc = (st0, st1), (pt0, pt1)

    def scores(j, slot):
        start = pl.multiple_of(j * t, t)
        for hh in range(2):
            lsl = slice(HEAD_PAD * hh, HEAD_PAD * (hh + 1))
            st_sc[hh][slot] = _dot_nt(k_ref[pl.ds(start, t), lsl], q_ref[:, lsl])

    def update(j, slot, carry, masked):
        start = pl.multiple_of(j * t, t)
        out = []
        for hh in range(2):
            m_prev, l_prev = carry[2 * hh], carry[2 * hh + 1]

            def chunk(r):
                blk = st_sc[hh][slot, ch * r:ch * (r + 1), :]
                return jnp.where(krow + ch * r <= qcol, blk, NEG_INF) if masked else blk

            mc = chunk(0)
            for r in range(1, t // ch):
                mc = jnp.maximum(mc, chunk(r))
            m_new = jnp.maximum(m_prev, jnp.max(mc, 0, keepdims=True))
            alpha = jnp.exp2(m_prev - m_new)
            ls = jnp.zeros((ch, t), F32)
            for r in range(t // ch):
                p = jnp.exp2(chunk(r) - m_new)
                ls = ls + p
                pt_sc[hh][ch * r:ch * (r + 1), :] = p.astype(BF16)
            acc_sc[hh] = alpha * acc_sc[hh] + _dot(vt_ref[:, pl.ds(start, t)], pt_sc[hh][...])
            out += [m_new, alpha * l_prev + jnp.sum(ls, 0, keepdims=True)]
        return tuple(out)

    def step(j, slot, carry):
        scores(j + 1, 1 - slot)
        return update(j, slot, carry, False)

    def pair(i, carry):
        return step(2 * i + 1, 1, step(2 * i, 0, carry))

    scores(0, 0)
    init = (jnp.full((1, t), NEG_INF, F32), jnp.zeros((1, t), F32)) * 2
    carry = lax.fori_loop(0, qi // 2, pair, init)
    _, l0, _, l1 = lax.cond(qi % 2 == 1,
                            lambda c: update(qi, 1, step(qi - 1, 0, c), True),
                            lambda c: update(qi, 0, c, True), carry)
    drow = lax.broadcasted_iota(jnp.int32, (LANES, t), 0)
    o_ref[...] = jnp.where(drow < V_HEAD, acc_sc[0] / l0, acc_sc[1] / l1).T.astype(o_ref.dtype)


def _pattn(q_all, k_all, vt_all, batch, seq):
    t = ATT_T
    nq = seq // t
    return pl.pallas_call(
        _pattn_body, grid=(batch, MLA_HEADS // 2, nq),
        in_specs=[pl.BlockSpec((t, 2 * HEAD_PAD), lambda b, hp, qi: (b * nq + qi, hp)),
                  pl.BlockSpec((seq, 2 * HEAD_PAD), lambda b, hp, qi: (b, hp)),
                  pl.BlockSpec((LANES, seq), lambda b, hp, qi: (hp, b))],
        out_specs=pl.BlockSpec((t, LANES), lambda b, hp, qi: (b * nq + qi, hp)),
        out_shape=jax.ShapeDtypeStruct((batch * seq, MLA_HEADS * V_HEAD), BF16),
        scratch_shapes=[pltpu.VMEM((2, t, t), F32), pltpu.VMEM((2, t, t), F32), pltpu.VMEM((t, t), BF16),
                        pltpu.VMEM((t, t), BF16), pltpu.VMEM((2, LANES, t), F32)],
        compiler_params=_cparams(("arbitrary", "arbitrary", "arbitrary"), 48), name="pattn",
    )(q_all, k_all, vt_all)


def _ret_log_decay():
    return jnp.log1p(-jnp.power(2.0, -5.0 - jnp.arange(RET_HEADS, dtype=F32)))


def _ret_tables(c):
    lg = _ret_log_decay()
    i = jnp.arange(c, dtype=F32)
    diff = i[:, None] - i[None, :]
    dmat = jnp.where(diff >= 0, jnp.exp(lg[:, None, None] * jnp.maximum(diff, 0.0)), 0.0)
    lane_head = lambda a: jnp.repeat(a.reshape(a.shape[0], 4, 2), RET_DV, axis=2).transpose(1, 0, 2)
    cross = lane_head(jnp.exp(lg[None, :] * (i[:, None] + 1.0)))
    kdec = lane_head(jnp.exp(lg[None, :] * (c - 1.0 - i[:, None])))
    gc = lane_head(jnp.exp(lg * c)[None, :])
    return dmat[0::2], dmat[1::2], cross, kdec, gc


def _pret_body(rq_ref, rk_ref, rv_ref, de_ref, do_ref, cross_ref, kdec_ref, gc_ref, o_ref, rout_ref, r_sc):
    c = pl.program_id(1)

    @pl.when(c == 0)
    def _():
        r_sc[...] = jnp.zeros(r_sc.shape, F32)

    n = rq_ref.shape[0]
    lane = lax.broadcasted_iota(jnp.int32, (n, LANES), 1)
    even = lane < RET_DK
    rr = lax.broadcasted_iota(jnp.int32, (LANES, LANES), 0)
    cc = lax.broadcasted_iota(jnp.int32, (LANES, LANES), 1)
    same_head = (rr < RET_DK) == (cc < RET_DV)
    for p in range(4):
        sl = slice(LANES * p, LANES * (p + 1))
        q, k, v = rq_ref[:, sl], rk_ref[:, sl], rv_ref[:, sl]
        zero = jnp.zeros_like(q)
        se = _dot_nt(jnp.where(even, q, zero), k)
        so = _dot_nt(jnp.where(even, zero, q), k)
        oe = _dot((se * de_ref[p]).astype(BF16), v)
        oo = _dot((so * do_ref[p]).astype(BF16), v)
        r = r_sc[p]
        cross = _dot(q, r.astype(BF16)) * cross_ref[p]
        o_ref[:, sl] = jnp.where(even, oe, oo) + cross
        kd = (k.astype(F32) * kdec_ref[p]).astype(BF16)
        upd = _dot_tn(kd, v)
        r_sc[p] = r * gc_ref[p] + jnp.where(same_head, upd, 0.0)

    @pl.when(c == pl.num_programs(1) - 1)
    def _():
        rout_ref[0] = r_sc[...]


def _pret(rq, rk, rv, batch, seq):
    c = RET_CHUNK
    nc = seq // c
    tabs = _ret_tables(c)
    blk = pl.BlockSpec((c, 512), lambda b, i: (b * nc + i, 0))
    return pl.pallas_call(
        _pret_body, grid=(batch, nc),
        in_specs=[blk, blk, blk, _full((4, c, c)), _full((4, c, c)), _full((4, c, LANES)), _full((4, c, LANES)),
                  _full((4, 1, LANES))],
        out_specs=[blk, pl.BlockSpec((1, 4, LANES, LANES), lambda b, i: (b, 0, 0, 0))],
        out_shape=[jax.ShapeDtypeStruct((batch * seq, 512), F32),
                   jax.ShapeDtypeStruct((batch, 4, LANES, LANES), F32)],
        scratch_shapes=[pltpu.VMEM((4, LANES, LANES), F32)],
        compiler_params=_cparams(("arbitrary", "arbitrary"), 32), name="pret",
    )(rq, rk, rv, *tabs)


def _sabs_body(q_ref, knw_ref, wpe_ref, wknt_ref, qa_ref, qcs_ref):
    q = q_ref[...].astype(F32)
    qa_ref[...] = _dot((q * knw_ref[...]).astype(BF16), wknt_ref[0]).astype(BF16)
    lane = lax.broadcasted_iota(jnp.int32, q.shape, 1)
    qpe = pltpu.roll(q, LANES - QK_NOPE, 1)
    half = QK_ROPE // 2
    swapped = jnp.where(lane < half, pltpu.roll(qpe, LANES - half, 1), -pltpu.roll(qpe, half, 1))
    keep = lane < QK_ROPE
    wpe = wpe_ref[...]
    qc = jnp.where(keep, qpe * wpe, 0.0)
    qs = jnp.where(keep, swapped * wpe, 0.0)
    qcs_ref[...] = (qc + pltpu.roll(qs, QK_ROPE, 1)).astype(BF16)


def _sabs(q_s, knw, wpe, wknt):
    n = q_s.shape[0]
    blk = pl.BlockSpec((n, HEAD_PAD), lambda h: (0, h))
    shp = jax.ShapeDtypeStruct((n, MLA_HEADS * HEAD_PAD), BF16)
    return pl.pallas_call(
        _sabs_body, grid=(MLA_HEADS,),
        in_specs=[blk, _full((1, LANES)), _full((1, LANES)), pl.BlockSpec((1, LANES, KV_LORA), lambda h: (h, 0, 0))],
        out_specs=[blk, blk], out_shape=[shp, shp],
        compiler_params=_cparams(("arbitrary",), 32), name="sabs",
    )(q_s, knw, wpe, wknt)


def _sattn_body(pt_ref, qa_ref, qcs_ref, wkn_ref, cost_ref, sint_ref, cnew_ref, pnew_ref, cosn_ref, sinn_ref,
                *rest, n_pages, dec_seq):
    lat_refs = rest[:n_pages]
    pe_refs = rest[n_pages:2 * n_pages]
    pc_ref = rest[2 * n_pages]
    cbuf, lhs_sc, xcs, m_sc, l_sc, acc_sc = rest[2 * n_pages + 1:]
    s_id = pl.program_id(0)
    j = pl.program_id(1)
    nrow = qa_ref.shape[1]
    nkn = wkn_ref.shape[0]

    @pl.when(jnp.logical_and(s_id == 0, j == 0))
    def _():
        xcs[...] = jnp.zeros(xcs.shape, BF16)
        lhs_sc[0:nkn, :] = wkn_ref[...]

    @pl.when(j == 0)
    def _():
        lhs_sc[nkn:nkn + nrow, :] = qa_ref[0]
        m_sc[...] = jnp.full(m_sc.shape, NEG_INF, F32)
        l_sc[...] = jnp.zeros(l_sc.shape, F32)
        acc_sc[...] = jnp.zeros(acc_sc.shape, F32)

    row8 = lax.broadcasted_iota(jnp.int32, (MLA_HEADS, 1), 0)

    def attend(nk, pe_t, cos_t, sin_t, mask):
        xcs[0:QK_ROPE, 0:nk] = (pe_t * cos_t).astype(BF16)
        xcs[QK_ROPE:2 * QK_ROPE, 0:nk] = (pe_t * sin_t).astype(BF16)
        pe2 = jnp.sum(pe_t * pe_t, 0, keepdims=True)
        lhs = lhs_sc[...]
        qcs = qcs_ref[0]
        parts = []
        ck = min(nk, SAMP_CHUNK)
        for c0 in range(0, nk, ck):
            c = cbuf[c0:c0 + ck, :]
            big = _dot_nt(lhs, c)
            ss = pe2[:, c0:c0 + ck] + jnp.zeros((MLA_HEADS, ck), F32)
            for h in range(MLA_HEADS):
                kh = big[QK_NOPE * h:QK_NOPE * (h + 1), :]
                ss = ss + jnp.where(row8 == h, jnp.sum(kh * kh, 0, keepdims=True), 0.0)
            rn = lax.rsqrt(ss * (1.0 / QK_HEAD) + EPS)
            sc = big[nkn:nkn + nrow, :] + _dot(qcs, xcs[:, c0:c0 + ck])
            parts.append((sc.reshape(dec_seq, MLA_HEADS, ck) * rn[None]).reshape(nrow, ck))
        s = parts[0] if len(parts) == 1 else jnp.concatenate(parts, axis=1)
        if mask is not None:
            s = jnp.where(mask, s, NEG_INF)
        m_prev = m_sc[...]
        m_new = jnp.maximum(m_prev, jnp.max(s, -1, keepdims=True))
        alpha = jnp.exp2(m_prev - m_new)
        p = jnp.exp2(s - m_new)
        l_sc[...] = alpha * l_sc[...] + jnp.sum(p, -1, keepdims=True)
        acc_sc[...] = alpha * acc_sc[...] + _dot(p.astype(BF16), cbuf[0:nk, :])
        m_sc[...] = m_new

    for i in range(n_pages):
        cbuf[PAGE * i:PAGE * (i + 1), :] = lat_refs[i][0].astype(BF16)
    pe_all = jnp.concatenate([pe_refs[i][0] for i in range(n_pages)], axis=1)
    attend(PAGE * n_pages, pe_all, cost_ref[...], sint_ref[...], None)

    @pl.when(j == pl.num_programs(1) - 1)
    def _():
        cbuf[0:PAGE, :] = cnew_ref[0].astype(BF16)
        qtok = lax.shift_right_logical(lax.broadcasted_iota(jnp.int32, (nrow, PAGE), 0), 3)
        kcol = lax.broadcasted_iota(jnp.int32, (nrow, PAGE), 1)
        attend(PAGE, pnew_ref[0], cosn_ref[...], sinn_ref[...], jnp.logical_and(kcol <= qtok, kcol < dec_seq))
        pc_ref[0] = acc_sc[...] / l_sc[...]


def _sattn(page_table, qa, qcs, wkn_t, cos_t, sin_t, cnew, pnew_t, cosn_t, sinn_t, cache_lat, cache_pe_t, dec_seq):
    n_seq, nrow, _ = qa.shape
    n_pages_total = page_table.shape[1]
    npg = SAMP_PAGES
    steps = n_pages_total // npg
    tk = npg * PAGE
    nkn = wkn_t.shape[0]
    assert MLA_HEADS == SUBLANES and nrow == dec_seq * MLA_HEADS
    qblk = pl.BlockSpec((1, nrow, LANES), lambda s, j, pt: (s, 0, 0))
    const2 = lambda shape: pl.BlockSpec(shape, lambda s, j, pt: (0, 0))
    tab = pl.BlockSpec((QK_ROPE, tk), lambda s, j, pt: (0, j))
    in_specs = [qblk, qblk, const2((nkn, KV_LORA)), tab, tab,
                pl.BlockSpec((1, PAGE, KV_LORA), lambda s, j, pt: (s, 0, 0)),
                pl.BlockSpec((1, QK_ROPE, PAGE), lambda s, j, pt: (s, 0, 0)),
                const2((QK_ROPE, PAGE)), const2((QK_ROPE, PAGE))]
    for i in range(npg):
        in_specs.append(pl.BlockSpec((1, PAGE, KV_LORA), lambda s, j, pt, i=i: (pt[s, j * npg + i], 0, 0)))
    for i in range(npg):
        in_specs.append(pl.BlockSpec((1, QK_ROPE, PAGE), lambda s, j, pt, i=i: (pt[s, j * npg + i], 0, 0)))
    gs = pltpu.PrefetchScalarGridSpec(
        num_scalar_prefetch=1, grid=(n_seq, steps), in_specs=in_specs,
        out_specs=pl.BlockSpec((1, nrow, LANES), lambda s, j, pt: (s, 0, 0)),
        scratch_shapes=[pltpu.VMEM((tk, LANES), BF16), pltpu.VMEM((nkn + nrow, KV_LORA), BF16),
                        pltpu.VMEM((LANES, tk), BF16), pltpu.VMEM((nrow, 1), F32), pltpu.VMEM((nrow, 1), F32),
                        pltpu.VMEM((nrow, LANES), F32)])
    return pl.pallas_call(
        functools.partial(_sattn_body, n_pages=npg, dec_seq=dec_seq), grid_spec=gs,
        out_shape=jax.ShapeDtypeStruct((n_seq, nrow, LANES), F32),
        compiler_params=_cparams(("arbitrary", "arbitrary"), 48), name="sattn",
    )(page_table, qa, qcs, wkn_t, cos_t, sin_t, cnew, pnew_t, cosn_t, sinn_t,
      *([cache_lat] * npg), *([cache_pe_t] * npg))


def _sup_body(pc_ref, we_ref, wo_ref, o_ref):
    o_ref[...] = (_dot(pc_ref[0].astype(BF16), we_ref[0]) + _dot(pc_ref[1].astype(BF16), wo_ref[0])).astype(o_ref.dtype)


def _sup(pc_h, wve, wvo):
    n = pc_h.shape[1]
    return pl.pallas_call(
        _sup_body, grid=(MLA_HEADS // 2,),
        in_specs=[pl.BlockSpec((2, n, KV_LORA), lambda p: (p, 0, 0)),
                  pl.BlockSpec((1, KV_LORA, LANES), lambda p: (p, 0, 0)),
                  pl.BlockSpec((1, KV_LORA, LANES), lambda p: (p, 0, 0))],
        out_specs=pl.BlockSpec((n, LANES), lambda p: (0, p)),
        out_shape=jax.ShapeDtypeStruct((n, MLA_HEADS * V_HEAD), BF16),
        compiler_params=_cparams(("arbitrary",), 32), name="sup",
    )(pc_h, wve, wvo)


def _sret_body(q_ref, k_ref, v_ref, r0_ref, d_ref, cross_ref, kdec_ref, g_ref, o_ref, r_ref):
    nt = q_ref.shape[0]
    outs = []
    for i in range(nt):
        acc = jnp.zeros(v_ref.shape[1:], F32)
        for j in range(i + 1):
            sij = jnp.sum(q_ref[i] * k_ref[j], 0, keepdims=True) * d_ref[0, i * nt + j:i * nt + j + 1, :]
            acc = acc + sij * v_ref[j]
        outs.append(acc)

    def body(d, carry):
        r = r0_ref[0, d]
        row = pl.ds(d, 1)
        rn = g_ref[0] * r
        new = []
        for i in range(nt):
            new.append(carry[i] + (q_ref[i, row, :] * cross_ref[0, i:i + 1, :]) * r)
            rn = rn + (k_ref[i, row, :] * kdec_ref[0, i:i + 1, :]) * v_ref[i]
        r_ref[0, d] = rn
        return tuple(new)

    outs = lax.fori_loop(0, RET_DK, body, tuple(outs))
    for i in range(nt):
        o_ref[i] = outs[i]


def _sret(q_t, k_t, v_t, r0_t, dec_seq):
    n_seq = q_t.shape[2]
    lg = _ret_log_decay()
    i = jnp.arange(dec_seq, dtype=F32)
    lanes = jnp.ones((1, 1, n_seq), F32)
    diff = i[:, None] - i[None, :]
    dmat = jnp.where(diff >= 0, jnp.exp(lg[:, None, None] * jnp.maximum(diff, 0.0)), 0.0)
    dtab = dmat.reshape(RET_HEADS, dec_seq * dec_seq, 1) * lanes
    cross = jnp.exp(lg[:, None] * (i[None, :] + 1.0))[:, :, None] * lanes
    kdec = jnp.exp(lg[:, None] * (dec_seq - 1.0 - i[None, :]))[:, :, None] * lanes
    gdec = jnp.exp(lg * dec_seq)[:, None, None] * lanes
    blk = pl.BlockSpec((dec_seq, RET_DK, n_seq), lambda h: (0, h, 0))
    rblk = pl.BlockSpec((1, RET_DK, RET_DV, n_seq), lambda h: (h, 0, 0, 0))
    tab = lambda r: pl.BlockSpec((1, r, n_seq), lambda h: (h, 0, 0))
    return pl.pallas_call(
        _sret_body, grid=(RET_HEADS,),
        in_specs=[blk, blk, blk, rblk, tab(dec_seq * dec_seq), tab(dec_seq), tab(dec_seq), tab(1)],
        out_specs=[blk, rblk],
        out_shape=[jax.ShapeDtypeStruct((dec_seq, RET_HEADS * RET_DV, n_seq), F32),
                   jax.ShapeDtypeStruct((RET_HEADS, RET_DK, RET_DV, n_seq), F32)],
        compiler_params=_cparams(("arbitrary",), 32), name="sret",
    )(q_t, k_t, v_t, r0_t, dtab, cross, kdec, gdec)


def _post_body(x_ref, mla_ref, reto_ref, rg_ref, rnw_ref, wo1_ref, wo2_ref, fnw_ref, wpq_ref, k1_ref, k2_ref,
               h_ref, hn_ref, s1_ref, s2_ref):
    tm = x_ref.shape[0]
    lane = lax.broadcasted_iota(jnp.int32, (tm, LANES), 1)
    lo = lane < RET_DV

    def group(a):
        s_lo = jnp.sum(jnp.where(lo, a, 0.0), -1, keepdims=True)
        s_hi = jnp.sum(jnp.where(lo, 0.0, a), -1, keepdims=True)
        return jnp.where(lo, s_lo, s_hi)

    parts = []
    for p in range(4):
        sl = slice(LANES * p, LANES * (p + 1))
        o = reto_ref[:, sl]
        d = o - group(o) * (1.0 / RET_DV)
        y = d * lax.rsqrt(group(d * d) * (1.0 / RET_DV) + EPS) * rnw_ref[:, sl]
        rg = rg_ref[:, sl]
        parts.append((rg * jax.nn.sigmoid(rg) * y).astype(BF16))
    ret_out = jnp.concatenate(parts, axis=1)
    h = x_ref[...] + _dot(mla_ref[...], wo1_ref[...]) + _dot(ret_out, wo2_ref[...])
    h_ref[...] = h
    hn = _rms(h, fnw_ref[...], D_MODEL).astype(BF16)
    hn_ref[...] = hn
    pq = _dot(hn, wpq_ref[...])
    half = PEER_QDIM // 2
    for hd in range(PEER_HEADS):
        q1 = pq[:, PEER_QDIM * hd:PEER_QDIM * hd + half].astype(BF16)
        q2 = pq[:, PEER_QDIM * hd + half:PEER_QDIM * (hd + 1)].astype(BF16)
        s1_ref[hd] = _dot_nt(k1_ref[...], q1)
        s2_ref[hd] = _dot_nt(k2_ref[...], q2)


def _post(x_all, mla, reto, rg, wts):
    n = x_all.shape[0]
    tm = MIX_TM
    row = lambda w: pl.BlockSpec((tm, w), lambda i: (i, 0))
    sblk = pl.BlockSpec((PEER_HEADS, PEER_KEYS, tm), lambda i: (0, 0, i))
    half = PEER_QDIM // 2
    return pl.pallas_call(
        _post_body, grid=(n // tm,),
        in_specs=[row(D_MODEL), row(512), row(512), row(512), _full((1, 512)), _full((512, D_MODEL)),
                  _full((512, D_MODEL)), _full((1, D_MODEL)), _full((D_MODEL, PEER_HEADS * PEER_QDIM)),
                  _full((PEER_KEYS, half)), _full((PEER_KEYS, half))],
        out_specs=[row(D_MODEL), row(D_MODEL), sblk, sblk],
        out_shape=[jax.ShapeDtypeStruct((n, D_MODEL), F32), jax.ShapeDtypeStruct((n, D_MODEL), BF16),
                   jax.ShapeDtypeStruct((PEER_HEADS, PEER_KEYS, n), F32),
                   jax.ShapeDtypeStruct((PEER_HEADS, PEER_KEYS, n), F32)],
        compiler_params=_cparams(("arbitrary",), 48), name="post",
    )(x_all, mla, reto, rg, wts["rnw"], wts["wo1"], wts["wo2"], wts["fnw"], wts["wpq"], wts["k1"], wts["k2"])


def _topk_body(s1_ref, s2_ref, e1_ref, e2_ref, g_ref):
    kk = PEER_TOPK
    tt = s1_ref.shape[2]
    row16 = lax.broadcasted_iota(jnp.int32, (kk, tt), 0)

    def take_top(x):
        nrow = x.shape[0]
        row = lax.broadcasted_iota(jnp.int32, x.shape, 0)
        vals = jnp.zeros((kk, tt), F32)
        idxs = jnp.zeros((kk, tt), jnp.int32)
        for r in range(kk):
            m = jnp.max(x, axis=0, keepdims=True)
            idx = jnp.min(jnp.where(x == m, row, nrow), axis=0, keepdims=True)
            vals = jnp.where(row16 == r, m, vals)
            idxs = jnp.where(row16 == r, idx, idxs)
            x = jnp.where(row == idx, NEG_INF, x)
        return vals, idxs

    def pick(table, sel):
        return jnp.sum(jnp.where(row16 == sel, table, 0), axis=0, keepdims=True)

    for hh in range(TOPK_HEADS):
        v1, i1 = take_top(s1_ref[hh])
        v2, i2 = take_top(s2_ref[hh])
        cur = v1 + v2[0:1, :]
        ptr = jnp.zeros((kk, tt), jnp.int32)
        vs = jnp.zeros((kk, tt), F32)
        e1 = jnp.zeros((kk, tt), jnp.int32)
        e2 = jnp.zeros((kk, tt), jnp.int32)
        for r in range(kk):
            m = jnp.max(cur, axis=0, keepdims=True)
            a_sel = row16 == jnp.min(jnp.where(cur == m, row16, kk), axis=0, keepdims=True)
            b_cur = jnp.sum(jnp.where(a_sel, ptr, 0), axis=0, keepdims=True)
            vs = jnp.where(row16 == r, m, vs)
            e1 = jnp.where(row16 == r, jnp.sum(jnp.where(a_sel, i1, 0), axis=0, keepdims=True), e1)
            e2 = jnp.where(row16 == r, pick(i2, b_cur), e2)
            b_next = b_cur + 1
            head = jnp.sum(jnp.where(a_sel, v1, 0.0), axis=0, keepdims=True) + pick(v2, b_next)
            cur = jnp.where(a_sel, jnp.where(b_next < kk, head, NEG_INF), cur)
            ptr = jnp.where(a_sel, b_next, ptr)
        p = jnp.exp(vs - jnp.max(vs, axis=0, keepdims=True))
        g_ref[kk * hh:kk * (hh + 1), :] = p / jnp.sum(p, axis=0, keepdims=True)
        e1_ref[kk * hh:kk * (hh + 1), :] = e1.astype(F32)
        e2_ref[kk * hh:kk * (hh + 1), :] = e2.astype(F32)


def _topk(s1t, s2t):
    n = s1t.shape[2]
    tt = TOPK_TT
    sblk = pl.BlockSpec((TOPK_HEADS, PEER_KEYS, tt), lambda i, h: (h, 0, i))
    oblk = pl.BlockSpec((TOPK_HEADS * PEER_TOPK, tt), lambda i, h: (h, i))
    shp = jax.ShapeDtypeStruct((PEER_HEADS * PEER_TOPK, n), F32)
    return pl.pallas_call(
        _topk_body, grid=(n // tt, PEER_HEADS // TOPK_HEADS), in_specs=[sblk, sblk], out_specs=[oblk, oblk, oblk],
        out_shape=[shp, shp, shp],
        compiler_params=_cparams(("arbitrary", "arbitrary"), 32), name="topk",
    )(s1t, s2t)


def _peer_body(e1_ref, e2_ref, g_ref, hn_ref, h_ref, ut_ref, v_ref, o_ref, e1_sc, e2_sc, g_sc, stg_sc, w_sc):
    eb = pl.program_id(1)
    tn = hn_ref.shape[0]
    pack = 2 * SUBLANES

    @pl.when(eb == 0)
    def _():
        e1_sc[...] = e1_ref[...].T
        e2_sc[...] = e2_ref[...].T
        g_sc[...] = g_ref[...].T
        o_ref[...] = h_ref[...]
        iot = lax.broadcasted_iota(jnp.int32, (PEER_KEYS, LANES), 0).astype(F32)

        def body(grp, c):
            for half in range(2):
                rows = pl.ds(pl.multiple_of(grp * pack + half * SUBLANES, SUBLANES), SUBLANES)
                e1g, e2g, gg = e1_sc[rows, :], e2_sc[rows, :], g_sc[rows, :]
                for t in range(SUBLANES):
                    a = jnp.where(e1g[t:t + 1, :] == iot, 1.0, 0.0).astype(BF16)
                    b = jnp.where(e2g[t:t + 1, :] == iot, gg[t:t + 1, :], 0.0).astype(BF16)
                    stg_sc[half, pl.ds(t, PEER_KEYS, stride=SUBLANES), :] = _dot_nt(a, b)
            lo = stg_sc[0].reshape(PEER_KEYS, SUBLANES, LANES)
            hi = stg_sc[1].reshape(PEER_KEYS, SUBLANES, LANES)
            w_sc[:, pl.ds(pl.multiple_of(grp * pack, pack), pack), :] = jnp.concatenate([lo, hi], axis=1).astype(BF16)
            return c

        lax.fori_loop(0, tn // pack, body, 0)

    act = _dot(hn_ref[...], ut_ref[...])
    gl = 0.5 * act * (1.0 + lax.erf(act * INV_SQRT2))
    zs = []
    for c in range(PEER_EB // PEER_KEYS):
        wsl = w_sc[eb * (PEER_EB // PEER_KEYS) + c].astype(F32)
        zs.append((gl[:, LANES * c:LANES * (c + 1)] * wsl).astype(BF16))
    o_ref[...] += _dot(jnp.concatenate(zs, axis=1), v_ref[...])


def _peer(e1t, e2t, gt, hn, h, ut, v):
    n = hn.shape[0]
    tn = PEER_TN
    n_exp = ut.shape[1]
    jn = PEER_HEADS * PEER_TOPK
    sel = pl.BlockSpec((jn, tn), lambda i, e: (0, i))
    row = pl.BlockSpec((tn, D_MODEL), lambda i, e: (i, 0))
    wblk = pl.BlockSpec((PEER_EB, D_MODEL), lambda i, e: (e, 0))
    return pl.pallas_call(
        _peer_body, grid=(n // tn, n_exp // PEER_EB),
        in_specs=[sel, sel, sel, row, row, pl.BlockSpec((D_MODEL, PEER_EB), lambda i, e: (0, e)), wblk],
        out_specs=row, out_shape=jax.ShapeDtypeStruct((n, D_MODEL), F32),
        scratch_shapes=[pltpu.VMEM((tn, jn), F32), pltpu.VMEM((tn, jn), F32), pltpu.VMEM((tn, jn), F32),
                        pltpu.VMEM((2, PEER_KEYS * SUBLANES, LANES), F32),
                        pltpu.VMEM((PEER_KEYS, tn, LANES), BF16)],
        compiler_params=_cparams(("arbitrary", "arbitrary"), 56), name="peer",
    )(e1t, e2t, gt, hn, h, ut, v)


def _pad_last(a, n):
    return jnp.pad(a, [(0, 0)] * (a.ndim - 1) + [(0, n - a.shape[-1])])


def _layer_weights(attn_norm_w, w_in, q_a_norm_w, w_uq, q_norm_w, kv_a_norm_w, w_ukv, k_norm_w,
                   ret_norm_w, w_o, ffn_norm_w, w_peer_q, sub_keys1, sub_keys2, peer_u, peer_v):
    o = np.cumsum([0, Q_LORA, KV_LORA, QK_ROPE, 512, 512, 512, 512])
    zeros = lambda w: jnp.zeros((D_MODEL, w), F32)
    win = jnp.concatenate([w_in[:, o[0]:o[2]], zeros(QK_NOPE), w_in[:, o[2]:o[3]], zeros(LANES - QK_HEAD),
                           w_in[:, o[3]:o[7]]], axis=1)
    wuq = _pad_last(w_uq.reshape(Q_LORA, MLA_HEADS, QK_HEAD), HEAD_PAD).reshape(Q_LORA, MLA_HEADS * HEAD_PAD)
    ukv = w_ukv.reshape(KV_LORA, MLA_HEADS, QK_NOPE + V_HEAD)
    ukn, uv = ukv[:, :, :QK_NOPE], ukv[:, :, QK_NOPE:]
    knw = _pad_last(k_norm_w, LANES)[None]
    wknt = _pad_last(ukn, HEAD_PAD).transpose(1, 2, 0)
    uvp = uv.reshape(KV_LORA, 4, 2, V_HEAD)
    zv = jnp.zeros((KV_LORA, 4, V_HEAD), F32)
    wve = jnp.concatenate([uvp[:, :, 0], zv], -1).transpose(1, 0, 2)
    wvo = jnp.concatenate([zv, uvp[:, :, 1]], -1).transpose(1, 0, 2)
    return dict(
        anw=attn_norm_w[None], win=win.astype(BF16), qanw=q_a_norm_w[None], wuq=wuq.astype(BF16),
        qnw=(_pad_last(q_norm_w, LANES) * (QK_HEAD ** -0.5 * LOG2E))[None], kvnw=kv_a_norm_w[None],
        wukn=_pad_last(ukn, HEAD_PAD).reshape(KV_LORA, MLA_HEADS * HEAD_PAD).astype(BF16),
        wuv=uv.reshape(KV_LORA, MLA_HEADS * V_HEAD).astype(BF16), knw=knw,
        wpe=_pad_last(k_norm_w[QK_NOPE:], LANES)[None], wknt=wknt.astype(BF16),
        wkn_t=ukn.transpose(1, 2, 0).reshape(MLA_HEADS * QK_NOPE, KV_LORA).astype(BF16), wve=wve.astype(BF16), wvo=wvo.astype(BF16),
        rnw=ret_norm_w[None], wo1=w_o[:512].astype(BF16), wo2=w_o[512:].astype(BF16), fnw=ffn_norm_w[None],
        wpq=w_peer_q.astype(BF16), k1=sub_keys1.astype(BF16), k2=sub_keys2.astype(BF16),
        put=peer_u.T.astype(BF16), pv=peer_v.astype(BF16))


def _rope_tables(pos):
    posf = np.asarray(pos, np.float64)[:, None]
    n = posf.shape[0]

    def cos_sin(half):
        ang = posf * (ROPE_THETA ** (-np.arange(half, dtype=np.float64) / half))[None, :]
        return jnp.asarray(np.cos(ang), F32), jnp.asarray(np.sin(ang), F32)

    c, s = cos_sin(QK_ROPE // 2)
    one, zero = jnp.ones((n, QK_NOPE), F32), jnp.zeros((n, QK_NOPE), F32)
    cq = jnp.concatenate([one, c, c, one[:, :LANES - QK_HEAD]], 1)
    sq = jnp.concatenate([zero, -s, s, zero[:, :LANES - QK_HEAD]], 1)
    c2, s2 = cos_sin(RET_DK // 2)
    cr = jnp.concatenate([c2, c2, c2, c2], 1)
    sr = jnp.concatenate([-s2, s2, -s2, s2], 1)
    return dict(cq=cq, sq=sq, cr=cr, sr=sr, cos_t=jnp.concatenate([c, c], 1).T, sin_t=jnp.concatenate([s, s], 1).T)


def kernel(x_prompt, x_sample, cache_kv_latent, cache_k_rope, state_ret, page_table, attn_norm_w, w_in, q_a_norm_w, w_uq, q_norm_w, kv_a_norm_w, w_ukv, k_norm_w, ret_norm_w, w_o, ffn_norm_w, w_peer_q, peer_sub_keys1, peer_sub_keys2, peer_u, peer_v):
    batch, seq, _ = x_prompt.shape
    n_seq, dec_seq, _ = x_sample.shape
    depth = attn_norm_w.shape[0]
    past = page_table.shape[1] * PAGE
    n_p, n_s = batch * seq, n_seq * dec_seq
    nrow = MLA_HEADS * dec_seq

    pos = np.concatenate([np.tile(np.arange(seq), batch), np.tile(past + np.arange(dec_seq), n_seq)])
    tabs = _rope_tables(pos)
    past_tabs = _rope_tables(np.arange(past))
    new_tabs = _rope_tables(past + np.arange(PAGE))

    h_all = jnp.concatenate([x_prompt.reshape(n_p, D_MODEL), x_sample.reshape(n_s, D_MODEL)], 0)
    outs = [[] for _ in range(6)]
    for l in range(depth):
        wts = _layer_weights(attn_norm_w[l], w_in[l], q_a_norm_w[l], w_uq[l], q_norm_w[l], kv_a_norm_w[l], w_ukv[l],
                             k_norm_w[l], ret_norm_w[l], w_o[l], ffn_norm_w[l], w_peer_q[l], peer_sub_keys1[l],
                             peer_sub_keys2[l], peer_u[l], peer_v[l])
        q, k, vt, ckv, kpe_t, rq, rk, rv, rg = _mixer(h_all, wts, tabs)
        kpe_s = kpe_t[:, n_p:].reshape(QK_ROPE, n_seq, dec_seq)

        mla_p = _pattn(q, k, vt, batch, seq)
        reto_p, r_p = _pret(rq, rk, rv, batch, seq)
        r_p = jnp.stack([r_p[:, :, :RET_DK, :RET_DV], r_p[:, :, RET_DK:, RET_DV:]], 2).reshape(batch, RET_HEADS, RET_DK, RET_DV)

        qa, qcs = _sabs(q[n_p:], wts["knw"], wts["wpe"], wts["wknt"])
        cnew = jnp.pad(ckv[n_p:].reshape(n_seq, dec_seq, KV_LORA), ((0, 0), (0, PAGE - dec_seq), (0, 0)))
        pnew_t = jnp.pad(kpe_s.transpose(1, 0, 2), ((0, 0), (0, 0), (0, PAGE - dec_seq)))
        pc = _sattn(page_table, qa.reshape(n_seq, nrow, HEAD_PAD), qcs.reshape(n_seq, nrow, HEAD_PAD), wts["wkn_t"],
                    past_tabs["cos_t"], past_tabs["sin_t"], cnew, pnew_t, new_tabs["cos_t"], new_tabs["sin_t"],
                    cache_kv_latent[l], cache_k_rope[l].transpose(0, 2, 1), dec_seq)
        pc_h = pc.reshape(n_seq, dec_seq, MLA_HEADS, KV_LORA).transpose(2, 0, 1, 3).reshape(MLA_HEADS, n_s, KV_LORA)
        mla_s = _sup(pc_h, wts["wve"], wts["wvo"])
        seq_last = lambda a: a[n_p:].astype(F32).reshape(n_seq, dec_seq, 512).transpose(1, 2, 0)
        reto_s, r_s = _sret(seq_last(rq), seq_last(rk), seq_last(rv), state_ret[l].astype(F32).transpose(1, 2, 3, 0),
                            dec_seq)
        reto_s = reto_s.transpose(2, 0, 1).reshape(n_s, 512)
        r_s = r_s.transpose(3, 0, 1, 2)

        mla = jnp.concatenate([mla_p, mla_s], 0)
        reto = jnp.concatenate([reto_p, reto_s], 0)
        h_mid, hn, s1t, s2t = _post(h_all, mla, reto, rg, wts)
        e1t, e2t, gt = _topk(s1t, s2t)
        h_all = _peer(e1t, e2t, gt, hn, h_mid, wts["put"], wts["pv"])

        outs[0].append(ckv[:n_p].reshape(batch, seq, KV_LORA))
        outs[1].append(kpe_t[:, :n_p].reshape(QK_ROPE, batch, seq).transpose(1, 2, 0))
        outs[2].append(r_p)
        outs[3].append(ckv[n_p:].reshape(n_seq, dec_seq, KV_LORA))
        outs[4].append(kpe_s.transpose(1, 2, 0))
        outs[5].append(r_s)
    return (h_all[:n_p].reshape(batch, seq, D_MODEL), h_all[n_p:].reshape(n_seq, dec_seq, D_MODEL),
            *[jnp.stack(o) for o in outs])
```

```python
import functools

import numpy as np
import jax
import jax.numpy as jnp
from jax import lax
from jax.experimental import pallas as pl
from jax.experimental.pallas import tpu as pltpu

F32 = jnp.float32
BF16 = jnp.bfloat16

D_MODEL = 1024
PAGE = 128
MLA_HEADS = 8
QK_NOPE = 64
QK_ROPE = 32
QK_HEAD = QK_NOPE + QK_ROPE
V_HEAD = 64
Q_LORA = 256
KV_LORA = 128
RET_HEADS = 8
RET_DK = 64
RET_DV = 64
RET_CHUNK = 128
ROPE_THETA = 10000.0
PEER_KEYS = 128
PEER_HEADS = 8
PEER_QDIM = 256
PEER_TOPK = 16
EPS = 1e-6
LANES = 128
SUBLANES = 8
HEAD_PAD = 128
NEG_INF = float("-inf")
INV_SQRT2 = 0.7071067811865476
LOG2E = 1.4426950408889634

MIX_TM = 256
ATT_T = 512
ATT_CH = 32
SAMP_SUB = 2048
SAMP_CHUNK = 512
TOPK_TT = 256
TOPK_HEADS = 2
PEER_TN = 512
PEER_EB = 1024


def _cparams(sem, vmem_mb):
    return pltpu.CompilerParams(dimension_semantics=sem, vmem_limit_bytes=vmem_mb << 20)


def _full(shape):
    n = len(shape)
    return pl.BlockSpec(shape, lambda *_: (0,) * n)


def _rms(x, w, n):
    return x * lax.rsqrt(jnp.sum(x * x, -1, keepdims=True) * (1.0 / n) + EPS) * w


def _dot(a, b):
    return jnp.dot(a, b, preferred_element_type=F32)


def _dot_nt(a, b):
    return lax.dot_general(a, b, (((1,), (1,)), ((), ())), preferred_element_type=F32)


def _dot_tn(a, b):
    return lax.dot_general(a, b, (((0,), (0,)), ((), ())), preferred_element_type=F32)


def _mixer_body(x_ref, anw_ref, win_ref, qanw_ref, wuq_ref, qnw_ref, kvnw_ref, wukn_ref, wuv_ref, knw_ref,
                cq_ref, sq_ref, cr_ref, sr_ref,
                q_ref, k_ref, vt_ref, ckv_ref, kpe_ref, rq_ref, rk_ref, rv_ref, rg_ref):
    x = x_ref[...]
    xn = _rms(x, anw_ref[...], D_MODEL)
    proj = _dot(xn.astype(BF16), win_ref[...])
    lane = lax.broadcasted_iota(jnp.int32, (x.shape[0], LANES), 1)
    cq, sq, cr, sr = cq_ref[...], sq_ref[...], cr_ref[...], sr_ref[...]
    lo_q = lane < QK_NOPE + QK_ROPE // 2
    lo_r = (lane & (RET_DK - 1)) < RET_DK // 2

    def rope_q(v):
        return v * cq + jnp.where(lo_q, pltpu.roll(v, LANES - 16, 1), pltpu.roll(v, 16, 1)) * sq

    def rope_r(v):
        return v * cr + jnp.where(lo_r, pltpu.roll(v, LANES - 32, 1), pltpu.roll(v, 32, 1)) * sr

    q_lat = _rms(proj[:, 0:256], qanw_ref[...], Q_LORA)
    qf = _dot(q_lat.astype(BF16), wuq_ref[...])
    ckv = _rms(proj[:, 256:384], kvnw_ref[...], KV_LORA)
    ckv_ref[...] = ckv
    kslot = proj[:, 384:512]
    kpe_ref[...] = kslot.T[QK_NOPE:QK_HEAD, :]
    cb = ckv.astype(BF16)
    knf = _dot(cb, wukn_ref[...])
    vt_ref[...] = _dot(cb, wuv_ref[...]).T.astype(BF16)
    qnw, knw = qnw_ref[...], knw_ref[...]
    for h in range(MLA_HEADS):
        sl = slice(HEAD_PAD * h, HEAD_PAD * (h + 1))
        q_ref[:, sl] = rope_q(_rms(qf[:, sl], qnw, QK_HEAD)).astype(BF16)
        k_ref[:, sl] = rope_q(_rms(knf[:, sl] + kslot, knw, QK_HEAD)).astype(BF16)
    for p in range(4):
        sl = slice(LANES * p, LANES * (p + 1))
        rq_ref[:, sl] = rope_r(proj[:, 512 + LANES * p:512 + LANES * (p + 1)]).astype(BF16)
        rk_ref[:, sl] = (rope_r(proj[:, 1024 + LANES * p:1024 + LANES * (p + 1)]) * (RET_DK ** -0.5)).astype(BF16)
    rv_ref[...] = proj[:, 1536:2048].astype(BF16)
    rg_ref[...] = proj[:, 2048:2560]


def _mixer(x_all, wts, tabs):
    n = x_all.shape[0]
    tm = MIX_TM
    row = lambda w: pl.BlockSpec((tm, w), lambda i: (i, 0))
    ins = [row(D_MODEL), _full((1, D_MODEL)), _full((D_MODEL, 2560)), _full((1, Q_LORA)), _full((Q_LORA, 1024)),
           _full((1, LANES)), _full((1, KV_LORA)), _full((KV_LORA, 1024)), _full((KV_LORA, 512)), _full((1, LANES)),
           row(LANES), row(LANES), row(LANES), row(LANES)]
    outs = [row(1024), row(1024), pl.BlockSpec((512, tm), lambda i: (0, i)), row(KV_LORA),
            pl.BlockSpec((QK_ROPE, tm), lambda i: (0, i)), row(512),
            row(512), row(512), row(512)]
    shapes = [jax.ShapeDtypeStruct((n, 1024), BF16), jax.ShapeDtypeStruct((n, 1024), BF16),
              jax.ShapeDtypeStruct((512, n), BF16), jax.ShapeDtypeStruct((n, KV_LORA), F32),
              jax.ShapeDtypeStruct((QK_ROPE, n), F32), jax.ShapeDtypeStruct((n, 512), BF16),
              jax.ShapeDtypeStruct((n, 512), BF16), jax.ShapeDtypeStruct((n, 512), BF16),
              jax.ShapeDtypeStruct((n, 512), F32)]
    return pl.pallas_call(
        _mixer_body, grid=(n // tm,), in_specs=ins, out_specs=outs, out_shape=shapes,
        compiler_params=_cparams(("arbitrary",), 48), name="mixer",
    )(x_all, wts["anw"], wts["win"], wts["qanw"], wts["wuq"], wts["qnw"], wts["kvnw"], wts["wukn"], wts["wuv"],
      wts["knw"], tabs["cq"], tabs["sq"], tabs["cr"], tabs["sr"])


def _pattn_body(q_ref, k_ref, vt_ref, o_ref, st0, st1, pt0, pt1, acc_sc):
    qi = pl.program_id(2)
    t = ATT_T
    ch = ATT_CH
    krow = lax.broadcasted_iota(jnp.int32, (ch, t), 0)
    qcol = lax.broadcasted_iota(jnp.int32, (ch, t), 1)
    acc_sc[...] = jnp.zeros(acc_sc.shape, F32)
    st_sc, pt_sc = (st0, st1), (pt0, pt1)

    def scores(j, slot):
        start = pl.multiple_of(j * t, t)
        for hh in range(2):
            lsl = slice(HEAD_PAD * hh, HEAD_PAD * (hh + 1))
            st_sc[hh][slot] = _dot_nt(k_ref[pl.ds(start, t), lsl], q_ref[:, lsl])

    def update(j, slot, carry, masked):
        start = pl.multiple_of(j * t, t)
        out = []
        for hh in range(2):
            m_prev, l_prev = carry[2 * hh], carry[2 * hh + 1]

            def chunk(r):
                blk = st_sc[hh][slot, ch * r:ch * (r + 1), :]
                return jnp.where(krow + ch * r <= qcol, blk, NEG_INF) if masked else blk

            mc = chunk(0)
            for r in range(1, t // ch):
                mc = jnp.maximum(mc, chunk(r))
            m_new = jnp.maximum(m_prev, jnp.max(mc, 0, keepdims=True))
            alpha = jnp.exp2(m_prev - m_new)
            ls = jnp.zeros((ch, t), F32)
            for r in range(t // ch):
                p = jnp.exp2(chunk(r) - m_new)
                ls = ls + p
                pt_sc[hh][ch * r:ch * (r + 1), :] = p.astype(BF16)
            acc_sc[hh] = alpha * acc_sc[hh] + _dot(vt_ref[:, pl.ds(start, t)], pt_sc[hh][...])
            out += [m_new, alpha * l_prev + jnp.sum(ls, 0, keepdims=True)]
        return tuple(out)

    def step(j, slot, carry):
        scores(j + 1, 1 - slot)
        return update(j, slot, carry, False)

    def pair(i, carry):
        return step(2 * i + 1, 1, step(2 * i, 0, carry))

    scores(0, 0)
    init = (jnp.full((1, t), NEG_INF, F32), jnp.zeros((1, t), F32)) * 2
    carry = lax.fori_loop(0, qi // 2, pair, init)
    _, l0, _, l1 = lax.cond(qi % 2 == 1,
                            lambda c: update(qi, 1, step(qi - 1, 0, c), True),
                            lambda c: update(qi, 0, c, True), carry)
    drow = lax.broadcasted_iota(jnp.int32, (LANES, t), 0)
    o_ref[...] = jnp.where(drow < V_HEAD, acc_sc[0] / l0, acc_sc[1] / l1).T.astype(o_ref.dtype)


def _pattn(q_all, k_all, vt_all, batch, seq):
    t = ATT_T
    nq = seq // t
    return pl.pallas_call(
        _pattn_body, grid=(batch, MLA_HEADS // 2, nq),
        in_specs=[pl.BlockSpec((t, 2 * HEAD_PAD), lambda b, hp, qi: (b * nq + qi, hp)),
                  pl.BlockSpec((seq, 2 * HEAD_PAD), lambda b, hp, qi: (b, hp)),
                  pl.BlockSpec((LANES, seq), lambda b, hp, qi: (hp, b))],
        out_specs=pl.BlockSpec((t, LANES), lambda b, hp, qi: (b * nq + qi, hp)),
        out_shape=jax.ShapeDtypeStruct((batch * seq, MLA_HEADS * V_HEAD), BF16),
        scratch_shapes=[pltpu.VMEM((2, t, t), F32), pltpu.VMEM((2, t, t), F32), pltpu.VMEM((t, t), BF16),
                        pltpu.VMEM((t, t), BF16), pltpu.VMEM((2, LANES, t), F32)],
        compiler_params=_cparams(("arbitrary", "arbitrary", "arbitrary"), 48), name="pattn",
    )(q_all, k_all, vt_all)


def _ret_log_decay():
    return jnp.log1p(-jnp.power(2.0, -5.0 - jnp.arange(RET_HEADS, dtype=F32)))


def _ret_tables(c):
    lg = _ret_log_decay()
    i = jnp.arange(c, dtype=F32)
    diff = i[:, None] - i[None, :]
    dmat = jnp.where(diff >= 0, jnp.exp(lg[:, None, None] * jnp.maximum(diff, 0.0)), 0.0)
    lane_head = lambda a: jnp.repeat(a.reshape(a.shape[0], 4, 2), RET_DV, axis=2).transpose(1, 0, 2)
    cross = lane_head(jnp.exp(lg[None, :] * (i[:, None] + 1.0)))
    kdec = lane_head(jnp.exp(lg[None, :] * (c - 1.0 - i[:, None])))
    gc = lane_head(jnp.exp(lg * c)[None, :])
    return dmat[0::2], dmat[1::2], cross, kdec, gc


def _pret_body(rq_ref, rk_ref, rv_ref, de_ref, do_ref, cross_ref, kdec_ref, gc_ref, o_ref, rout_ref, r_sc):
    c = pl.program_id(1)

    @pl.when(c == 0)
    def _():
        r_sc[...] = jnp.zeros(r_sc.shape, F32)

    n = rq_ref.shape[0]
    lane = lax.broadcasted_iota(jnp.int32, (n, LANES), 1)
    even = lane < RET_DK
    rr = lax.broadcasted_iota(jnp.int32, (LANES, LANES), 0)
    cc = lax.broadcasted_iota(jnp.int32, (LANES, LANES), 1)
    same_head = (rr < RET_DK) == (cc < RET_DV)
    for p in range(4):
        sl = slice(LANES * p, LANES * (p + 1))
        q, k, v = rq_ref[:, sl], rk_ref[:, sl], rv_ref[:, sl]
        zero = jnp.zeros_like(q)
        se = _dot_nt(jnp.where(even, q, zero), k)
        so = _dot_nt(jnp.where(even, zero, q), k)
        oe = _dot((se * de_ref[p]).astype(BF16), v)
        oo = _dot((so * do_ref[p]).astype(BF16), v)
        r = r_sc[p]
        cross = _dot(q, r.astype(BF16)) * cross_ref[p]
        o_ref[:, sl] = jnp.where(even, oe, oo) + cross
        kd = (k.astype(F32) * kdec_ref[p]).astype(BF16)
        upd = _dot_tn(kd, v)
        r_sc[p] = r * gc_ref[p] + jnp.where(same_head, upd, 0.0)

    @pl.when(c == pl.num_programs(1) - 1)
    def _():
        rout_ref[0] = r_sc[...]


def _pret(rq, rk, rv, batch, seq):
    c = RET_CHUNK
    nc = seq // c
    tabs = _ret_tables(c)
    blk = pl.BlockSpec((c, 512), lambda b, i: (b * nc + i, 0))
    return pl.pallas_call(
        _pret_body, grid=(batch, nc),
        in_specs=[blk, blk, blk, _full((4, c, c)), _full((4, c, c)), _full((4, c, LANES)), _full((4, c, LANES)),
                  _full((4, 1, LANES))],
        out_specs=[blk, pl.BlockSpec((1, 4, LANES, LANES), lambda b, i: (b, 0, 0, 0))],
        out_shape=[jax.ShapeDtypeStruct((batch * seq, 512), F32),
                   jax.ShapeDtypeStruct((batch, 4, LANES, LANES), F32)],
        scratch_shapes=[pltpu.VMEM((4, LANES, LANES), F32)],
        compiler_params=_cparams(("arbitrary", "arbitrary"), 32), name="pret",
    )(rq, rk, rv, *tabs)


def _sabs_body(q_ref, knw_ref, wpe_ref, wknt_ref, qa_ref, qcs_ref):
    q = q_ref[...].astype(F32)
    qa_ref[...] = _dot((q * knw_ref[...]).astype(BF16), wknt_ref[0]).astype(BF16)
    lane = lax.broadcasted_iota(jnp.int32, q.shape, 1)
    qpe = pltpu.roll(q, LANES - QK_NOPE, 1)
    half = QK_ROPE // 2
    swapped = jnp.where(lane < half, pltpu.roll(qpe, LANES - half, 1), -pltpu.roll(qpe, half, 1))
    keep = lane < QK_ROPE
    wpe = wpe_ref[...]
    qc = jnp.where(keep, qpe * wpe, 0.0)
    qs = jnp.where(keep, swapped * wpe, 0.0)
    qcs_ref[...] = (qc + pltpu.roll(qs, QK_ROPE, 1)).astype(BF16)


def _sabs(q_s, knw, wpe, wknt):
    n = q_s.shape[0]
    blk = pl.BlockSpec((n, HEAD_PAD), lambda h: (0, h))
    shp = jax.ShapeDtypeStruct((n, MLA_HEADS * HEAD_PAD), BF16)
    return pl.pallas_call(
        _sabs_body, grid=(MLA_HEADS,),
        in_specs=[blk, _full((1, LANES)), _full((1, LANES)), pl.BlockSpec((1, LANES, KV_LORA), lambda h: (h, 0, 0))],
        out_specs=[blk, blk], out_shape=[shp, shp],
        compiler_params=_cparams(("arbitrary",), 32), name="sabs",
    )(q_s, knw, wpe, wknt)


def _sattn_body(pt_ref, qa_ref, qcs_ref, wkn_ref, cost_ref, sint_ref, cnew_ref, pnew_ref, cosn_ref, sinn_ref,
                lat_hbm, pe_hbm, pc_ref, latbuf, pebuf, cbuf, lhs_sc, xcs, m_sc, l_sc, acc_sc, sem, *, n_pages, dec_seq):
    s_id = pl.program_id(0)
    slot = s_id & 1
    nrow = qa_ref.shape[1]
    nkn = wkn_ref.shape[0]

    def page_copies(page_id, slt, i):
        cols = pl.ds(PAGE * i, PAGE)
        return (pltpu.make_async_copy(lat_hbm.at[page_id], latbuf.at[slt, cols, :], sem.at[slt, 0]),
                pltpu.make_async_copy(pe_hbm.at[page_id], pebuf.at[slt, :, cols], sem.at[slt, 1]))

    def start_sequence(seq, slt):
        for i in range(n_pages):
            for cp in page_copies(pt_ref[seq, i], slt, i):
                cp.start()

    @pl.when(s_id == 0)
    def _():
        xcs[...] = jnp.zeros(xcs.shape, BF16)
        lhs_sc[0:nkn, :] = wkn_ref[...]
        start_sequence(0, 0)

    @pl.when(s_id + 1 < pl.num_programs(0))
    def _():
        start_sequence(s_id + 1, 1 - slot)

    lhs_sc[nkn:nkn + nrow, :] = qa_ref[0]
    m_sc[...] = jnp.full(m_sc.shape, NEG_INF, F32)
    l_sc[...] = jnp.zeros(l_sc.shape, F32)
    acc_sc[...] = jnp.zeros(acc_sc.shape, F32)

    row8 = lax.broadcasted_iota(jnp.int32, (MLA_HEADS, 1), 0)

    def attend(nk, pe_t, cos_t, sin_t, mask):
        xcs[0:QK_ROPE, 0:nk] = (pe_t * cos_t).astype(BF16)
        xcs[QK_ROPE:2 * QK_ROPE, 0:nk] = (pe_t * sin_t).astype(BF16)
        pe2 = jnp.sum(pe_t * pe_t, 0, keepdims=True)
        lhs = lhs_sc[...]
        qcs = qcs_ref[0]
        parts = []
        ck = min(nk, SAMP_CHUNK)
        for c0 in range(0, nk, ck):
            c = cbuf[c0:c0 + ck, :]
            big = _dot_nt(lhs, c)
            ss = pe2[:, c0:c0 + ck] + jnp.zeros((MLA_HEADS, ck), F32)
            for h in range(MLA_HEADS):
                kh = big[QK_NOPE * h:QK_NOPE * (h + 1), :]
                ss = ss + jnp.where(row8 == h, jnp.sum(kh * kh, 0, keepdims=True), 0.0)
            rn = lax.rsqrt(ss * (1.0 / QK_HEAD) + EPS)
            sc = big[nkn:nkn + nrow, :] + _dot(qcs, xcs[:, c0:c0 + ck])
            parts.append((sc.reshape(dec_seq, MLA_HEADS, ck) * rn[None]).reshape(nrow, ck))
        s = parts[0] if len(parts) == 1 else jnp.concatenate(parts, axis=1)
        if mask is not None:
            s = jnp.where(mask, s, NEG_INF)
        m_prev = m_sc[...]
        m_new = jnp.maximum(m_prev, jnp.max(s, -1, keepdims=True))
        alpha = jnp.exp2(m_prev - m_new)
        p = jnp.exp2(s - m_new)
        l_sc[...] = alpha * l_sc[...] + jnp.sum(p, -1, keepdims=True)
        acc_sc[...] = alpha * acc_sc[...] + _dot(p.astype(BF16), cbuf[0:nk, :])
        m_sc[...] = m_new

    for i in range(n_pages):
        for cp in page_copies(0, slot, i):
            cp.wait()
    sub = SAMP_SUB
    for k0 in range(0, PAGE * n_pages, sub):
        cbuf[0:sub, :] = latbuf[slot, k0:k0 + sub, :].astype(BF16)
        attend(sub, pebuf[slot, :, k0:k0 + sub], cost_ref[:, k0:k0 + sub], sint_ref[:, k0:k0 + sub], None)
    cbuf[0:PAGE, :] = cnew_ref[0].astype(BF16)
    qtok = lax.shift_right_logical(lax.broadcasted_iota(jnp.int32, (nrow, PAGE), 0), 3)
    kcol = lax.broadcasted_iota(jnp.int32, (nrow, PAGE), 1)
    attend(PAGE, pnew_ref[0], cosn_ref[...], sinn_ref[...], jnp.logical_and(kcol <= qtok, kcol < dec_seq))
    pc_ref[0] = acc_sc[...] / l_sc[...]


def _sattn(page_table, qa, qcs, wkn_t, cos_t, sin_t, cnew, pnew_t, cosn_t, sinn_t, cache_lat, cache_pe_t, dec_seq):
    n_seq, nrow, _ = qa.shape
    n_pages = page_table.shape[1]
    past = n_pages * PAGE
    nkn = wkn_t.shape[0]
    assert MLA_HEADS == SUBLANES and nrow == dec_seq * MLA_HEADS and past % SAMP_SUB == 0
    qblk = pl.BlockSpec((1, nrow, LANES), lambda s, pt: (s, 0, 0))
    const2 = lambda shape: pl.BlockSpec(shape, lambda s, pt: (0, 0))
    hbm = pl.BlockSpec(memory_space=pl.ANY)
    in_specs = [qblk, qblk, const2((nkn, KV_LORA)), const2((QK_ROPE, past)), const2((QK_ROPE, past)),
                pl.BlockSpec((1, PAGE, KV_LORA), lambda s, pt: (s, 0, 0)),
                pl.BlockSpec((1, QK_ROPE, PAGE), lambda s, pt: (s, 0, 0)),
                const2((QK_ROPE, PAGE)), const2((QK_ROPE, PAGE)), hbm, hbm]
    gs = pltpu.PrefetchScalarGridSpec(
        num_scalar_prefetch=1, grid=(n_seq,), in_specs=in_specs,
        out_specs=pl.BlockSpec((1, nrow, LANES), lambda s, pt: (s, 0, 0)),
        scratch_shapes=[pltpu.VMEM((2, past, KV_LORA), F32), pltpu.VMEM((2, QK_ROPE, past), F32),
                        pltpu.VMEM((SAMP_SUB, LANES), BF16), pltpu.VMEM((nkn + nrow, KV_LORA), BF16),
                        pltpu.VMEM((LANES, SAMP_SUB), BF16), pltpu.VMEM((nrow, 1), F32), pltpu.VMEM((nrow, 1), F32),
                        pltpu.VMEM((nrow, LANES), F32), pltpu.SemaphoreType.DMA((2, 2))])
    return pl.pallas_call(
        functools.partial(_sattn_body, n_pages=n_pages, dec_seq=dec_seq), grid_spec=gs,
        out_shape=jax.ShapeDtypeStruct((n_seq, nrow, LANES), F32),
        compiler_params=_cparams(("arbitrary",), 48), name="sattn",
    )(page_table, qa, qcs, wkn_t, cos_t, sin_t, cnew, pnew_t, cosn_t, sinn_t, cache_lat, cache_pe_t)


def _sup_body(pc_ref, we_ref, wo_ref, o_ref):
    o_ref[...] = (_dot(pc_ref[0].astype(BF16), we_ref[0]) + _dot(pc_ref[1].astype(BF16), wo_ref[0])).astype(o_ref.dtype)


def _sup(pc_h, wve, wvo):
    n = pc_h.shape[1]
    return pl.pallas_call(
        _sup_body, grid=(MLA_HEADS // 2,),
        in_specs=[pl.BlockSpec((2, n, KV_LORA), lambda p: (p, 0, 0)),
                  pl.BlockSpec((1, KV_LORA, LANES), lambda p: (p, 0, 0)),
                  pl.BlockSpec((1, KV_LORA, LANES), lambda p: (p, 0, 0))],
        out_specs=pl.BlockSpec((n, LANES), lambda p: (0, p)),
        out_shape=jax.ShapeDtypeStruct((n, MLA_HEADS * V_HEAD), BF16),
        compiler_params=_cparams(("arbitrary",), 32), name="sup",
    )(pc_h, wve, wvo)


def _sret_body(q_ref, k_ref, v_ref, r0_ref, d_ref, cross_ref, kdec_ref, g_ref, o_ref, r_ref):
    nt = q_ref.shape[0]
    outs = []
    for i in range(nt):
        acc = jnp.zeros(v_ref.shape[1:], F32)
        for j in range(i + 1):
            sij = jnp.sum(q_ref[i] * k_ref[j], 0, keepdims=True) * d_ref[0, i * nt + j:i * nt + j + 1, :]
            acc = acc + sij * v_ref[j]
        outs.append(acc)

    def body(d, carry):
        r = r0_ref[0, d]
        row = pl.ds(d, 1)
        rn = g_ref[0] * r
        new = []
        for i in range(nt):
            new.append(carry[i] + (q_ref[i, row, :] * cross_ref[0, i:i + 1, :]) * r)
            rn = rn + (k_ref[i, row, :] * kdec_ref[0, i:i + 1, :]) * v_ref[i]
        r_ref[0, d] = rn
        return tuple(new)

    outs = lax.fori_loop(0, RET_DK, body, tuple(outs))
    for i in range(nt):
        o_ref[i] = outs[i]


def _sret(q_t, k_t, v_t, r0_t, dec_seq):
    n_seq = q_t.shape[2]
    lg = _ret_log_decay()
    i = jnp.arange(dec_seq, dtype=F32)
    lanes = jnp.ones((1, 1, n_seq), F32)
    diff = i[:, None] - i[None, :]
    dmat = jnp.where(diff >= 0, jnp.exp(lg[:, None, None] * jnp.maximum(diff, 0.0)), 0.0)
    dtab = dmat.reshape(RET_HEADS, dec_seq * dec_seq, 1) * lanes
    cross = jnp.exp(lg[:, None] * (i[None, :] + 1.0))[:, :, None] * lanes
    kdec = jnp.exp(lg[:, None] * (dec_seq - 1.0 - i[None, :]))[:, :, None] * lanes
    gdec = jnp.exp(lg * dec_seq)[:, None, None] * lanes
    blk = pl.BlockSpec((dec_seq, RET_DK, n_seq), lambda h: (0, h, 0))
    rblk = pl.BlockSpec((1, RET_DK, RET_DV, n_seq), lambda h: (h, 0, 0, 0))
    tab = lambda r: pl.BlockSpec((1, r, n_seq), lambda h: (h, 0, 0))
    return pl.pallas_call(
        _sret_body, grid=(RET_HEADS,),
        in_specs=[blk, blk, blk, rblk, tab(dec_seq * dec_seq), tab(dec_seq), tab(dec_seq), tab(1)],
        out_specs=[blk, rblk],
        out_shape=[jax.ShapeDtypeStruct((dec_seq, RET_HEADS * RET_DV, n_seq), F32),
                   jax.ShapeDtypeStruct((RET_HEADS, RET_DK, RET_DV, n_seq), F32)],
        compiler_params=_cparams(("arbitrary",), 32), name="sret",
    )(q_t, k_t, v_t, r0_t, dtab, cross, kdec, gdec)


def _post_body(x_ref, mla_ref, reto_ref, rg_ref, rnw_ref, wo1_ref, wo2_ref, fnw_ref, wpq_ref, k1_ref, k2_ref,
               h_ref, hn_ref, s1_ref, s2_ref):
    tm = x_ref.shape[0]
    lane = lax.broadcasted_iota(jnp.int32, (tm, LANES), 1)
    lo = lane < RET_DV

    def group(a):
        s_lo = jnp.sum(jnp.where(lo, a, 0.0), -1, keepdims=True)
        s_hi = jnp.sum(jnp.where(lo, 0.0, a), -1, keepdims=True)
        return jnp.where(lo, s_lo, s_hi)

    parts = []
    for p in range(4):
        sl = slice(LANES * p, LANES * (p + 1))
        o = reto_ref[:, sl]
        d = o - group(o) * (1.0 / RET_DV)
        y = d * lax.rsqrt(group(d * d) * (1.0 / RET_DV) + EPS) * rnw_ref[:, sl]
        rg = rg_ref[:, sl]
        parts.append((rg * jax.nn.sigmoid(rg) * y).astype(BF16))
    ret_out = jnp.concatenate(parts, axis=1)
    h = x_ref[...] + _dot(mla_ref[...], wo1_ref[...]) + _dot(ret_out, wo2_ref[...])
    h_ref[...] = h
    hn = _rms(h, fnw_ref[...], D_MODEL).astype(BF16)
    hn_ref[...] = hn
    pq = _dot(hn, wpq_ref[...])
    half = PEER_QDIM // 2
    for hd in range(PEER_HEADS):
        q1 = pq[:, PEER_QDIM * hd:PEER_QDIM * hd + half].astype(BF16)
        q2 = pq[:, PEER_QDIM * hd + half:PEER_QDIM * (hd + 1)].astype(BF16)
        s1_ref[hd] = _dot_nt(k1_ref[...], q1)
        s2_ref[hd] = _dot_nt(k2_ref[...], q2)


def _post(x_all, mla, reto, rg, wts):
    n = x_all.shape[0]
    tm = MIX_TM
    row = lambda w: pl.BlockSpec((tm, w), lambda i: (i, 0))
    sblk = pl.BlockSpec((PEER_HEADS, PEER_KEYS, tm), lambda i: (0, 0, i))
    half = PEER_QDIM // 2
    return pl.pallas_call(
        _post_body, grid=(n // tm,),
        in_specs=[row(D_MODEL), row(512), row(512), row(512), _full((1, 512)), _full((512, D_MODEL)),
                  _full((512, D_MODEL)), _full((1, D_MODEL)), _full((D_MODEL, PEER_HEADS * PEER_QDIM)),
                  _full((PEER_KEYS, half)), _full((PEER_KEYS, half))],
        out_specs=[row(D_MODEL), row(D_MODEL), sblk, sblk],
        out_shape=[jax.ShapeDtypeStruct((n, D_MODEL), F32), jax.ShapeDtypeStruct((n, D_MODEL), BF16),
                   jax.ShapeDtypeStruct((PEER_HEADS, PEER_KEYS, n), F32),
                   jax.ShapeDtypeStruct((PEER_HEADS, PEER_KEYS, n), F32)],
        compiler_params=_cparams(("arbitrary",), 48), name="post",
    )(x_all, mla, reto, rg, wts["rnw"], wts["wo1"], wts["wo2"], wts["fnw"], wts["wpq"], wts["k1"], wts["k2"])


def _topk_body(s1_ref, s2_ref, e1_ref, e2_ref, g_ref):
    kk = PEER_TOPK
    tt = s1_ref.shape[2]
    row16 = lax.broadcasted_iota(jnp.int32, (kk, tt), 0)

    def take_top(x):
        nrow = x.shape[0]
        row = lax.broadcasted_iota(jnp.int32, x.shape, 0)
        vals = jnp.zeros((kk, tt), F32)
        idxs = jnp.zeros((kk, tt), jnp.int32)
        for r in range(kk):
            m = jnp.max(x, axis=0, keepdims=True)
            idx = jnp.min(jnp.where(x == m, row, nrow), axis=0, keepdims=True)
            vals = jnp.where(row16 == r, m, vals)
            idxs = jnp.where(row16 == r, idx, idxs)
            x = jnp.where(row == idx, NEG_INF, x)
        return vals, idxs

    def pick(table, sel):
        return jnp.sum(jnp.where(row16 == sel, table, 0), axis=0, keepdims=True)

    for hh in range(TOPK_HEADS):
        v1, i1 = take_top(s1_ref[hh])
        v2, i2 = take_top(s2_ref[hh])
        cur = v1 + v2[0:1, :]
        ptr = jnp.zeros((kk, tt), jnp.int32)
        vs = jnp.zeros((kk, tt), F32)
        e1 = jnp.zeros((kk, tt), jnp.int32)
        e2 = jnp.zeros((kk, tt), jnp.int32)
        for r in range(kk):
            m = jnp.max(cur, axis=0, keepdims=True)
            a_sel = row16 == jnp.min(jnp.where(cur == m, row16, kk), axis=0, keepdims=True)
            b_cur = jnp.sum(jnp.where(a_sel, ptr, 0), axis=0, keepdims=True)
            vs = jnp.where(row16 == r, m, vs)
            e1 = jnp.where(row16 == r, jnp.sum(jnp.where(a_sel, i1, 0), axis=0, keepdims=True), e1)
            e2 = jnp.where(row16 == r, pick(i2, b_cur), e2)
            b_next = b_cur + 1
            head = jnp.sum(jnp.where(a_sel, v1, 0.0), axis=0, keepdims=True) + pick(v2, b_next)
            cur = jnp.where(a_sel, jnp.where(b_next < kk, head, NEG_INF), cur)
            ptr = jnp.where(a_sel, b_next, ptr)
        p = jnp.exp(vs - jnp.max(vs, axis=0, keepdims=True))
        g_ref[kk * hh:kk * (hh + 1), :] = p / jnp.sum(p, axis=0, keepdims=True)
        e1_ref[kk * hh:kk * (hh + 1), :] = e1.astype(F32)
        e2_ref[kk * hh:kk * (hh + 1), :] = e2.astype(F32)


def _topk(s1t, s2t):
    n = s1t.shape[2]
    tt = TOPK_TT
    sblk = pl.BlockSpec((TOPK_HEADS, PEER_KEYS, tt), lambda i, h: (h, 0, i))
    oblk = pl.BlockSpec((TOPK_HEADS * PEER_TOPK, tt), lambda i, h: (h, i))
    shp = jax.ShapeDtypeStruct((PEER_HEADS * PEER_TOPK, n), F32)
    return pl.pallas_call(
        _topk_body, grid=(n // tt, PEER_HEADS // TOPK_HEADS), in_specs=[sblk, sblk], out_specs=[oblk, oblk, oblk],
        out_shape=[shp, shp, shp],
        compiler_params=_cparams(("arbitrary", "arbitrary"), 32), name="topk",
    )(s1t, s2t)


def _peer_body(e1_ref, e2_ref, g_ref, hn_ref, h_ref, ut_ref, v_ref, o_ref, e1_sc, e2_sc, g_sc, stg_sc, w_sc):
    eb = pl.program_id(1)
    tn = hn_ref.shape[0]
    pack = 2 * SUBLANES

    @pl.when(eb == 0)
    def _():
        e1_sc[...] = e1_ref[...].T
        e2_sc[...] = e2_ref[...].T
        g_sc[...] = g_ref[...].T
        o_ref[...] = h_ref[...]
        iot = lax.broadcasted_iota(jnp.int32, (PEER_KEYS, LANES), 0).astype(F32)

        def body(grp, c):
            for half in range(2):
                rows = pl.ds(pl.multiple_of(grp * pack + half * SUBLANES, SUBLANES), SUBLANES)
                e1g, e2g, gg = e1_sc[rows, :], e2_sc[rows, :], g_sc[rows, :]
                for t in range(SUBLANES):
                    a = jnp.where(e1g[t:t + 1, :] == iot, 1.0, 0.0).astype(BF16)
                    b = jnp.where(e2g[t:t + 1, :] == iot, gg[t:t + 1, :], 0.0).astype(BF16)
                    stg_sc[half, pl.ds(t, PEER_KEYS, stride=SUBLANES), :] = _dot_nt(a, b)
            lo = stg_sc[0].reshape(PEER_KEYS, SUBLANES, LANES)
            hi = stg_sc[1].reshape(PEER_KEYS, SUBLANES, LANES)
            w_sc[:, pl.ds(pl.multiple_of(grp * pack, pack), pack), :] = jnp.concatenate([lo, hi], axis=1).astype(BF16)
            return c

        lax.fori_loop(0, tn // pack, body, 0)

    act = _dot(hn_ref[...], ut_ref[...])
    gl = 0.5 * act * (1.0 + lax.erf(act * INV_SQRT2))
    zs = []
    for c in range(PEER_EB // PEER_KEYS):
        wsl = w_sc[eb * (PEER_EB // PEER_KEYS) + c].astype(F32)
        zs.append((gl[:, LANES * c:LANES * (c + 1)] * wsl).astype(BF16))
    o_ref[...] += _dot(jnp.concatenate(zs, axis=1), v_ref[...])


def _peer(e1t, e2t, gt, hn, h, ut, v):
    n = hn.shape[0]
    tn = PEER_TN
    n_exp = ut.shape[1]
    jn = PEER_HEADS * PEER_TOPK
    sel = pl.BlockSpec((jn, tn), lambda i, e: (0, i))
    row = pl.BlockSpec((tn, D_MODEL), lambda i, e: (i, 0))
    wblk = pl.BlockSpec((PEER_EB, D_MODEL), lambda i, e: (e, 0))
    return pl.pallas_call(
        _peer_body, grid=(n // tn, n_exp // PEER_EB),
        in_specs=[sel, sel, sel, row, row, pl.BlockSpec((D_MODEL, PEER_EB), lambda i, e: (0, e)), wblk],
        out_specs=row, out_shape=jax.ShapeDtypeStruct((n, D_MODEL), F32),
        scratch_shapes=[pltpu.VMEM((tn, jn), F32), pltpu.VMEM((tn, jn), F32), pltpu.VMEM((tn, jn), F32),
                        pltpu.VMEM((2, PEER_KEYS * SUBLANES, LANES), F32),
                        pltpu.VMEM((PEER_KEYS, tn, LANES), BF16)],
        compiler_params=_cparams(("arbitrary", "arbitrary"), 56), name="peer",
    )(e1t, e2t, gt, hn, h, ut, v)


def _pad_last(a, n):
    return jnp.pad(a, [(0, 0)] * (a.ndim - 1) + [(0, n - a.shape[-1])])


def _layer_weights(attn_norm_w, w_in, q_a_norm_w, w_uq, q_norm_w, kv_a_norm_w, w_ukv, k_norm_w,
                   ret_norm_w, w_o, ffn_norm_w, w_peer_q, sub_keys1, sub_keys2, peer_u, peer_v):
    o = np.cumsum([0, Q_LORA, KV_LORA, QK_ROPE, 512, 512, 512, 512])
    zeros = lambda w: jnp.zeros((D_MODEL, w), F32)
    win = jnp.concatenate([w_in[:, o[0]:o[2]], zeros(QK_NOPE), w_in[:, o[2]:o[3]], zeros(LANES - QK_HEAD),
                           w_in[:, o[3]:o[7]]], axis=1)
    wuq = _pad_last(w_uq.reshape(Q_LORA, MLA_HEADS, QK_HEAD), HEAD_PAD).reshape(Q_LORA, MLA_HEADS * HEAD_PAD)
    ukv = w_ukv.reshape(KV_LORA, MLA_HEADS, QK_NOPE + V_HEAD)
    ukn, uv = ukv[:, :, :QK_NOPE], ukv[:, :, QK_NOPE:]
    knw = _pad_last(k_norm_w, LANES)[None]
    wknt = _pad_last(ukn, HEAD_PAD).transpose(1, 2, 0)
    uvp = uv.reshape(KV_LORA, 4, 2, V_HEAD)
    zv = jnp.zeros((KV_LORA, 4, V_HEAD), F32)
    wve = jnp.concatenate([uvp[:, :, 0], zv], -1).transpose(1, 0, 2)
    wvo = jnp.concatenate([zv, uvp[:, :, 1]], -1).transpose(1, 0, 2)
    return dict(
        anw=attn_norm_w[None], win=win.astype(BF16), qanw=q_a_norm_w[None], wuq=wuq.astype(BF16),
        qnw=(_pad_last(q_norm_w, LANES) * (QK_HEAD ** -0.5 * LOG2E))[None], kvnw=kv_a_norm_w[None],
        wukn=_pad_last(ukn, HEAD_PAD).reshape(KV_LORA, MLA_HEADS * HEAD_PAD).astype(BF16),
        wuv=uv.reshape(KV_LORA, MLA_HEADS * V_HEAD).astype(BF16), knw=knw,
        wpe=_pad_last(k_norm_w[QK_NOPE:], LANES)[None], wknt=wknt.astype(BF16),
        wkn_t=ukn.transpose(1, 2, 0).reshape(MLA_HEADS * QK_NOPE, KV_LORA).astype(BF16), wve=wve.astype(BF16), wvo=wvo.astype(BF16),
        rnw=ret_norm_w[None], wo1=w_o[:512].astype(BF16), wo2=w_o[512:].astype(BF16), fnw=ffn_norm_w[None],
        wpq=w_peer_q.astype(BF16), k1=sub_keys1.astype(BF16), k2=sub_keys2.astype(BF16),
        put=peer_u.T.astype(BF16), pv=peer_v.astype(BF16))


def _rope_tables(pos):
    posf = np.asarray(pos, np.float64)[:, None]
    n = posf.shape[0]

    def cos_sin(half):
        ang = posf * (ROPE_THETA ** (-np.arange(half, dtype=np.float64) / half))[None, :]
        return jnp.asarray(np.cos(ang), F32), jnp.asarray(np.sin(ang), F32)

    c, s = cos_sin(QK_ROPE // 2)
    one, zero = jnp.ones((n, QK_NOPE), F32), jnp.zeros((n, QK_NOPE), F32)
    cq = jnp.concatenate([one, c, c, one[:, :LANES - QK_HEAD]], 1)
    sq = jnp.concatenate([zero, -s, s, zero[:, :LANES - QK_HEAD]], 1)
    c2, s2 = cos_sin(RET_DK // 2)
    cr = jnp.concatenate([c2, c2, c2, c2], 1)
    sr = jnp.concatenate([-s2, s2, -s2, s2], 1)
    return dict(cq=cq, sq=sq, cr=cr, sr=sr, cos_t=jnp.concatenate([c, c], 1).T, sin_t=jnp.concatenate([s, s], 1).T)


def kernel(x_prompt, x_sample, cache_kv_latent, cache_k_rope, state_ret, page_table, attn_norm_w, w_in, q_a_norm_w, w_uq, q_norm_w, kv_a_norm_w, w_ukv, k_norm_w, ret_norm_w, w_o, ffn_norm_w, w_peer_q, peer_sub_keys1, peer_sub_keys2, peer_u, peer_v):
    batch, seq, _ = x_prompt.shape
    n_seq, dec_seq, _ = x_sample.shape
    depth = attn_norm_w.shape[0]
    past = page_table.shape[1] * PAGE
    n_p, n_s = batch * seq, n_seq * dec_seq
    nrow = MLA_HEADS * dec_seq

    pos = np.concatenate([np.tile(np.arange(seq), batch), np.tile(past + np.arange(dec_seq), n_seq)])
    tabs = _rope_tables(pos)
    past_tabs = _rope_tables(np.arange(past))
    new_tabs = _rope_tables(past + np.arange(PAGE))

    h_all = jnp.concatenate([x_prompt.reshape(n_p, D_MODEL), x_sample.reshape(n_s, D_MODEL)], 0)
    outs = [[] for _ in range(6)]
    for l in range(depth):
        wts = _layer_weights(attn_norm_w[l], w_in[l], q_a_norm_w[l], w_uq[l], q_norm_w[l], kv_a_norm_w[l], w_ukv[l],
                             k_norm_w[l], ret_norm_w[l], w_o[l], ffn_norm_w[l], w_peer_q[l], peer_sub_keys1[l],
                             peer_sub_keys2[l], peer_u[l], peer_v[l])
        q, k, vt, ckv, kpe_t, rq, rk, rv, rg = _mixer(h_all, wts, tabs)
        kpe_s = kpe_t[:, n_p:].reshape(QK_ROPE, n_seq, dec_seq)

        mla_p = _pattn(q, k, vt, batch, seq)
        reto_p, r_p = _pret(rq, rk, rv, batch, seq)
        r_p = jnp.stack([r_p[:, :, :RET_DK, :RET_DV], r_p[:, :, RET_DK:, RET_DV:]], 2).reshape(batch, RET_HEADS, RET_DK, RET_DV)

        qa, qcs = _sabs(q[n_p:], wts["knw"], wts["wpe"], wts["wknt"])
        cnew = jnp.pad(ckv[n_p:].reshape(n_seq, dec_seq, KV_LORA), ((0, 0), (0, PAGE - dec_seq), (0, 0)))
        pnew_t = jnp.pad(kpe_s.transpose(1, 0, 2), ((0, 0), (0, 0), (0, PAGE - dec_seq)))
        pc = _sattn(page_table, qa.reshape(n_seq, nrow, HEAD_PAD), qcs.reshape(n_seq, nrow, HEAD_PAD), wts["wkn_t"],
                    past_tabs["cos_t"], past_tabs["sin_t"], cnew, pnew_t, new_tabs["cos_t"], new_tabs["sin_t"],
                    cache_kv_latent[l], cache_k_rope[l].transpose(0, 2, 1), dec_seq)
        pc_h = pc.reshape(n_seq, dec_seq, MLA_HEADS, KV_LORA).transpose(2, 0, 1, 3).reshape(MLA_HEADS, n_s, KV_LORA)
        mla_s = _sup(pc_h, wts["wve"], wts["wvo"])
        seq_last = lambda a: a[n_p:].astype(F32).reshape(n_seq, dec_seq, 512).transpose(1, 2, 0)
        reto_s, r_s = _sret(seq_last(rq), seq_last(rk), seq_last(rv), state_ret[l].astype(F32).transpose(1, 2, 3, 0),
                            dec_seq)
        reto_s = reto_s.transpose(2, 0, 1).reshape(n_s, 512)
        r_s = r_s.transpose(3, 0, 1, 2)

        mla = jnp.concatenate([mla_p, mla_s], 0)
        reto = jnp.concatenate([reto_p, reto_s], 0)
        h_mid, hn, s1t, s2t = _post(h_all, mla, reto, rg, wts)
        e1t, e2t, gt = _topk(s1t, s2t)
        h_all = _peer(e1t, e2t, gt, hn, h_mid, wts["put"], wts["pv"])

        outs[0].append(ckv[:n_p].reshape(batch, seq, KV_LORA))
        outs[1].append(kpe_t[:, :n_p].reshape(QK_ROPE, batch, seq).transpose(1, 2, 0))
        outs[2].append(r_p)
        outs[3].append(ckv[n_p:].reshape(n_seq, dec_seq, KV_LORA))
        outs[4].append(kpe_s.transpose(1, 2, 0))
        outs[5].append(r_s)
    return (h_all[:n_p].reshape(batch, seq, D_MODEL), h_all[n_p:].reshape(n_seq, dec_seq, D_MODEL),
            *[jnp.stack(o) for o in outs])
```

```python
import functools

import numpy as np
import jax
import jax.numpy as jnp
from jax import lax
from jax.experimental import pallas as pl
from jax.experimental.pallas import tpu as pltpu

F32 = jnp.float32
BF16 = jnp.bfloat16

D_MODEL = 1024
PAGE = 128
MLA_HEADS = 8
QK_NOPE = 64
QK_ROPE = 32
QK_HEAD = QK_NOPE + QK_ROPE
V_HEAD = 64
Q_LORA = 256
KV_LORA = 128
RET_HEADS = 8
RET_DK = 64
RET_DV = 64
RET_CHUNK = 128
ROPE_THETA = 10000.0
PEER_KEYS = 128
PEER_HEADS = 8
PEER_QDIM = 256
PEER_TOPK = 16
EPS = 1e-6
LANES = 128
SUBLANES = 8
HEAD_PAD = 128
NEG_INF = float("-inf")
INV_SQRT2 = 0.7071067811865476
LOG2E = 1.4426950408889634

MIX_TM = 256
ATT_T = 512
ATT_CH = 32
SAMP_SUB = 2048
SAMP_CHUNK = 512
TOPK_TT = 256
TOPK_HEADS = 2
PEER_TN = 512
PEER_EB = 1024


def _cparams(sem, vmem_mb):
    return pltpu.CompilerParams(dimension_semantics=sem, vmem_limit_bytes=vmem_mb << 20)


def _full(shape):
    n = len(shape)
    return pl.BlockSpec(shape, lambda *_: (0,) * n)


def _rms(x, w, n):
    return x * lax.rsqrt(jnp.sum(x * x, -1, keepdims=True) * (1.0 / n) + EPS) * w


def _dot(a, b):
    return jnp.dot(a, b, preferred_element_type=F32)


def _dot_nt(a, b):
    return lax.dot_general(a, b, (((1,), (1,)), ((), ())), preferred_element_type=F32)


def _dot_tn(a, b):
    return lax.dot_general(a, b, (((0,), (0,)), ((), ())), preferred_element_type=F32)


def _mixer_body(x_ref, anw_ref, win_ref, qanw_ref, wuq_ref, qnw_ref, kvnw_ref, wukn_ref, wuv_ref, knw_ref,
                cq_ref, sq_ref, cr_ref, sr_ref,
                q_ref, k_ref, vt_ref, ckv_ref, kpe_ref, rq_ref, rk_ref, rv_ref, rg_ref):
    x = x_ref[...]
    xn = _rms(x, anw_ref[...], D_MODEL)
    proj = _dot(xn.astype(BF16), win_ref[...])
    lane = lax.broadcasted_iota(jnp.int32, (x.shape[0], LANES), 1)
    cq, sq, cr, sr = cq_ref[...], sq_ref[...], cr_ref[...], sr_ref[...]
    lo_q = lane < QK_NOPE + QK_ROPE // 2
    lo_r = (lane & (RET_DK - 1)) < RET_DK // 2

    def rope_q(v):
        return v * cq + jnp.where(lo_q, pltpu.roll(v, LANES - 16, 1), pltpu.roll(v, 16, 1)) * sq

    def rope_r(v):
        return v * cr + jnp.where(lo_r, pltpu.roll(v, LANES - 32, 1), pltpu.roll(v, 32, 1)) * sr

    q_lat = _rms(proj[:, 0:256], qanw_ref[...], Q_LORA)
    qf = _dot(q_lat.astype(BF16), wuq_ref[...])
    ckv = _rms(proj[:, 256:384], kvnw_ref[...], KV_LORA)
    ckv_ref[...] = ckv
    kslot = proj[:, 384:512]
    kpe_ref[...] = kslot.T[QK_NOPE:QK_HEAD, :]
    cb = ckv.astype(BF16)
    knf = _dot(cb, wukn_ref[...])
    vt_ref[...] = _dot(cb, wuv_ref[...]).T.astype(BF16)
    qnw, knw = qnw_ref[...], knw_ref[...]
    for h in range(MLA_HEADS):
        sl = slice(HEAD_PAD * h, HEAD_PAD * (h + 1))
        q_ref[:, sl] = rope_q(_rms(qf[:, sl], qnw, QK_HEAD)).astype(BF16)
        k_ref[:, sl] = rope_q(_rms(knf[:, sl] + kslot, knw, QK_HEAD)).astype(BF16)
    for p in range(4):
        sl = slice(LANES * p, LANES * (p + 1))
        rq_ref[:, sl] = rope_r(proj[:, 512 + LANES * p:512 + LANES * (p + 1)]).astype(BF16)
        rk_ref[:, sl] = (rope_r(proj[:, 1024 + LANES * p:1024 + LANES * (p + 1)]) * (RET_DK ** -0.5)).astype(BF16)
    rv_ref[...] = proj[:, 1536:2048].astype(BF16)
    rg_ref[...] = proj[:, 2048:2560]


def _mixer(x_all, wts, tabs, batch, seq):
    n = x_all.shape[0]
    tm = MIX_TM
    assert seq % tm == 0 and n % tm == 0
    nb = seq // tm
    row = lambda w: pl.BlockSpec((tm, w), lambda i: (i, 0))
    tab = pl.BlockSpec((tm, LANES), lambda i: (jnp.where(i < batch * nb, i % nb, i - (batch - 1) * nb), 0))
    ins = [row(D_MODEL), _full((1, D_MODEL)), _full((D_MODEL, 2560)), _full((1, Q_LORA)), _full((Q_LORA, 1024)),
           _full((1, LANES)), _full((1, KV_LORA)), _full((KV_LORA, 1024)), _full((KV_LORA, 512)), _full((1, LANES)),
           tab, tab, tab, tab]
    outs = [row(1024), row(1024), pl.BlockSpec((512, tm), lambda i: (0, i)), row(KV_LORA),
            pl.BlockSpec((QK_ROPE, tm), lambda i: (0, i)), row(512),
            row(512), row(512), row(512)]
    shapes = [jax.ShapeDtypeStruct((n, 1024), BF16), jax.ShapeDtypeStruct((n, 1024), BF16),
              jax.ShapeDtypeStruct((512, n), BF16), jax.ShapeDtypeStruct((n, KV_LORA), F32),
              jax.ShapeDtypeStruct((QK_ROPE, n), F32), jax.ShapeDtypeStruct((n, 512), BF16),
              jax.ShapeDtypeStruct((n, 512), BF16), jax.ShapeDtypeStruct((n, 512), BF16),
              jax.ShapeDtypeStruct((n, 512), F32)]
    return pl.pallas_call(
        _mixer_body, grid=(n // tm,), in_specs=ins, out_specs=outs, out_shape=shapes,
        compiler_params=_cparams(("arbitrary",), 48), name="mixer",
    )(x_all, wts["anw"], wts["win"], wts["qanw"], wts["wuq"], wts["qnw"], wts["kvnw"], wts["wukn"], wts["wuv"],
      wts["knw"], tabs["cq"], tabs["sq"], tabs["cr"], tabs["sr"])


def _pattn_body(q_ref, k_ref, vt_ref, o_ref, st0, st1, pt0, pt1, acc_sc):
    qi = pl.program_id(2)
    t = ATT_T
    ch = ATT_CH
    krow = lax.broadcasted_iota(jnp.int32, (ch, t), 0)
    qcol = lax.broadcasted_iota(jnp.int32, (ch, t), 1)
    acc_sc[...] = jnp.zeros(acc_sc.shape, F32)
    st_sc, pt_sc = (st0, st1), (pt0, pt1)

    def scores(j, slot):
        start = pl.multiple_of(j * t, t)
        for hh in range(2):
            lsl = slice(HEAD_PAD * hh, HEAD_PAD * (hh + 1))
            st_sc[hh][slot] = _dot_nt(k_ref[pl.ds(start, t), lsl], q_ref[:, lsl])

    def update(j, slot, carry, masked):
        start = pl.multiple_of(j * t, t)
        out = []
        for hh in range(2):
            m_prev, l_prev = carry[2 * hh], carry[2 * hh + 1]

            def chunk(r):
                blk = st_sc[hh][slot, ch * r:ch * (r + 1), :]
                return jnp.where(krow + ch * r <= qcol, blk, NEG_INF) if masked else blk

            mc = chunk(0)
            for r in range(1, t // ch):
                mc = jnp.maximum(mc, chunk(r))
            m_new = jnp.maximum(m_prev, jnp.max(mc, 0, keepdims=True))
            alpha = jnp.exp2(m_prev - m_new)
            ls = jnp.zeros((ch, t), F32)
            for r in range(t // ch):
                p = jnp.exp2(chunk(r) - m_new)
                ls = ls + p
                pt_sc[hh][ch * r:ch * (r + 1), :] = p.astype(BF16)
            acc_sc[hh] = alpha * acc_sc[hh] + _dot(vt_ref[:, pl.ds(start, t)], pt_sc[hh][...])
            out += [m_new, alpha * l_prev + jnp.sum(ls, 0, keepdims=True)]
        return tuple(out)

    def step(j, slot, carry):
        scores(j + 1, 1 - slot)
        return update(j, slot, carry, False)

    def pair(i, carry):
        return step(2 * i + 1, 1, step(2 * i, 0, carry))

    scores(0, 0)
    init = (jnp.full((1, t), NEG_INF, F32), jnp.zeros((1, t), F32)) * 2
    carry = lax.fori_loop(0, qi // 2, pair, init)
    _, l0, _, l1 = lax.cond(qi % 2 == 1,
                            lambda c: update(qi, 1, step(qi - 1, 0, c), True),
                            lambda c: update(qi, 0, c, True), carry)
    drow = lax.broadcasted_iota(jnp.int32, (LANES, t), 0)
    o_ref[...] = jnp.where(drow < V_HEAD, acc_sc[0] / l0, acc_sc[1] / l1).T.astype(o_ref.dtype)


def _pattn(q_all, k_all, vt_all, batch, seq):
    t = ATT_T
    nq = seq // t
    return pl.pallas_call(
        _pattn_body, grid=(batch, MLA_HEADS // 2, nq),
        in_specs=[pl.BlockSpec((t, 2 * HEAD_PAD), lambda b, hp, qi: (b * nq + qi, hp)),
                  pl.BlockSpec((seq, 2 * HEAD_PAD), lambda b, hp, qi: (b, hp)),
                  pl.BlockSpec((LANES, seq), lambda b, hp, qi: (hp, b))],
        out_specs=pl.BlockSpec((t, LANES), lambda b, hp, qi: (b * nq + qi, hp)),
        out_shape=jax.ShapeDtypeStruct((batch * seq, MLA_HEADS * V_HEAD), BF16),
        scratch_shapes=[pltpu.VMEM((2, t, t), F32), pltpu.VMEM((2, t, t), F32), pltpu.VMEM((t, t), BF16),
                        pltpu.VMEM((t, t), BF16), pltpu.VMEM((2, LANES, t), F32)],
        compiler_params=_cparams(("arbitrary", "arbitrary", "arbitrary"), 48), name="pattn",
    )(q_all, k_all, vt_all)


def _ret_log_decay():
    return jnp.log1p(-jnp.power(2.0, -5.0 - jnp.arange(RET_HEADS, dtype=F32)))


def _ret_tables(c):
    lg = _ret_log_decay()
    i = jnp.arange(c, dtype=F32)
    diff = i[:, None] - i[None, :]
    dmat = jnp.where(diff >= 0, jnp.exp(lg[:, None, None] * jnp.maximum(diff, 0.0)), 0.0)
    lane_head = lambda a: jnp.repeat(a.reshape(a.shape[0], 4, 2), RET_DV, axis=2).transpose(1, 0, 2)
    cross = lane_head(jnp.exp(lg[None, :] * (i[:, None] + 1.0)))
    kdec = lane_head(jnp.exp(lg[None, :] * (c - 1.0 - i[:, None])))
    gc = lane_head(jnp.exp(lg * c)[None, :])
    return dmat[0::2], dmat[1::2], cross, kdec, gc


def _pret_body(rq_ref, rk_ref, rv_ref, de_ref, do_ref, cross_ref, kdec_ref, gc_ref, o_ref, rout_ref, r_sc):
    c = pl.program_id(1)

    @pl.when(c == 0)
    def _():
        r_sc[...] = jnp.zeros(r_sc.shape, F32)

    n = rq_ref.shape[0]
    lane = lax.broadcasted_iota(jnp.int32, (n, LANES), 1)
    even = lane < RET_DK
    rr = lax.broadcasted_iota(jnp.int32, (LANES, LANES), 0)
    cc = lax.broadcasted_iota(jnp.int32, (LANES, LANES), 1)
    same_head = (rr < RET_DK) == (cc < RET_DV)
    for p in range(4):
        sl = slice(LANES * p, LANES * (p + 1))
        q, k, v = rq_ref[:, sl], rk_ref[:, sl], rv_ref[:, sl]
        zero = jnp.zeros_like(q)
        se = _dot_nt(jnp.where(even, q, zero), k)
        so = _dot_nt(jnp.where(even, zero, q), k)
        oe = _dot((se * de_ref[p]).astype(BF16), v)
        oo = _dot((so * do_ref[p]).astype(BF16), v)
        r = r_sc[p]
        cross = _dot(q, r.astype(BF16)) * cross_ref[p]
        o_ref[:, sl] = jnp.where(even, oe, oo) + cross
        kd = (k.astype(F32) * kdec_ref[p]).astype(BF16)
        upd = _dot_tn(kd, v)
        r_sc[p] = r * gc_ref[p] + jnp.where(same_head, upd, 0.0)

    @pl.when(c == pl.num_programs(1) - 1)
    def _():
        rout_ref[0] = r_sc[...]


def _pret(rq, rk, rv, batch, seq):
    c = RET_CHUNK
    nc = seq // c
    tabs = _ret_tables(c)
    blk = pl.BlockSpec((c, 512), lambda b, i: (b * nc + i, 0))
    return pl.pallas_call(
        _pret_body, grid=(batch, nc),
        in_specs=[blk, blk, blk, _full((4, c, c)), _full((4, c, c)), _full((4, c, LANES)), _full((4, c, LANES)),
                  _full((4, 1, LANES))],
        out_specs=[blk, pl.BlockSpec((1, 4, LANES, LANES), lambda b, i: (b, 0, 0, 0))],
        out_shape=[jax.ShapeDtypeStruct((batch * seq, 512), F32),
                   jax.ShapeDtypeStruct((batch, 4, LANES, LANES), F32)],
        scratch_shapes=[pltpu.VMEM((4, LANES, LANES), F32)],
        compiler_params=_cparams(("arbitrary", "arbitrary"), 32), name="pret",
    )(rq, rk, rv, *tabs)


def _sabs_body(q_ref, knw_ref, wpe_ref, wknt_ref, qa_ref, qcs_ref):
    q = q_ref[...].astype(F32)
    qa_ref[...] = _dot((q * knw_ref[...]).astype(BF16), wknt_ref[0]).astype(BF16)
    lane = lax.broadcasted_iota(jnp.int32, q.shape, 1)
    qpe = pltpu.roll(q, LANES - QK_NOPE, 1)
    half = QK_ROPE // 2
    swapped = jnp.where(lane < half, pltpu.roll(qpe, LANES - half, 1), -pltpu.roll(qpe, half, 1))
    keep = lane < QK_ROPE
    wpe = wpe_ref[...]
    qc = jnp.where(keep, qpe * wpe, 0.0)
    qs = jnp.where(keep, swapped * wpe, 0.0)
    qcs_ref[...] = (qc + pltpu.roll(qs, QK_ROPE, 1)).astype(BF16)


def _sabs(q_s, knw, wpe, wknt):
    n = q_s.shape[0]
    blk = pl.BlockSpec((n, HEAD_PAD), lambda h: (0, h))
    shp = jax.ShapeDtypeStruct((n, MLA_HEADS * HEAD_PAD), BF16)
    return pl.pallas_call(
        _sabs_body, grid=(MLA_HEADS,),
        in_specs=[blk, _full((1, LANES)), _full((1, LANES)), pl.BlockSpec((1, LANES, KV_LORA), lambda h: (h, 0, 0))],
        out_specs=[blk, blk], out_shape=[shp, shp],
        compiler_params=_cparams(("arbitrary",), 32), name="sabs",
    )(q_s, knw, wpe, wknt)


def _sattn_body(pt_ref, qa_ref, qcs_ref, wkn_ref, cost_ref, sint_ref, cnew_ref, pnew_ref, cosn_ref, sinn_ref,
                lat_hbm, pe_hbm, pc_ref, latbuf, pebuf, cbuf, lhs_sc, xcs, m_sc, l_sc, acc_sc, sem, *, n_pages, dec_seq):
    s_id = pl.program_id(0)
    slot = s_id & 1
    nrow = qa_ref.shape[1]
    nkn = wkn_ref.shape[0]

    def page_copies(page_id, slt, i):
        cols = pl.ds(PAGE * i, PAGE)
        return (pltpu.make_async_copy(lat_hbm.at[page_id], latbuf.at[slt, cols, :], sem.at[slt, 0]),
                pltpu.make_async_copy(pe_hbm.at[page_id], pebuf.at[slt, :, cols], sem.at[slt, 1]))

    def start_sequence(seq, slt):
        for i in range(n_pages):
            for cp in page_copies(pt_ref[seq, i], slt, i):
                cp.start()

    @pl.when(s_id == 0)
    def _():
        xcs[...] = jnp.zeros(xcs.shape, BF16)
        lhs_sc[0:nkn, :] = wkn_ref[...]
        start_sequence(0, 0)

    @pl.when(s_id + 1 < pl.num_programs(0))
    def _():
        start_sequence(s_id + 1, 1 - slot)

    lhs_sc[nkn:nkn + nrow, :] = qa_ref[0]
    m_sc[...] = jnp.full(m_sc.shape, NEG_INF, F32)
    l_sc[...] = jnp.zeros(l_sc.shape, F32)
    acc_sc[...] = jnp.zeros(acc_sc.shape, F32)

    row8 = lax.broadcasted_iota(jnp.int32, (MLA_HEADS, 1), 0)

    def attend(nk, pe_t, cos_t, sin_t, mask):
        xcs[0:QK_ROPE, 0:nk] = (pe_t * cos_t).astype(BF16)
        xcs[QK_ROPE:2 * QK_ROPE, 0:nk] = (pe_t * sin_t).astype(BF16)
        pe2 = jnp.sum(pe_t * pe_t, 0, keepdims=True)
        lhs = lhs_sc[...]
        qcs = qcs_ref[0]
        parts = []
        ck = min(nk, SAMP_CHUNK)
        for c0 in range(0, nk, ck):
            c = cbuf[c0:c0 + ck, :]
            big = _dot_nt(lhs, c)
            ss = pe2[:, c0:c0 + ck] + jnp.zeros((MLA_HEADS, ck), F32)
            for h in range(MLA_HEADS):
                kh = big[QK_NOPE * h:QK_NOPE * (h + 1), :]
                ss = ss + jnp.where(row8 == h, jnp.sum(kh * kh, 0, keepdims=True), 0.0)
            rn = lax.rsqrt(ss * (1.0 / QK_HEAD) + EPS)
            sc = big[nkn:nkn + nrow, :] + _dot(qcs, xcs[:, c0:c0 + ck])
            parts.append((sc.reshape(dec_seq, MLA_HEADS, ck) * rn[None]).reshape(nrow, ck))
        s = parts[0] if len(parts) == 1 else jnp.concatenate(parts, axis=1)
        if mask is not None:
            s = jnp.where(mask, s, NEG_INF)
        m_prev = m_sc[...]
        m_new = jnp.maximum(m_prev, jnp.max(s, -1, keepdims=True))
        alpha = jnp.exp2(m_prev - m_new)
        p = jnp.exp2(s - m_new)
        l_sc[...] = alpha * l_sc[...] + jnp.sum(p, -1, keepdims=True)
        acc_sc[...] = alpha * acc_sc[...] + _dot(p.astype(BF16), cbuf[0:nk, :])
        m_sc[...] = m_new

    for i in range(n_pages):
        for cp in page_copies(0, slot, i):
            cp.wait()
    sub = SAMP_SUB
    for k0 in range(0, PAGE * n_pages, sub):
        cbuf[0:sub, :] = latbuf[slot, k0:k0 + sub, :].astype(BF16)
        attend(sub, pebuf[slot, :, k0:k0 + sub], cost_ref[:, k0:k0 + sub], sint_ref[:, k0:k0 + sub], None)
    cbuf[0:PAGE, :] = cnew_ref[0].astype(BF16)
    qtok = lax.shift_right_logical(lax.broadcasted_iota(jnp.int32, (nrow, PAGE), 0), 3)
    kcol = lax.broadcasted_iota(jnp.int32, (nrow, PAGE), 1)
    attend(PAGE, pnew_ref[0], cosn_ref[...], sinn_ref[...], jnp.logical_and(kcol <= qtok, kcol < dec_seq))
    pc_ref[0] = acc_sc[...] / l_sc[...]


def _sattn(page_table, qa, qcs, wkn_t, cos_t, sin_t, cnew, pnew_t, cosn_t, sinn_t, cache_lat, cache_pe_t, dec_seq):
    n_seq, nrow, _ = qa.shape
    n_pages = page_table.shape[1]
    past = n_pages * PAGE
    nkn = wkn_t.shape[0]
    assert MLA_HEADS == SUBLANES and nrow == dec_seq * MLA_HEADS and past % SAMP_SUB == 0
    qblk = pl.BlockSpec((1, nrow, LANES), lambda s, pt: (s, 0, 0))
    const2 = lambda shape: pl.BlockSpec(shape, lambda s, pt: (0, 0))
    hbm = pl.BlockSpec(memory_space=pl.ANY)
    in_specs = [qblk, qblk, const2((nkn, KV_LORA)), const2((QK_ROPE, past)), const2((QK_ROPE, past)),
                pl.BlockSpec((1, PAGE, KV_LORA), lambda s, pt: (s, 0, 0)),
                pl.BlockSpec((1, QK_ROPE, PAGE), lambda s, pt: (s, 0, 0)),
                const2((QK_ROPE, PAGE)), const2((QK_ROPE, PAGE)), hbm, hbm]
    gs = pltpu.PrefetchScalarGridSpec(
        num_scalar_prefetch=1, grid=(n_seq,), in_specs=in_specs,
        out_specs=pl.BlockSpec((1, nrow, LANES), lambda s, pt: (s, 0, 0)),
        scratch_shapes=[pltpu.VMEM((2, past, KV_LORA), F32), pltpu.VMEM((2, QK_ROPE, past), F32),
                        pltpu.VMEM((SAMP_SUB, LANES), BF16), pltpu.VMEM((nkn + nrow, KV_LORA), BF16),
                        pltpu.VMEM((LANES, SAMP_SUB), BF16), pltpu.VMEM((nrow, 1), F32), pltpu.VMEM((nrow, 1), F32),
                        pltpu.VMEM((nrow, LANES), F32), pltpu.SemaphoreType.DMA((2, 2))])
    return pl.pallas_call(
        functools.partial(_sattn_body, n_pages=n_pages, dec_seq=dec_seq), grid_spec=gs,
        out_shape=jax.ShapeDtypeStruct((n_seq, nrow, LANES), F32),
        compiler_params=_cparams(("arbitrary",), 48), name="sattn",
    )(page_table, qa, qcs, wkn_t, cos_t, sin_t, cnew, pnew_t, cosn_t, sinn_t, cache_lat, cache_pe_t)


def _sup_body(pc_ref, we_ref, wo_ref, o_ref):
    o_ref[...] = (_dot(pc_ref[0].astype(BF16), we_ref[0]) + _dot(pc_ref[1].astype(BF16), wo_ref[0])).astype(o_ref.dtype)


def _sup(pc_h, wve, wvo):
    n = pc_h.shape[1]
    return pl.pallas_call(
        _sup_body, grid=(MLA_HEADS // 2,),
        in_specs=[pl.BlockSpec((2, n, KV_LORA), lambda p: (p, 0, 0)),
                  pl.BlockSpec((1, KV_LORA, LANES), lambda p: (p, 0, 0)),
                  pl.BlockSpec((1, KV_LORA, LANES), lambda p: (p, 0, 0))],
        out_specs=pl.BlockSpec((n, LANES), lambda p: (0, p)),
        out_shape=jax.ShapeDtypeStruct((n, MLA_HEADS * V_HEAD), BF16),
        compiler_params=_cparams(("arbitrary",), 32), name="sup",
    )(pc_h, wve, wvo)


def _sret_body(q_ref, k_ref, v_ref, r0_ref, d_ref, cross_ref, kdec_ref, g_ref, o_ref, r_ref):
    nt = q_ref.shape[0]
    outs = []
    for i in range(nt):
        acc = jnp.zeros(v_ref.shape[1:], F32)
        for j in range(i + 1):
            sij = jnp.sum(q_ref[i] * k_ref[j], 0, keepdims=True) * d_ref[0, i * nt + j:i * nt + j + 1, :]
            acc = acc + sij * v_ref[j]
        outs.append(acc)

    def body(d, carry):
        r = r0_ref[0, d]
        row = pl.ds(d, 1)
        rn = g_ref[0] * r
        new = []
        for i in range(nt):
            new.append(carry[i] + (q_ref[i, row, :] * cross_ref[0, i:i + 1, :]) * r)
            rn = rn + (k_ref[i, row, :] * kdec_ref[0, i:i + 1, :]) * v_ref[i]
        r_ref[0, d] = rn
        return tuple(new)

    outs = lax.fori_loop(0, RET_DK, body, tuple(outs))
    for i in range(nt):
        o_ref[i] = outs[i]


def _sret(q_t, k_t, v_t, r0_t, dec_seq):
    n_seq = q_t.shape[2]
    lg = _ret_log_decay()
    i = jnp.arange(dec_seq, dtype=F32)
    lanes = jnp.ones((1, 1, n_seq), F32)
    diff = i[:, None] - i[None, :]
    dmat = jnp.where(diff >= 0, jnp.exp(lg[:, None, None] * jnp.maximum(diff, 0.0)), 0.0)
    dtab = dmat.reshape(RET_HEADS, dec_seq * dec_seq, 1) * lanes
    cross = jnp.exp(lg[:, None] * (i[None, :] + 1.0))[:, :, None] * lanes
    kdec = jnp.exp(lg[:, None] * (dec_seq - 1.0 - i[None, :]))[:, :, None] * lanes
    gdec = jnp.exp(lg * dec_seq)[:, None, None] * lanes
    blk = pl.BlockSpec((dec_seq, RET_DK, n_seq), lambda h: (0, h, 0))
    rblk = pl.BlockSpec((1, RET_DK, RET_DV, n_seq), lambda h: (h, 0, 0, 0))
    tab = lambda r: pl.BlockSpec((1, r, n_seq), lambda h: (h, 0, 0))
    return pl.pallas_call(
        _sret_body, grid=(RET_HEADS,),
        in_specs=[blk, blk, blk, rblk, tab(dec_seq * dec_seq), tab(dec_seq), tab(dec_seq), tab(1)],
        out_specs=[blk, rblk],
        out_shape=[jax.ShapeDtypeStruct((dec_seq, RET_HEADS * RET_DV, n_seq), F32),
                   jax.ShapeDtypeStruct((RET_HEADS, RET_DK, RET_DV, n_seq), F32)],
        compiler_params=_cparams(("arbitrary",), 32), name="sret",
    )(q_t, k_t, v_t, r0_t, dtab, cross, kdec, gdec)


def _post_body(x_ref, mla_ref, reto_ref, rg_ref, rnw_ref, wo1_ref, wo2_ref, fnw_ref, wpq_ref, k1_ref, k2_ref,
               h_ref, hn_ref, s1_ref, s2_ref):
    tm = x_ref.shape[0]
    lane = lax.broadcasted_iota(jnp.int32, (tm, LANES), 1)
    lo = lane < RET_DV

    def group(a):
        s_lo = jnp.sum(jnp.where(lo, a, 0.0), -1, keepdims=True)
        s_hi = jnp.sum(jnp.where(lo, 0.0, a), -1, keepdims=True)
        return jnp.where(lo, s_lo, s_hi)

    parts = []
    for p in range(4):
        sl = slice(LANES * p, LANES * (p + 1))
        o = reto_ref[:, sl]
        d = o - group(o) * (1.0 / RET_DV)
        y = d * lax.rsqrt(group(d * d) * (1.0 / RET_DV) + EPS) * rnw_ref[:, sl]
        rg = rg_ref[:, sl]
        parts.append((rg * jax.nn.sigmoid(rg) * y).astype(BF16))
    ret_out = jnp.concatenate(parts, axis=1)
    h = x_ref[...] + _dot(mla_ref[...], wo1_ref[...]) + _dot(ret_out, wo2_ref[...])
    h_ref[...] = h
    hn = _rms(h, fnw_ref[...], D_MODEL).astype(BF16)
    hn_ref[...] = hn
    pq = _dot(hn, wpq_ref[...])
    half = PEER_QDIM // 2
    for hd in range(PEER_HEADS):
        q1 = pq[:, PEER_QDIM * hd:PEER_QDIM * hd + half].astype(BF16)
        q2 = pq[:, PEER_QDIM * hd + half:PEER_QDIM * (hd + 1)].astype(BF16)
        s1_ref[hd] = _dot_nt(k1_ref[...], q1)
        s2_ref[hd] = _dot_nt(k2_ref[...], q2)


def _post(x_all, mla, reto, rg, wts):
    n = x_all.shape[0]
    tm = MIX_TM
    row = lambda w: pl.BlockSpec((tm, w), lambda i: (i, 0))
    sblk = pl.BlockSpec((PEER_HEADS, PEER_KEYS, tm), lambda i: (0, 0, i))
    half = PEER_QDIM // 2
    return pl.pallas_call(
        _post_body, grid=(n // tm,),
        in_specs=[row(D_MODEL), row(512), row(512), row(512), _full((1, 512)), _full((512, D_MODEL)),
                  _full((512, D_MODEL)), _full((1, D_MODEL)), _full((D_MODEL, PEER_HEADS * PEER_QDIM)),
                  _full((PEER_KEYS, half)), _full((PEER_KEYS, half))],
        out_specs=[row(D_MODEL), row(D_MODEL), sblk, sblk],
        out_shape=[jax.ShapeDtypeStruct((n, D_MODEL), F32), jax.ShapeDtypeStruct((n, D_MODEL), BF16),
                   jax.ShapeDtypeStruct((PEER_HEADS, PEER_KEYS, n), F32),
                   jax.ShapeDtypeStruct((PEER_HEADS, PEER_KEYS, n), F32)],
        compiler_params=_cparams(("arbitrary",), 48), name="post",
    )(x_all, mla, reto, rg, wts["rnw"], wts["wo1"], wts["wo2"], wts["fnw"], wts["wpq"], wts["k1"], wts["k2"])


def _topk_body(s1_ref, s2_ref, e1_ref, e2_ref, g_ref):
    kk = PEER_TOPK
    tt = s1_ref.shape[2]
    row16 = lax.broadcasted_iota(jnp.int32, (kk, tt), 0)

    def take_top(x):
        nrow = x.shape[0]
        row = lax.broadcasted_iota(jnp.int32, x.shape, 0)
        vals = jnp.zeros((kk, tt), F32)
        idxs = jnp.zeros((kk, tt), jnp.int32)
        for r in range(kk):
            m = jnp.max(x, axis=0, keepdims=True)
            idx = jnp.min(jnp.where(x == m, row, nrow), axis=0, keepdims=True)
            vals = jnp.where(row16 == r, m, vals)
            idxs = jnp.where(row16 == r, idx, idxs)
            x = jnp.where(row == idx, NEG_INF, x)
        return vals, idxs

    def pick(table, sel):
        return jnp.sum(jnp.where(row16 == sel, table, 0), axis=0, keepdims=True)

    for hh in range(TOPK_HEADS):
        v1, i1 = take_top(s1_ref[hh])
        v2, i2 = take_top(s2_ref[hh])
        cur = v1 + v2[0:1, :]
        ptr = jnp.zeros((kk, tt), jnp.int32)
        vs = jnp.zeros((kk, tt), F32)
        e1 = jnp.zeros((kk, tt), jnp.int32)
        e2 = jnp.zeros((kk, tt), jnp.int32)
        for r in range(kk):
            m = jnp.max(cur, axis=0, keepdims=True)
            a_sel = row16 == jnp.min(jnp.where(cur == m, row16, kk), axis=0, keepdims=True)
            b_cur = jnp.sum(jnp.where(a_sel, ptr, 0), axis=0, keepdims=True)
            vs = jnp.where(row16 == r, m, vs)
            e1 = jnp.where(row16 == r, jnp.sum(jnp.where(a_sel, i1, 0), axis=0, keepdims=True), e1)
            e2 = jnp.where(row16 == r, pick(i2, b_cur), e2)
            b_next = b_cur + 1
            head = jnp.sum(jnp.where(a_sel, v1, 0.0), axis=0, keepdims=True) + pick(v2, b_next)
            cur = jnp.where(a_sel, jnp.where(b_next < kk, head, NEG_INF), cur)
            ptr = jnp.where(a_sel, b_next, ptr)
        p = jnp.exp(vs - jnp.max(vs, axis=0, keepdims=True))
        g_ref[kk * hh:kk * (hh + 1), :] = p / jnp.sum(p, axis=0, keepdims=True)
        e1_ref[kk * hh:kk * (hh + 1), :] = e1.astype(F32)
        e2_ref[kk * hh:kk * (hh + 1), :] = e2.astype(F32)


def _topk(s1t, s2t):
    n = s1t.shape[2]
    tt = TOPK_TT
    sblk = pl.BlockSpec((TOPK_HEADS, PEER_KEYS, tt), lambda i, h: (h, 0, i))
    oblk = pl.BlockSpec((TOPK_HEADS * PEER_TOPK, tt), lambda i, h: (h, i))
    shp = jax.ShapeDtypeStruct((PEER_HEADS * PEER_TOPK, n), F32)
    return pl.pallas_call(
        _topk_body, grid=(n // tt, PEER_HEADS // TOPK_HEADS), in_specs=[sblk, sblk], out_specs=[oblk, oblk, oblk],
        out_shape=[shp, shp, shp],
        compiler_params=_cparams(("arbitrary", "arbitrary"), 32), name="topk",
    )(s1t, s2t)


def _peer_body(e1_ref, e2_ref, g_ref, hn_ref, h_ref, ut_ref, v_ref, o_ref, e1_sc, e2_sc, g_sc, stg_sc, w_sc):
    eb = pl.program_id(1)
    tn = hn_ref.shape[0]
    pack = 2 * SUBLANES

    @pl.when(eb == 0)
    def _():
        e1_sc[...] = e1_ref[...].T
        e2_sc[...] = e2_ref[...].T
        g_sc[...] = g_ref[...].T
        o_ref[...] = h_ref[...]
        iot = lax.broadcasted_iota(jnp.int32, (PEER_KEYS, LANES), 0).astype(F32)

        def body(grp, c):
            for half in range(2):
                rows = pl.ds(pl.multiple_of(grp * pack + half * SUBLANES, SUBLANES), SUBLANES)
                e1g, e2g, gg = e1_sc[rows, :], e2_sc[rows, :], g_sc[rows, :]
                for t in range(SUBLANES):
                    a = jnp.where(e1g[t:t + 1, :] == iot, 1.0, 0.0).astype(BF16)
                    b = jnp.where(e2g[t:t + 1, :] == iot, gg[t:t + 1, :], 0.0).astype(BF16)
                    stg_sc[half, pl.ds(t, PEER_KEYS, stride=SUBLANES), :] = _dot_nt(a, b)
            lo = stg_sc[0].reshape(PEER_KEYS, SUBLANES, LANES)
            hi = stg_sc[1].reshape(PEER_KEYS, SUBLANES, LANES)
            w_sc[:, pl.ds(pl.multiple_of(grp * pack, pack), pack), :] = jnp.concatenate([lo, hi], axis=1).astype(BF16)
            return c

        lax.fori_loop(0, tn // pack, body, 0)

    act = _dot(hn_ref[...], ut_ref[...])
    gl = 0.5 * act * (1.0 + lax.erf(act * INV_SQRT2))
    zs = []
    for c in range(PEER_EB // PEER_KEYS):
        wsl = w_sc[eb * (PEER_EB // PEER_KEYS) + c].astype(F32)
        zs.append((gl[:, LANES * c:LANES * (c + 1)] * wsl).astype(BF16))
    o_ref[...] += _dot(jnp.concatenate(zs, axis=1), v_ref[...])


def _peer(e1t, e2t, gt, hn, h, ut, v):
    n = hn.shape[0]
    tn = PEER_TN
    n_exp = ut.shape[1]
    jn = PEER_HEADS * PEER_TOPK
    sel = pl.BlockSpec((jn, tn), lambda i, e: (0, i))
    row = pl.BlockSpec((tn, D_MODEL), lambda i, e: (i, 0))
    wblk = pl.BlockSpec((PEER_EB, D_MODEL), lambda i, e: (e, 0))
    return pl.pallas_call(
        _peer_body, grid=(n // tn, n_exp // PEER_EB),
        in_specs=[sel, sel, sel, row, row, pl.BlockSpec((D_MODEL, PEER_EB), lambda i, e: (0, e)), wblk],
        out_specs=row, out_shape=jax.ShapeDtypeStruct((n, D_MODEL), F32),
        scratch_shapes=[pltpu.VMEM((tn, jn), F32), pltpu.VMEM((tn, jn), F32), pltpu.VMEM((tn, jn), F32),
                        pltpu.VMEM((2, PEER_KEYS * SUBLANES, LANES), F32),
                        pltpu.VMEM((PEER_KEYS, tn, LANES), BF16)],
        compiler_params=_cparams(("arbitrary", "arbitrary"), 56), name="peer",
    )(e1t, e2t, gt, hn, h, ut, v)


def _pad_last(a, n):
    return jnp.pad(a, [(0, 0)] * (a.ndim - 1) + [(0, n - a.shape[-1])])


def _layer_weights(attn_norm_w, w_in, q_a_norm_w, w_uq, q_norm_w, kv_a_norm_w, w_ukv, k_norm_w,
                   ret_norm_w, w_o, ffn_norm_w, w_peer_q, sub_keys1, sub_keys2, peer_u, peer_v):
    o = np.cumsum([0, Q_LORA, KV_LORA, QK_ROPE, 512, 512, 512, 512])
    zeros = lambda w: jnp.zeros((D_MODEL, w), F32)
    win = jnp.concatenate([w_in[:, o[0]:o[2]], zeros(QK_NOPE), w_in[:, o[2]:o[3]], zeros(LANES - QK_HEAD),
                           w_in[:, o[3]:o[7]]], axis=1)
    wuq = _pad_last(w_uq.reshape(Q_LORA, MLA_HEADS, QK_HEAD), HEAD_PAD).reshape(Q_LORA, MLA_HEADS * HEAD_PAD)
    ukv = w_ukv.reshape(KV_LORA, MLA_HEADS, QK_NOPE + V_HEAD)
    ukn, uv = ukv[:, :, :QK_NOPE], ukv[:, :, QK_NOPE:]
    knw = _pad_last(k_norm_w, LANES)[None]
    wknt = _pad_last(ukn, HEAD_PAD).transpose(1, 2, 0)
    uvp = uv.reshape(KV_LORA, 4, 2, V_HEAD)
    zv = jnp.zeros((KV_LORA, 4, V_HEAD), F32)
    wve = jnp.concatenate([uvp[:, :, 0], zv], -1).transpose(1, 0, 2)
    wvo = jnp.concatenate([zv, uvp[:, :, 1]], -1).transpose(1, 0, 2)
    return dict(
        anw=attn_norm_w[None], win=win.astype(BF16), qanw=q_a_norm_w[None], wuq=wuq.astype(BF16),
        qnw=(_pad_last(q_norm_w, LANES) * (QK_HEAD ** -0.5 * LOG2E))[None], kvnw=kv_a_norm_w[None],
        wukn=_pad_last(ukn, HEAD_PAD).reshape(KV_LORA, MLA_HEADS * HEAD_PAD).astype(BF16),
        wuv=uv.reshape(KV_LORA, MLA_HEADS * V_HEAD).astype(BF16), knw=knw,
        wpe=_pad_last(k_norm_w[QK_NOPE:], LANES)[None], wknt=wknt.astype(BF16),
        wkn_t=ukn.transpose(1, 2, 0).reshape(MLA_HEADS * QK_NOPE, KV_LORA).astype(BF16), wve=wve.astype(BF16), wvo=wvo.astype(BF16),
        rnw=ret_norm_w[None], wo1=w_o[:512].astype(BF16), wo2=w_o[512:].astype(BF16), fnw=ffn_norm_w[None],
        wpq=w_peer_q.astype(BF16), k1=sub_keys1.astype(BF16), k2=sub_keys2.astype(BF16),
        put=peer_u.T.astype(BF16), pv=peer_v.astype(BF16))


def _rope_tables(pos):
    posf = np.asarray(pos, np.float64)[:, None]
    n = posf.shape[0]

    def cos_sin(half):
        ang = posf * (ROPE_THETA ** (-np.arange(half, dtype=np.float64) / half))[None, :]
        return np.cos(ang).astype(np.float32), np.sin(ang).astype(np.float32)

    c, s = cos_sin(QK_ROPE // 2)
    one, zero = np.ones((n, QK_NOPE), np.float32), np.zeros((n, QK_NOPE), np.float32)
    cq = np.concatenate([one, c, c, one[:, :LANES - QK_HEAD]], 1)
    sq = np.concatenate([zero, -s, s, zero[:, :LANES - QK_HEAD]], 1)
    c2, s2 = cos_sin(RET_DK // 2)
    cr = np.concatenate([c2, c2, c2, c2], 1)
    sr = np.concatenate([-s2, s2, -s2, s2], 1)
    tabs = dict(cq=cq, sq=sq, cr=cr, sr=sr, cos_t=np.concatenate([c, c], 1).T, sin_t=np.concatenate([s, s], 1).T)
    return {k: jnp.asarray(np.ascontiguousarray(v)) for k, v in tabs.items()}


def kernel(x_prompt, x_sample, cache_kv_latent, cache_k_rope, state_ret, page_table, attn_norm_w, w_in, q_a_norm_w, w_uq, q_norm_w, kv_a_norm_w, w_ukv, k_norm_w, ret_norm_w, w_o, ffn_norm_w, w_peer_q, peer_sub_keys1, peer_sub_keys2, peer_u, peer_v):
    batch, seq, _ = x_prompt.shape
    n_seq, dec_seq, _ = x_sample.shape
    depth = attn_norm_w.shape[0]
    past = page_table.shape[1] * PAGE
    n_p, n_s = batch * seq, n_seq * dec_seq
    nrow = MLA_HEADS * dec_seq

    tabs = _rope_tables(np.concatenate([np.arange(seq), np.tile(past + np.arange(dec_seq), n_seq)]))
    past_tabs = _rope_tables(np.arange(past))
    new_tabs = _rope_tables(past + np.arange(PAGE))

    h_all = jnp.concatenate([x_prompt.reshape(n_p, D_MODEL), x_sample.reshape(n_s, D_MODEL)], 0)
    outs = [[] for _ in range(6)]
    for l in range(depth):
        wts = _layer_weights(attn_norm_w[l], w_in[l], q_a_norm_w[l], w_uq[l], q_norm_w[l], kv_a_norm_w[l], w_ukv[l],
                             k_norm_w[l], ret_norm_w[l], w_o[l], ffn_norm_w[l], w_peer_q[l], peer_sub_keys1[l],
                             peer_sub_keys2[l], peer_u[l], peer_v[l])
        q, k, vt, ckv, kpe_t, rq, rk, rv, rg = _mixer(h_all, wts, tabs, batch, seq)
        kpe_s = kpe_t[:, n_p:].reshape(QK_ROPE, n_seq, dec_seq)

        mla_p = _pattn(q, k, vt, batch, seq)
        reto_p, r_p = _pret(rq, rk, rv, batch, seq)
        r_p = jnp.stack([r_p[:, :, :RET_DK, :RET_DV], r_p[:, :, RET_DK:, RET_DV:]], 2).reshape(batch, RET_HEADS, RET_DK, RET_DV)

        qa, qcs = _sabs(q[n_p:], wts["knw"], wts["wpe"], wts["wknt"])
        cnew = jnp.pad(ckv[n_p:].reshape(n_seq, dec_seq, KV_LORA), ((0, 0), (0, PAGE - dec_seq), (0, 0)))
        pnew_t = jnp.pad(kpe_s.transpose(1, 0, 2), ((0, 0), (0, 0), (0, PAGE - dec_seq)))
        pc = _sattn(page_table, qa.reshape(n_seq, nrow, HEAD_PAD), qcs.reshape(n_seq, nrow, HEAD_PAD), wts["wkn_t"],
                    past_tabs["cos_t"], past_tabs["sin_t"], cnew, pnew_t, new_tabs["cos_t"], new_tabs["sin_t"],
                    cache_kv_latent[l], cache_k_rope[l].transpose(0, 2, 1), dec_seq)
        pc_h = pc.reshape(n_seq, dec_seq, MLA_HEADS, KV_LORA).transpose(2, 0, 1, 3).reshape(MLA_HEADS, n_s, KV_LORA)
        mla_s = _sup(pc_h, wts["wve"], wts["wvo"])
        seq_last = lambda a: a[n_p:].astype(F32).reshape(n_seq, dec_seq, 512).transpose(1, 2, 0)
        reto_s, r_s = _sret(seq_last(rq), seq_last(rk), seq_last(rv), state_ret[l].astype(F32).transpose(1, 2, 3, 0),
                            dec_seq)
        reto_s = reto_s.transpose(2, 0, 1).reshape(n_s, 512)
        r_s = r_s.transpose(3, 0, 1, 2)

        mla = jnp.concatenate([mla_p, mla_s], 0)
        reto = jnp.concatenate([reto_p, reto_s], 0)
        h_mid, hn, s1t, s2t = _post(h_all, mla, reto, rg, wts)
        e1t, e2t, gt = _topk(s1t, s2t)
        h_all = _peer(e1t, e2t, gt, hn, h_mid, wts["put"], wts["pv"])

        outs[0].append(ckv[:n_p].reshape(batch, seq, KV_LORA))
        outs[1].append(kpe_t[:, :n_p].reshape(QK_ROPE, batch, seq).transpose(1, 2, 0))
        outs[2].append(r_p)
        outs[3].append(ckv[n_p:].reshape(n_seq, dec_seq, KV_LORA))
        outs[4].append(kpe_s.transpose(1, 2, 0))
        outs[5].append(r_s)
    return (h_all[:n_p].reshape(batch, seq, D_MODEL), h_all[n_p:].reshape(n_seq, dec_seq, D_MODEL),
            *[jnp.stack(o) for o in outs])
```

```python
import functools

import numpy as np
import jax
import jax.numpy as jnp
from jax import lax
from jax.experimental import pallas as pl
from jax.experimental.pallas import tpu as pltpu

F32 = jnp.float32
BF16 = jnp.bfloat16

D_MODEL = 1024
PAGE = 128
MLA_HEADS = 8
QK_NOPE = 64
QK_ROPE = 32
QK_HEAD = QK_NOPE + QK_ROPE
V_HEAD = 64
Q_LORA = 256
KV_LORA = 128
RET_HEADS = 8
RET_DK = 64
RET_DV = 64
RET_CHUNK = 128
ROPE_THETA = 10000.0
PEER_KEYS = 128
PEER_HEADS = 8
PEER_QDIM = 256
PEER_TOPK = 16
EPS = 1e-6
LANES = 128
SUBLANES = 8
HEAD_PAD = 128
NEG_INF = float("-inf")
INV_SQRT2 = 0.7071067811865476
LOG2E = 1.4426950408889634

MIX_TM = 256
ATT_T = 512
ATT_CH = 32
SAMP_SUB = 2048
SAMP_CHUNK = 512
TOPK_TT = 256
TOPK_HEADS = 2
PEER_TN = 512
PEER_EB = 1024


def _cparams(sem, vmem_mb):
    return pltpu.CompilerParams(dimension_semantics=sem, vmem_limit_bytes=vmem_mb << 20)


def _full(shape):
    n = len(shape)
    return pl.BlockSpec(shape, lambda *_: (0,) * n)


def _rms(x, w, n):
    return x * lax.rsqrt(jnp.sum(x * x, -1, keepdims=True) * (1.0 / n) + EPS) * w


def _dot(a, b):
    return jnp.dot(a, b, preferred_element_type=F32)


def _dot_nt(a, b):
    return lax.dot_general(a, b, (((1,), (1,)), ((), ())), preferred_element_type=F32)


def _dot_tn(a, b):
    return lax.dot_general(a, b, (((0,), (0,)), ((), ())), preferred_element_type=F32)


def _mixer_body(x_ref, anw_ref, win_ref, qanw_ref, wuq_ref, qnw_ref, kvnw_ref, wukn_ref, wuv_ref, knw_ref,
                cq_ref, sq_ref, cr_ref, sr_ref,
                q_ref, k_ref, vt_ref, ckv_ref, kpe_ref, rq_ref, rk_ref, rv_ref, rg_ref):
    x = x_ref[...]
    xn = _rms(x, anw_ref[...], D_MODEL)
    proj = _dot(xn.astype(BF16), win_ref[...])
    lane = lax.broadcasted_iota(jnp.int32, (x.shape[0], LANES), 1)
    cq, sq, cr, sr = cq_ref[...], sq_ref[...], cr_ref[...], sr_ref[...]
    lo_q = lane < QK_NOPE + QK_ROPE // 2
    lo_r = (lane & (RET_DK - 1)) < RET_DK // 2

    def rope_q(v):
        return v * cq + jnp.where(lo_q, pltpu.roll(v, LANES - 16, 1), pltpu.roll(v, 16, 1)) * sq

    def rope_r(v):
        return v * cr + jnp.where(lo_r, pltpu.roll(v, LANES - 32, 1), pltpu.roll(v, 32, 1)) * sr

    q_lat = _rms(proj[:, 0:256], qanw_ref[...], Q_LORA)
    qf = _dot(q_lat.astype(BF16), wuq_ref[...])
    ckv = _rms(proj[:, 256:384], kvnw_ref[...], KV_LORA)
    ckv_ref[...] = ckv
    kslot = proj[:, 384:512]
    kpe_ref[...] = kslot.T[QK_NOPE:QK_HEAD, :]
    cb = ckv.astype(BF16)
    knf = _dot(cb, wukn_ref[...])
    vt_ref[...] = _dot(cb, wuv_ref[...]).T.astype(BF16)
    qnw, knw = qnw_ref[...], knw_ref[...]
    for h in range(MLA_HEADS):
        sl = slice(HEAD_PAD * h, HEAD_PAD * (h + 1))
        q_ref[:, sl] = rope_q(_rms(qf[:, sl], qnw, QK_HEAD)).astype(BF16)
        k_ref[:, sl] = rope_q(_rms(knf[:, sl] + kslot, knw, QK_HEAD)).astype(BF16)
    for p in range(4):
        sl = slice(LANES * p, LANES * (p + 1))
        rq_ref[:, sl] = rope_r(proj[:, 512 + LANES * p:512 + LANES * (p + 1)]).astype(BF16)
        rk_ref[:, sl] = (rope_r(proj[:, 1024 + LANES * p:1024 + LANES * (p + 1)]) * (RET_DK ** -0.5)).astype(BF16)
    rv_ref[...] = proj[:, 1536:2048].astype(BF16)
    rg_ref[...] = proj[:, 2048:2560]


def _mixer(x_all, wts, tabs):
    n = x_all.shape[0]
    tm = MIX_TM
    row = lambda w: pl.BlockSpec((tm, w), lambda i: (i, 0))
    ins = [row(D_MODEL), _full((1, D_MODEL)), _full((D_MODEL, 2560)), _full((1, Q_LORA)), _full((Q_LORA, 1024)),
           _full((1, LANES)), _full((1, KV_LORA)), _full((KV_LORA, 1024)), _full((KV_LORA, 512)), _full((1, LANES)),
           row(LANES), row(LANES), row(LANES), row(LANES)]
    outs = [row(1024), row(1024), pl.BlockSpec((512, tm), lambda i: (0, i)), row(KV_LORA),
            pl.BlockSpec((QK_ROPE, tm), lambda i: (0, i)), row(512),
            row(512), row(512), row(512)]
    shapes = [jax.ShapeDtypeStruct((n, 1024), BF16), jax.ShapeDtypeStruct((n, 1024), BF16),
              jax.ShapeDtypeStruct((512, n), BF16), jax.ShapeDtypeStruct((n, KV_LORA), F32),
              jax.ShapeDtypeStruct((QK_ROPE, n), F32), jax.ShapeDtypeStruct((n, 512), BF16),
              jax.ShapeDtypeStruct((n, 512), BF16), jax.ShapeDtypeStruct((n, 512), BF16),
              jax.ShapeDtypeStruct((n, 512), F32)]
    return pl.pallas_call(
        _mixer_body, grid=(n // tm,), in_specs=ins, out_specs=outs, out_shape=shapes,
        compiler_params=_cparams(("arbitrary",), 48), name="mixer",
    )(x_all, wts["anw"], wts["win"], wts["qanw"], wts["wuq"], wts["qnw"], wts["kvnw"], wts["wukn"], wts["wuv"],
      wts["knw"], tabs["cq"], tabs["sq"], tabs["cr"], tabs["sr"])


def _pattn_body(q_ref, k_ref, vt_ref, o_ref, st0, st1, pt0, pt1, acc_sc):
    qi = pl.program_id(2)
    t = ATT_T
    ch = ATT_CH
    krow = lax.broadcasted_iota(jnp.int32, (ch, t), 0)
    qcol = lax.broadcasted_iota(jnp.int32, (ch, t), 1)
    acc_sc[...] = jnp.zeros(acc_sc.shape, F32)
    st_sc, pt_sc = (st0, st1), (pt0, pt1)

    def scores(j, slot):
        start = pl.multiple_of(j * t, t)
        for hh in range(2):
            lsl = slice(HEAD_PAD * hh, HEAD_PAD * (hh + 1))
            st_sc[hh][slot] = _dot_nt(k_ref[pl.ds(start, t), lsl], q_ref[:, lsl])

    def update(j, slot, carry, masked):
        start = pl.multiple_of(j * t, t)
        out = []
        for hh in range(2):
            m_prev, l_prev = carry[2 * hh], carry[2 * hh + 1]

            def chunk(r):
                blk = st_sc[hh][slot, ch * r:ch * (r + 1), :]
                return jnp.where(krow + ch * r <= qcol, blk, NEG_INF) if masked else blk

            mc = chunk(0)
            for r in range(1, t // ch):
                mc = jnp.maximum(mc, chunk(r))
            m_new = jnp.maximum(m_prev, jnp.max(mc, 0, keepdims=True))
            alpha = jnp.exp2(m_prev - m_new)
            ls = jnp.zeros((ch, t), F32)
            for r in range(t // ch):
                p = jnp.exp2(chunk(r) - m_new)
                ls = ls + p
                pt_sc[hh][ch * r:ch * (r + 1), :] = p.astype(BF16)
            acc_sc[hh] = alpha * acc_sc[hh] + _dot(vt_ref[:, pl.ds(start, t)], pt_sc[hh][...])
            out += [m_new, alpha * l_prev + jnp.sum(ls, 0, keepdims=True)]
        return tuple(out)

    def step(j, slot, carry):
        scores(j + 1, 1 - slot)
        return update(j, slot, carry, False)

    def pair(i, carry):
        return step(2 * i + 1, 1, step(2 * i, 0, carry))

    scores(0, 0)
    init = (jnp.full((1, t), NEG_INF, F32), jnp.zeros((1, t), F32)) * 2
    carry = lax.fori_loop(0, qi // 2, pair, init)
    _, l0, _, l1 = lax.cond(qi % 2 == 1,
                            lambda c: update(qi, 1, step(qi - 1, 0, c), True),
                            lambda c: update(qi, 0, c, True), carry)
    drow = lax.broadcasted_iota(jnp.int32, (LANES, t), 0)
    o_ref[...] = jnp.where(drow < V_HEAD, acc_sc[0] / l0, acc_sc[1] / l1).T.astype(o_ref.dtype)


def _pattn(q_all, k_all, vt_all, batch, seq):
    t = ATT_T
    nq = seq // t
    return pl.pallas_call(
        _pattn_body, grid=(batch, MLA_HEADS // 2, nq),
        in_specs=[pl.BlockSpec((t, 2 * HEAD_PAD), lambda b, hp, qi: (b * nq + qi, hp)),
                  pl.BlockSpec((seq, 2 * HEAD_PAD), lambda b, hp, qi: (b, hp)),
                  pl.BlockSpec((LANES, seq), lambda b, hp, qi: (hp, b))],
        out_specs=pl.BlockSpec((t, LANES), lambda b, hp, qi: (b * nq + qi, hp)),
        out_shape=jax.ShapeDtypeStruct((batch * seq, MLA_HEADS * V_HEAD), BF16),
        scratch_shapes=[pltpu.VMEM((2, t, t), F32), pltpu.VMEM((2, t, t), F32), pltpu.VMEM((t, t), BF16),
                        pltpu.VMEM((t, t), BF16), pltpu.VMEM((2, LANES, t), F32)],
        compiler_params=_cparams(("arbitrary", "arbitrary", "arbitrary"), 48), name="pattn",
    )(q_all, k_all, vt_all)


def _ret_log_decay():
    return jnp.log1p(-jnp.power(2.0, -5.0 - jnp.arange(RET_HEADS, dtype=F32)))


def _ret_tables(c):
    lg = _ret_log_decay()
    i = jnp.arange(c, dtype=F32)
    diff = i[:, None] - i[None, :]
    dmat = jnp.where(diff >= 0, jnp.exp(lg[:, None, None] * jnp.maximum(diff, 0.0)), 0.0)
    lane_head = lambda a: jnp.repeat(a.reshape(a.shape[0], 4, 2), RET_DV, axis=2).transpose(1, 0, 2)
    cross = lane_head(jnp.exp(lg[None, :] * (i[:, None] + 1.0)))
    kdec = lane_head(jnp.exp(lg[None, :] * (c - 1.0 - i[:, None])))
    gc = lane_head(jnp.exp(lg * c)[None, :])
    return dmat[0::2], dmat[1::2], cross, kdec, gc


def _pret_body(rq_ref, rk_ref, rv_ref, de_ref, do_ref, cross_ref, kdec_ref, gc_ref, o_ref, rout_ref, r_sc):
    c = pl.program_id(1)

    @pl.when(c == 0)
    def _():
        r_sc[...] = jnp.zeros(r_sc.shape, F32)

    n = rq_ref.shape[0]
    lane = lax.broadcasted_iota(jnp.int32, (n, LANES), 1)
    even = lane < RET_DK
    rr = lax.broadcasted_iota(jnp.int32, (LANES, LANES), 0)
    cc = lax.broadcasted_iota(jnp.int32, (LANES, LANES), 1)
    same_head = (rr < RET_DK) == (cc < RET_DV)
    for p in range(4):
        sl = slice(LANES * p, LANES * (p + 1))
        q, k, v = rq_ref[:, sl], rk_ref[:, sl], rv_ref[:, sl]
        zero = jnp.zeros_like(q)
        se = _dot_nt(jnp.where(even, q, zero), k)
        so = _dot_nt(jnp.where(even, zero, q), k)
        oe = _dot((se * de_ref[p]).astype(BF16), v)
        oo = _dot((so * do_ref[p]).astype(BF16), v)
        r = r_sc[p]
        cross = _dot(q, r.astype(BF16)) * cross_ref[p]
        o_ref[:, sl] = jnp.where(even, oe, oo) + cross
        kd = (k.astype(F32) * kdec_ref[p]).astype(BF16)
        upd = _dot_tn(kd, v)
        r_sc[p] = r * gc_ref[p] + jnp.where(same_head, upd, 0.0)

    @pl.when(c == pl.num_programs(1) - 1)
    def _():
        rout_ref[0] = r_sc[...]


def _pret(rq, rk, rv, batch, seq):
    c = RET_CHUNK
    nc = seq // c
    tabs = _ret_tables(c)
    blk = pl.BlockSpec((c, 512), lambda b, i: (b * nc + i, 0))
    return pl.pallas_call(
        _pret_body, grid=(batch, nc),
        in_specs=[blk, blk, blk, _full((4, c, c)), _full((4, c, c)), _full((4, c, LANES)), _full((4, c, LANES)),
                  _full((4, 1, LANES))],
        out_specs=[blk, pl.BlockSpec((1, 4, LANES, LANES), lambda b, i: (b, 0, 0, 0))],
        out_shape=[jax.ShapeDtypeStruct((batch * seq, 512), F32),
                   jax.ShapeDtypeStruct((batch, 4, LANES, LANES), F32)],
        scratch_shapes=[pltpu.VMEM((4, LANES, LANES), F32)],
        compiler_params=_cparams(("arbitrary", "arbitrary"), 32), name="pret",
    )(rq, rk, rv, *tabs)


def _sabs_body(q_ref, knw_ref, wpe_ref, wknt_ref, qa_ref, qcs_ref):
    q = q_ref[...].astype(F32)
    qa_ref[...] = _dot((q * knw_ref[...]).astype(BF16), wknt_ref[0]).astype(BF16)
    lane = lax.broadcasted_iota(jnp.int32, q.shape, 1)
    qpe = pltpu.roll(q, LANES - QK_NOPE, 1)
    half = QK_ROPE // 2
    swapped = jnp.where(lane < half, pltpu.roll(qpe, LANES - half, 1), -pltpu.roll(qpe, half, 1))
    keep = lane < QK_ROPE
    wpe = wpe_ref[...]
    qc = jnp.where(keep, qpe * wpe, 0.0)
    qs = jnp.where(keep, swapped * wpe, 0.0)
    qcs_ref[...] = (qc + pltpu.roll(qs, QK_ROPE, 1)).astype(BF16)


def _sabs(q_s, knw, wpe, wknt):
    n = q_s.shape[0]
    blk = pl.BlockSpec((n, HEAD_PAD), lambda h: (0, h))
    shp = jax.ShapeDtypeStruct((n, MLA_HEADS * HEAD_PAD), BF16)
    return pl.pallas_call(
        _sabs_body, grid=(MLA_HEADS,),
        in_specs=[blk, _full((1, LANES)), _full((1, LANES)), pl.BlockSpec((1, LANES, KV_LORA), lambda h: (h, 0, 0))],
        out_specs=[blk, blk], out_shape=[shp, shp],
        compiler_params=_cparams(("arbitrary",), 32), name="sabs",
    )(q_s, knw, wpe, wknt)


def _sattn_body(pt_ref, qa_ref, qcs_ref, wkn_ref, cost_ref, sint_ref, cnew_ref, pnew_ref, cosn_ref, sinn_ref,
                lat_hbm, pe_hbm, pc_ref, latbuf, pebuf, cbuf, lhs_sc, xcs, m_sc, l_sc, acc_sc, sem, *, n_pages, dec_seq):
    s_id = pl.program_id(0)
    slot = s_id & 1
    nrow = qa_ref.shape[1]
    nkn = wkn_ref.shape[0]

    def page_copies(page_id, slt, i):
        cols = pl.ds(PAGE * i, PAGE)
        return (pltpu.make_async_copy(lat_hbm.at[page_id], latbuf.at[slt, cols, :], sem.at[slt, 0]),
                pltpu.make_async_copy(pe_hbm.at[page_id], pebuf.at[slt, :, cols], sem.at[slt, 1]))

    def start_sequence(seq, slt):
        for i in range(n_pages):
            for cp in page_copies(pt_ref[seq, i], slt, i):
                cp.start()

    @pl.when(s_id == 0)
    def _():
        xcs[...] = jnp.zeros(xcs.shape, BF16)
        lhs_sc[0:nkn, :] = wkn_ref[...]
        start_sequence(0, 0)

    @pl.when(s_id + 1 < pl.num_programs(0))
    def _():
        start_sequence(s_id + 1, 1 - slot)

    lhs_sc[nkn:nkn + nrow, :] = qa_ref[0]
    m_sc[...] = jnp.full(m_sc.shape, NEG_INF, F32)
    l_sc[...] = jnp.zeros(l_sc.shape, F32)
    acc_sc[...] = jnp.zeros(acc_sc.shape, F32)

    row8 = lax.broadcasted_iota(jnp.int32, (MLA_HEADS, 1), 0)

    def attend(nk, pe_t, cos_t, sin_t, mask):
        xcs[0:QK_ROPE, 0:nk] = (pe_t * cos_t).astype(BF16)
        xcs[QK_ROPE:2 * QK_ROPE, 0:nk] = (pe_t * sin_t).astype(BF16)
        pe2 = jnp.sum(pe_t * pe_t, 0, keepdims=True)
        lhs = lhs_sc[...]
        qcs = qcs_ref[0]
        parts = []
        ck = min(nk, SAMP_CHUNK)
        for c0 in range(0, nk, ck):
            c = cbuf[c0:c0 + ck, :]
            big = _dot_nt(lhs, c)
            ss = pe2[:, c0:c0 + ck] + jnp.zeros((MLA_HEADS, ck), F32)
            for h in range(MLA_HEADS):
                kh = big[QK_NOPE * h:QK_NOPE * (h + 1), :]
                ss = ss + jnp.where(row8 == h, jnp.sum(kh * kh, 0, keepdims=True), 0.0)
            rn = lax.rsqrt(ss * (1.0 / QK_HEAD) + EPS)
            sc = big[nkn:nkn + nrow, :] + _dot(qcs, xcs[:, c0:c0 + ck])
            parts.append((sc.reshape(dec_seq, MLA_HEADS, ck) * rn[None]).reshape(nrow, ck))
        s = parts[0] if len(parts) == 1 else jnp.concatenate(parts, axis=1)
        if mask is not None:
            s = jnp.where(mask, s, NEG_INF)
        m_prev = m_sc[...]
        m_new = jnp.maximum(m_prev, jnp.max(s, -1, keepdims=True))
        alpha = jnp.exp2(m_prev - m_new)
        p = jnp.exp2(s - m_new)
        l_sc[...] = alpha * l_sc[...] + jnp.sum(p, -1, keepdims=True)
        acc_sc[...] = alpha * acc_sc[...] + _dot(p.astype(BF16), cbuf[0:nk, :])
        m_sc[...] = m_new

    pltpu.make_async_copy(latbuf.at[slot], latbuf.at[slot], sem.at[slot, 0]).wait()
    pltpu.make_async_copy(pebuf.at[slot], pebuf.at[slot], sem.at[slot, 1]).wait()
    sub = SAMP_SUB
    for k0 in range(0, PAGE * n_pages, sub):
        cbuf[0:sub, :] = latbuf[slot, k0:k0 + sub, :].astype(BF16)
        attend(sub, pebuf[slot, :, k0:k0 + sub], cost_ref[:, k0:k0 + sub], sint_ref[:, k0:k0 + sub], None)
    cbuf[0:PAGE, :] = cnew_ref[0].astype(BF16)
    qtok = lax.shift_right_logical(lax.broadcasted_iota(jnp.int32, (nrow, PAGE), 0), 3)
    kcol = lax.broadcasted_iota(jnp.int32, (nrow, PAGE), 1)
    attend(PAGE, pnew_ref[0], cosn_ref[...], sinn_ref[...], jnp.logical_and(kcol <= qtok, kcol < dec_seq))
    pc_ref[0] = acc_sc[...] / l_sc[...]


def _sattn(page_table, qa, qcs, wkn_t, cos_t, sin_t, cnew, pnew_t, cosn_t, sinn_t, cache_lat, cache_pe_t, dec_seq):
    n_seq, nrow, _ = qa.shape
    n_pages = page_table.shape[1]
    past = n_pages * PAGE
    nkn = wkn_t.shape[0]
    assert MLA_HEADS == SUBLANES and nrow == dec_seq * MLA_HEADS and past % SAMP_SUB == 0
    qblk = pl.BlockSpec((1, nrow, LANES), lambda s, pt: (s, 0, 0))
    const2 = lambda shape: pl.BlockSpec(shape, lambda s, pt: (0, 0))
    hbm = pl.BlockSpec(memory_space=pl.ANY)
    in_specs = [qblk, qblk, const2((nkn, KV_LORA)), const2((QK_ROPE, past)), const2((QK_ROPE, past)),
                pl.BlockSpec((1, PAGE, KV_LORA), lambda s, pt: (s, 0, 0)),
                pl.BlockSpec((1, QK_ROPE, PAGE), lambda s, pt: (s, 0, 0)),
                const2((QK_ROPE, PAGE)), const2((QK_ROPE, PAGE)), hbm, hbm]
    gs = pltpu.PrefetchScalarGridSpec(
        num_scalar_prefetch=1, grid=(n_seq,), in_specs=in_specs,
        out_specs=pl.BlockSpec((1, nrow, LANES), lambda s, pt: (s, 0, 0)),
        scratch_shapes=[pltpu.VMEM((2, past, KV_LORA), F32), pltpu.VMEM((2, QK_ROPE, past), F32),
                        pltpu.VMEM((SAMP_SUB, LANES), BF16), pltpu.VMEM((nkn + nrow, KV_LORA), BF16),
                        pltpu.VMEM((LANES, SAMP_SUB), BF16), pltpu.VMEM((nrow, 1), F32), pltpu.VMEM((nrow, 1), F32),
                        pltpu.VMEM((nrow, LANES), F32), pltpu.SemaphoreType.DMA((2, 2))])
    return pl.pallas_call(
        functools.partial(_sattn_body, n_pages=n_pages, dec_seq=dec_seq), grid_spec=gs,
        out_shape=jax.ShapeDtypeStruct((n_seq, nrow, LANES), F32),
        compiler_params=_cparams(("arbitrary",), 48), name="sattn",
    )(page_table, qa, qcs, wkn_t, cos_t, sin_t, cnew, pnew_t, cosn_t, sinn_t, cache_lat, cache_pe_t)


def _sup_body(pc_ref, we_ref, wo_ref, o_ref):
    o_ref[...] = (_dot(pc_ref[0].astype(BF16), we_ref[0]) + _dot(pc_ref[1].astype(BF16), wo_ref[0])).astype(o_ref.dtype)


def _sup(pc_h, wve, wvo):
    n = pc_h.shape[1]
    return pl.pallas_call(
        _sup_body, grid=(MLA_HEADS // 2,),
        in_specs=[pl.BlockSpec((2, n, KV_LORA), lambda p: (p, 0, 0)),
                  pl.BlockSpec((1, KV_LORA, LANES), lambda p: (p, 0, 0)),
                  pl.BlockSpec((1, KV_LORA, LANES), lambda p: (p, 0, 0))],
        out_specs=pl.BlockSpec((n, LANES), lambda p: (0, p)),
        out_shape=jax.ShapeDtypeStruct((n, MLA_HEADS * V_HEAD), BF16),
        compiler_params=_cparams(("arbitrary",), 32), name="sup",
    )(pc_h, wve, wvo)


def _sret_body(q_ref, k_ref, v_ref, r0_ref, d_ref, cross_ref, kdec_ref, g_ref, o_ref, r_ref):
    nt = q_ref.shape[0]
    outs = []
    for i in range(nt):
        acc = jnp.zeros(v_ref.shape[1:], F32)
        for j in range(i + 1):
            sij = jnp.sum(q_ref[i] * k_ref[j], 0, keepdims=True) * d_ref[0, i * nt + j:i * nt + j + 1, :]
            acc = acc + sij * v_ref[j]
        outs.append(acc)

    def body(d, carry):
        r = r0_ref[0, d]
        row = pl.ds(d, 1)
        rn = g_ref[0] * r
        new = []
        for i in range(nt):
            new.append(carry[i] + (q_ref[i, row, :] * cross_ref[0, i:i + 1, :]) * r)
            rn = rn + (k_ref[i, row, :] * kdec_ref[0, i:i + 1, :]) * v_ref[i]
        r_ref[0, d] = rn
        return tuple(new)

    outs = lax.fori_loop(0, RET_DK, body, tuple(outs))
    for i in range(nt):
        o_ref[i] = outs[i]


def _sret(q_t, k_t, v_t, r0_t, dec_seq):
    n_seq = q_t.shape[2]
    lg = _ret_log_decay()
    i = jnp.arange(dec_seq, dtype=F32)
    lanes = jnp.ones((1, 1, n_seq), F32)
    diff = i[:, None] - i[None, :]
    dmat = jnp.where(diff >= 0, jnp.exp(lg[:, None, None] * jnp.maximum(diff, 0.0)), 0.0)
    dtab = dmat.reshape(RET_HEADS, dec_seq * dec_seq, 1) * lanes
    cross = jnp.exp(lg[:, None] * (i[None, :] + 1.0))[:, :, None] * lanes
    kdec = jnp.exp(lg[:, None] * (dec_seq - 1.0 - i[None, :]))[:, :, None] * lanes
    gdec = jnp.exp(lg * dec_seq)[:, None, None] * lanes
    blk = pl.BlockSpec((dec_seq, RET_DK, n_seq), lambda h: (0, h, 0))
    rblk = pl.BlockSpec((1, RET_DK, RET_DV, n_seq), lambda h: (h, 0, 0, 0))
    tab = lambda r: pl.BlockSpec((1, r, n_seq), lambda h: (h, 0, 0))
    return pl.pallas_call(
        _sret_body, grid=(RET_HEADS,),
        in_specs=[blk, blk, blk, rblk, tab(dec_seq * dec_seq), tab(dec_seq), tab(dec_seq), tab(1)],
        out_specs=[blk, rblk],
        out_shape=[jax.ShapeDtypeStruct((dec_seq, RET_HEADS * RET_DV, n_seq), F32),
                   jax.ShapeDtypeStruct((RET_HEADS, RET_DK, RET_DV, n_seq), F32)],
        compiler_params=_cparams(("arbitrary",), 32), name="sret",
    )(q_t, k_t, v_t, r0_t, dtab, cross, kdec, gdec)


def _post_body(x_ref, mla_ref, reto_ref, rg_ref, rnw_ref, wo1_ref, wo2_ref, fnw_ref, wpq_ref, k1_ref, k2_ref,
               h_ref, hn_ref, s1_ref, s2_ref):
    tm = x_ref.shape[0]
    lane = lax.broadcasted_iota(jnp.int32, (tm, LANES), 1)
    lo = lane < RET_DV

    def group(a):
        s_lo = jnp.sum(jnp.where(lo, a, 0.0), -1, keepdims=True)
        s_hi = jnp.sum(jnp.where(lo, 0.0, a), -1, keepdims=True)
        return jnp.where(lo, s_lo, s_hi)

    parts = []
    for p in range(4):
        sl = slice(LANES * p, LANES * (p + 1))
        o = reto_ref[:, sl]
        d = o - group(o) * (1.0 / RET_DV)
        y = d * lax.rsqrt(group(d * d) * (1.0 / RET_DV) + EPS) * rnw_ref[:, sl]
        rg = rg_ref[:, sl]
        parts.append((rg * jax.nn.sigmoid(rg) * y).astype(BF16))
    ret_out = jnp.concatenate(parts, axis=1)
    h = x_ref[...] + _dot(mla_ref[...], wo1_ref[...]) + _dot(ret_out, wo2_ref[...])
    h_ref[...] = h
    hn = _rms(h, fnw_ref[...], D_MODEL).astype(BF16)
    hn_ref[...] = hn
    pq = _dot(hn, wpq_ref[...])
    half = PEER_QDIM // 2
    for hd in range(PEER_HEADS):
        q1 = pq[:, PEER_QDIM * hd:PEER_QDIM * hd + half].astype(BF16)
        q2 = pq[:, PEER_QDIM * hd + half:PEER_QDIM * (hd + 1)].astype(BF16)
        s1_ref[hd] = _dot_nt(k1_ref[...], q1)
        s2_ref[hd] = _dot_nt(k2_ref[...], q2)


def _post(x_all, mla, reto, rg, wts):
    n = x_all.shape[0]
    tm = MIX_TM
    row = lambda w: pl.BlockSpec((tm, w), lambda i: (i, 0))
    sblk = pl.BlockSpec((PEER_HEADS, PEER_KEYS, tm), lambda i: (0, 0, i))
    half = PEER_QDIM // 2
    return pl.pallas_call(
        _post_body, grid=(n // tm,),
        in_specs=[row(D_MODEL), row(512), row(512), row(512), _full((1, 512)), _full((512, D_MODEL)),
                  _full((512, D_MODEL)), _full((1, D_MODEL)), _full((D_MODEL, PEER_HEADS * PEER_QDIM)),
                  _full((PEER_KEYS, half)), _full((PEER_KEYS, half))],
        out_specs=[row(D_MODEL), row(D_MODEL), sblk, sblk],
        out_shape=[jax.ShapeDtypeStruct((n, D_MODEL), F32), jax.ShapeDtypeStruct((n, D_MODEL), BF16),
                   jax.ShapeDtypeStruct((PEER_HEADS, PEER_KEYS, n), F32),
                   jax.ShapeDtypeStruct((PEER_HEADS, PEER_KEYS, n), F32)],
        compiler_params=_cparams(("arbitrary",), 48), name="post",
    )(x_all, mla, reto, rg, wts["rnw"], wts["wo1"], wts["wo2"], wts["fnw"], wts["wpq"], wts["k1"], wts["k2"])


def _topk_body(s1_ref, s2_ref, e1_ref, e2_ref, g_ref):
    kk = PEER_TOPK
    tt = s1_ref.shape[2]
    row16 = lax.broadcasted_iota(jnp.int32, (kk, tt), 0)

    def take_top(x):
        nrow = x.shape[0]
        row = lax.broadcasted_iota(jnp.int32, x.shape, 0)
        vals = jnp.zeros((kk, tt), F32)
        idxs = jnp.zeros((kk, tt), jnp.int32)
        for r in range(kk):
            m = jnp.max(x, axis=0, keepdims=True)
            idx = jnp.min(jnp.where(x == m, row, nrow), axis=0, keepdims=True)
            vals = jnp.where(row16 == r, m, vals)
            idxs = jnp.where(row16 == r, idx, idxs)
            x = jnp.where(row == idx, NEG_INF, x)
        return vals, idxs

    def pick(table, sel):
        return jnp.sum(jnp.where(row16 == sel, table, 0), axis=0, keepdims=True)

    for hh in range(TOPK_HEADS):
        v1, i1 = take_top(s1_ref[hh])
        v2, i2 = take_top(s2_ref[hh])
        cur = v1 + v2[0:1, :]
        ptr = jnp.zeros((kk, tt), jnp.int32)
        vs = jnp.zeros((kk, tt), F32)
        e1 = jnp.zeros((kk, tt), jnp.int32)
        e2 = jnp.zeros((kk, tt), jnp.int32)
        for r in range(kk):
            m = jnp.max(cur, axis=0, keepdims=True)
            a_sel = row16 == jnp.min(jnp.where(cur == m, row16, kk), axis=0, keepdims=True)
            b_cur = jnp.sum(jnp.where(a_sel, ptr, 0), axis=0, keepdims=True)
            vs = jnp.where(row16 == r, m, vs)
            e1 = jnp.where(row16 == r, jnp.sum(jnp.where(a_sel, i1, 0), axis=0, keepdims=True), e1)
            e2 = jnp.where(row16 == r, pick(i2, b_cur), e2)
            b_next = b_cur + 1
            head = jnp.sum(jnp.where(a_sel, v1, 0.0), axis=0, keepdims=True) + pick(v2, b_next)
            cur = jnp.where(a_sel, jnp.where(b_next < kk, head, NEG_INF), cur)
            ptr = jnp.where(a_sel, b_next, ptr)
        p = jnp.exp(vs - jnp.max(vs, axis=0, keepdims=True))
        g_ref[kk * hh:kk * (hh + 1), :] = p / jnp.sum(p, axis=0, keepdims=True)
        e1_ref[kk * hh:kk * (hh + 1), :] = e1.astype(F32)
        e2_ref[kk * hh:kk * (hh + 1), :] = e2.astype(F32)


def _topk(s1t, s2t):
    n = s1t.shape[2]
    tt = TOPK_TT
    sblk = pl.BlockSpec((TOPK_HEADS, PEER_KEYS, tt), lambda i, h: (h, 0, i))
    oblk = pl.BlockSpec((TOPK_HEADS * PEER_TOPK, tt), lambda i, h: (h, i))
    shp = jax.ShapeDtypeStruct((PEER_HEADS * PEER_TOPK, n), F32)
    return pl.pallas_call(
        _topk_body, grid=(n // tt, PEER_HEADS // TOPK_HEADS), in_specs=[sblk, sblk], out_specs=[oblk, oblk, oblk],
        out_shape=[shp, shp, shp],
        compiler_params=_cparams(("arbitrary", "arbitrary"), 32), name="topk",
    )(s1t, s2t)


def _peer_body(e1_ref, e2_ref, g_ref, hn_ref, h_ref, ut_ref, v_ref, o_ref, e1_sc, e2_sc, g_sc, stg_sc, w_sc):
    eb = pl.program_id(1)
    tn = hn_ref.shape[0]
    pack = 2 * SUBLANES

    @pl.when(eb == 0)
    def _():
        e1_sc[...] = e1_ref[...].T
        e2_sc[...] = e2_ref[...].T
        g_sc[...] = g_ref[...].T
        o_ref[...] = h_ref[...]
        iot = lax.broadcasted_iota(jnp.int32, (PEER_KEYS, LANES), 0).astype(F32)

        def body(grp, c):
            for half in range(2):
                rows = pl.ds(pl.multiple_of(grp * pack + half * SUBLANES, SUBLANES), SUBLANES)
                e1g, e2g, gg = e1_sc[rows, :], e2_sc[rows, :], g_sc[rows, :]
                for t in range(SUBLANES):
                    a = jnp.where(e1g[t:t + 1, :] == iot, 1.0, 0.0).astype(BF16)
                    b = jnp.where(e2g[t:t + 1, :] == iot, gg[t:t + 1, :], 0.0).astype(BF16)
                    stg_sc[half, pl.ds(t, PEER_KEYS, stride=SUBLANES), :] = _dot_nt(a, b)
            lo = stg_sc[0].reshape(PEER_KEYS, SUBLANES, LANES)
            hi = stg_sc[1].reshape(PEER_KEYS, SUBLANES, LANES)
            w_sc[:, pl.ds(pl.multiple_of(grp * pack, pack), pack), :] = jnp.concatenate([lo, hi], axis=1).astype(BF16)
            return c

        lax.fori_loop(0, tn // pack, body, 0)

    act = _dot(hn_ref[...], ut_ref[...])
    gl = 0.5 * act * (1.0 + lax.erf(act * INV_SQRT2))
    zs = []
    for c in range(PEER_EB // PEER_KEYS):
        wsl = w_sc[eb * (PEER_EB // PEER_KEYS) + c].astype(F32)
        zs.append((gl[:, LANES * c:LANES * (c + 1)] * wsl).astype(BF16))
    o_ref[...] += _dot(jnp.concatenate(zs, axis=1), v_ref[...])


def _peer(e1t, e2t, gt, hn, h, ut, v):
    n = hn.shape[0]
    tn = PEER_TN
    n_exp = ut.shape[1]
    jn = PEER_HEADS * PEER_TOPK
    sel = pl.BlockSpec((jn, tn), lambda i, e: (0, i))
    row = pl.BlockSpec((tn, D_MODEL), lambda i, e: (i, 0))
    wblk = pl.BlockSpec((PEER_EB, D_MODEL), lambda i, e: (e, 0))
    return pl.pallas_call(
        _peer_body, grid=(n // tn, n_exp // PEER_EB),
        in_specs=[sel, sel, sel, row, row, pl.BlockSpec((D_MODEL, PEER_EB), lambda i, e: (0, e)), wblk],
        out_specs=row, out_shape=jax.ShapeDtypeStruct((n, D_MODEL), F32),
        scratch_shapes=[pltpu.VMEM((tn, jn), F32), pltpu.VMEM((tn, jn), F32), pltpu.VMEM((tn, jn), F32),
                        pltpu.VMEM((2, PEER_KEYS * SUBLANES, LANES), F32),
                        pltpu.VMEM((PEER_KEYS, tn, LANES), BF16)],
        compiler_params=_cparams(("arbitrary", "arbitrary"), 56), name="peer",
    )(e1t, e2t, gt, hn, h, ut, v)


def _pad_last(a, n):
    return jnp.pad(a, [(0, 0)] * (a.ndim - 1) + [(0, n - a.shape[-1])])


def _layer_weights(attn_norm_w, w_in, q_a_norm_w, w_uq, q_norm_w, kv_a_norm_w, w_ukv, k_norm_w,
                   ret_norm_w, w_o, ffn_norm_w, w_peer_q, sub_keys1, sub_keys2, peer_u, peer_v):
    o = np.cumsum([0, Q_LORA, KV_LORA, QK_ROPE, 512, 512, 512, 512])
    zeros = lambda w: jnp.zeros((D_MODEL, w), F32)
    win = jnp.concatenate([w_in[:, o[0]:o[2]], zeros(QK_NOPE), w_in[:, o[2]:o[3]], zeros(LANES - QK_HEAD),
                           w_in[:, o[3]:o[7]]], axis=1)
    wuq = _pad_last(w_uq.reshape(Q_LORA, MLA_HEADS, QK_HEAD), HEAD_PAD).reshape(Q_LORA, MLA_HEADS * HEAD_PAD)
    ukv = w_ukv.reshape(KV_LORA, MLA_HEADS, QK_NOPE + V_HEAD)
    ukn, uv = ukv[:, :, :QK_NOPE], ukv[:, :, QK_NOPE:]
    knw = _pad_last(k_norm_w, LANES)[None]
    wknt = _pad_last(ukn, HEAD_PAD).transpose(1, 2, 0)
    uvp = uv.reshape(KV_LORA, 4, 2, V_HEAD)
    zv = jnp.zeros((KV_LORA, 4, V_HEAD), F32)
    wve = jnp.concatenate([uvp[:, :, 0], zv], -1).transpose(1, 0, 2)
    wvo = jnp.concatenate([zv, uvp[:, :, 1]], -1).transpose(1, 0, 2)
    return dict(
        anw=attn_norm_w[None], win=win.astype(BF16), qanw=q_a_norm_w[None], wuq=wuq.astype(BF16),
        qnw=(_pad_last(q_norm_w, LANES) * (QK_HEAD ** -0.5 * LOG2E))[None], kvnw=kv_a_norm_w[None],
        wukn=_pad_last(ukn, HEAD_PAD).reshape(KV_LORA, MLA_HEADS * HEAD_PAD).astype(BF16),
        wuv=uv.reshape(KV_LORA, MLA_HEADS * V_HEAD).astype(BF16), knw=knw,
        wpe=_pad_last(k_norm_w[QK_NOPE:], LANES)[None], wknt=wknt.astype(BF16),
        wkn_t=ukn.transpose(1, 2, 0).reshape(MLA_HEADS * QK_NOPE, KV_LORA).astype(BF16), wve=wve.astype(BF16), wvo=wvo.astype(BF16),
        rnw=ret_norm_w[None], wo1=w_o[:512].astype(BF16), wo2=w_o[512:].astype(BF16), fnw=ffn_norm_w[None],
        wpq=w_peer_q.astype(BF16), k1=sub_keys1.astype(BF16), k2=sub_keys2.astype(BF16),
        put=peer_u.T.astype(BF16), pv=peer_v.astype(BF16))


def _rope_tables(pos):
    posf = np.asarray(pos, np.float64)[:, None]
    n = posf.shape[0]

    def cos_sin(half):
        ang = posf * (ROPE_THETA ** (-np.arange(half, dtype=np.float64) / half))[None, :]
        return jnp.asarray(np.cos(ang), F32), jnp.asarray(np.sin(ang), F32)

    c, s = cos_sin(QK_ROPE // 2)
    one, zero = jnp.ones((n, QK_NOPE), F32), jnp.zeros((n, QK_NOPE), F32)
    cq = jnp.concatenate([one, c, c, one[:, :LANES - QK_HEAD]], 1)
    sq = jnp.concatenate([zero, -s, s, zero[:, :LANES - QK_HEAD]], 1)
    c2, s2 = cos_sin(RET_DK // 2)
    cr = jnp.concatenate([c2, c2, c2, c2], 1)
    sr = jnp.concatenate([-s2, s2, -s2, s2], 1)
    return dict(cq=cq, sq=sq, cr=cr, sr=sr, cos_t=jnp.concatenate([c, c], 1).T, sin_t=jnp.concatenate([s, s], 1).T)


def kernel(x_prompt, x_sample, cache_kv_latent, cache_k_rope, state_ret, page_table, attn_norm_w, w_in, q_a_norm_w, w_uq, q_norm_w, kv_a_norm_w, w_ukv, k_norm_w, ret_norm_w, w_o, ffn_norm_w, w_peer_q, peer_sub_keys1, peer_sub_keys2, peer_u, peer_v):
    batch, seq, _ = x_prompt.shape
    n_seq, dec_seq, _ = x_sample.shape
    depth = attn_norm_w.shape[0]
    past = page_table.shape[1] * PAGE
    n_p, n_s = batch * seq, n_seq * dec_seq
    nrow = MLA_HEADS * dec_seq

    pos = np.concatenate([np.tile(np.arange(seq), batch), np.tile(past + np.arange(dec_seq), n_seq)])
    tabs = _rope_tables(pos)
    past_tabs = _rope_tables(np.arange(past))
    new_tabs = _rope_tables(past + np.arange(PAGE))

    h_all = jnp.concatenate([x_prompt.reshape(n_p, D_MODEL), x_sample.reshape(n_s, D_MODEL)], 0)
    outs = [[] for _ in range(6)]
    for l in range(depth):
        wts = _layer_weights(attn_norm_w[l], w_in[l], q_a_norm_w[l], w_uq[l], q_norm_w[l], kv_a_norm_w[l], w_ukv[l],
                             k_norm_w[l], ret_norm_w[l], w_o[l], ffn_norm_w[l], w_peer_q[l], peer_sub_keys1[l],
                             peer_sub_keys2[l], peer_u[l], peer_v[l])
        q, k, vt, ckv, kpe_t, rq, rk, rv, rg = _mixer(h_all, wts, tabs)
        kpe_s = kpe_t[:, n_p:].reshape(QK_ROPE, n_seq, dec_seq)

        mla_p = _pattn(q, k, vt, batch, seq)
        reto_p, r_p = _pret(rq, rk, rv, batch, seq)
        r_p = jnp.stack([r_p[:, :, :RET_DK, :RET_DV], r_p[:, :, RET_DK:, RET_DV:]], 2).reshape(batch, RET_HEADS, RET_DK, RET_DV)

        qa, qcs = _sabs(q[n_p:], wts["knw"], wts["wpe"], wts["wknt"])
        cnew = jnp.pad(ckv[n_p:].reshape(n_seq, dec_seq, KV_LORA), ((0, 0), (0, PAGE - dec_seq), (0, 0)))
        pnew_t = jnp.pad(kpe_s.transpose(1, 0, 2), ((0, 0), (0, 0), (0, PAGE - dec_seq)))
        pc = _sattn(page_table, qa.reshape(n_seq, nrow, HEAD_PAD), qcs.reshape(n_seq, nrow, HEAD_PAD), wts["wkn_t"],
                    past_tabs["cos_t"], past_tabs["sin_t"], cnew, pnew_t, new_tabs["cos_t"], new_tabs["sin_t"],
                    cache_kv_latent[l], cache_k_rope[l].transpose(0, 2, 1), dec_seq)
        pc_h = pc.reshape(n_seq, dec_seq, MLA_HEADS, KV_LORA).transpose(2, 0, 1, 3).reshape(MLA_HEADS, n_s, KV_LORA)
        mla_s = _sup(pc_h, wts["wve"], wts["wvo"])
        seq_last = lambda a: a[n_p:].astype(F32).reshape(n_seq, dec_seq, 512).transpose(1, 2, 0)
        reto_s, r_s = _sret(seq_last(rq), seq_last(rk), seq_last(rv), state_ret[l].astype(F32).transpose(1, 2, 3, 0),
                            dec_seq)
        reto_s = reto_s.transpose(2, 0, 1).reshape(n_s, 512)
        r_s = r_s.transpose(3, 0, 1, 2)

        mla = jnp.concatenate([mla_p, mla_s], 0)
        reto = jnp.concatenate([reto_p, reto_s], 0)
        h_mid, hn, s1t, s2t = _post(h_all, mla, reto, rg, wts)
        e1t, e2t, gt = _topk(s1t, s2t)
        h_all = _peer(e1t, e2t, gt, hn, h_mid, wts["put"], wts["pv"])

        outs[0].append(ckv[:n_p].reshape(batch, seq, KV_LORA))
        outs[1].append(kpe_t[:, :n_p].reshape(QK_ROPE, batch, seq).transpose(1, 2, 0))
        outs[2].append(r_p)
        outs[3].append(ckv[n_p:].reshape(n_seq, dec_seq, KV_LORA))
        outs[4].append(kpe_s.transpose(1, 2, 0))
        outs[5].append(r_s)
    return (h_all[:n_p].reshape(batch, seq, D_MODEL), h_all[n_p:].reshape(n_seq, dec_seq, D_MODEL),
            *[jnp.stack(o) for o in outs])
```

```python
import functools

import numpy as np
import jax
import jax.numpy as jnp
from jax import lax
from jax.experimental import pallas as pl
from jax.experimental.pallas import tpu as pltpu

F32 = jnp.float32
BF16 = jnp.bfloat16

D_MODEL = 1024
PAGE = 128
MLA_HEADS = 8
QK_NOPE = 64
QK_ROPE = 32
QK_HEAD = QK_NOPE + QK_ROPE
V_HEAD = 64
Q_LORA = 256
KV_LORA = 128
RET_HEADS = 8
RET_DK = 64
RET_DV = 64
RET_CHUNK = 128
ROPE_THETA = 10000.0
PEER_KEYS = 128
PEER_HEADS = 8
PEER_QDIM = 256
PEER_TOPK = 16
EPS = 1e-6
LANES = 128
SUBLANES = 8
HEAD_PAD = 128
NEG_INF = float("-inf")
INV_SQRT2 = 0.7071067811865476
LOG2E = 1.4426950408889634

MIX_TM = 256
ATT_T = 512
ATT_CH = 32
SAMP_SUB = 2048
SAMP_CHUNK = 512
TOPK_TT = 256
TOPK_HEADS = 2
PEER_TN = 512
PEER_EB = 1024


def _cparams(sem, vmem_mb):
    return pltpu.CompilerParams(dimension_semantics=sem, vmem_limit_bytes=vmem_mb << 20)


def _full(shape):
    n = len(shape)
    return pl.BlockSpec(shape, lambda *_: (0,) * n)


def _rms(x, w, n):
    return x * lax.rsqrt(jnp.sum(x * x, -1, keepdims=True) * (1.0 / n) + EPS) * w


def _dot(a, b):
    return jnp.dot(a, b, preferred_element_type=F32)


def _dot_nt(a, b):
    return lax.dot_general(a, b, (((1,), (1,)), ((), ())), preferred_element_type=F32)


def _dot_tn(a, b):
    return lax.dot_general(a, b, (((0,), (0,)), ((), ())), preferred_element_type=F32)


def _mixer_body(x_ref, anw_ref, win_ref, qanw_ref, wuq_ref, qnw_ref, kvnw_ref, wukn_ref, wuv_ref, knw_ref,
                cq_ref, sq_ref, cr_ref, sr_ref,
                q_ref, k_ref, vt_ref, ckv_ref, kpe_ref, rq_ref, rk_ref, rv_ref, rg_ref):
    x = x_ref[...]
    xn = _rms(x, anw_ref[...], D_MODEL)
    proj = _dot(xn.astype(BF16), win_ref[...])
    lane = lax.broadcasted_iota(jnp.int32, (x.shape[0], LANES), 1)
    cq, sq, cr, sr = cq_ref[...], sq_ref[...], cr_ref[...], sr_ref[...]
    lo_q = lane < QK_NOPE + QK_ROPE // 2
    lo_r = (lane & (RET_DK - 1)) < RET_DK // 2

    def rope_q(v):
        return v * cq + jnp.where(lo_q, pltpu.roll(v, LANES - 16, 1), pltpu.roll(v, 16, 1)) * sq

    def rope_r(v):
        return v * cr + jnp.where(lo_r, pltpu.roll(v, LANES - 32, 1), pltpu.roll(v, 32, 1)) * sr

    q_lat = _rms(proj[:, 0:256], qanw_ref[...], Q_LORA)
    qf = _dot(q_lat.astype(BF16), wuq_ref[...])
    ckv = _rms(proj[:, 256:384], kvnw_ref[...], KV_LORA)
    ckv_ref[...] = ckv
    kslot = proj[:, 384:512]
    kpe_ref[...] = kslot.T[QK_NOPE:QK_HEAD, :]
    cb = ckv.astype(BF16)
    knf = _dot(cb, wukn_ref[...])
    vt_ref[...] = _dot(cb, wuv_ref[...]).T.astype(BF16)
    qnw, knw = qnw_ref[...], knw_ref[...]
    for h in range(MLA_HEADS):
        sl = slice(HEAD_PAD * h, HEAD_PAD * (h + 1))
        q_ref[:, sl] = rope_q(_rms(qf[:, sl], qnw, QK_HEAD)).astype(BF16)
        k_ref[:, sl] = rope_q(_rms(knf[:, sl] + kslot, knw, QK_HEAD)).astype(BF16)
    for p in range(4):
        sl = slice(LANES * p, LANES * (p + 1))
        rq_ref[:, sl] = rope_r(proj[:, 512 + LANES * p:512 + LANES * (p + 1)]).astype(BF16)
        rk_ref[:, sl] = (rope_r(proj[:, 1024 + LANES * p:1024 + LANES * (p + 1)]) * (RET_DK ** -0.5)).astype(BF16)
    rv_ref[...] = proj[:, 1536:2048].astype(BF16)
    rg_ref[...] = proj[:, 2048:2560]


def _mixer(x_all, wts, tabs):
    n = x_all.shape[0]
    tm = MIX_TM
    row = lambda w: pl.BlockSpec((tm, w), lambda i: (i, 0))
    ins = [row(D_MODEL), _full((1, D_MODEL)), _full((D_MODEL, 2560)), _full((1, Q_LORA)), _full((Q_LORA, 1024)),
           _full((1, LANES)), _full((1, KV_LORA)), _full((KV_LORA, 1024)), _full((KV_LORA, 512)), _full((1, LANES)),
           row(LANES), row(LANES), row(LANES), row(LANES)]
    outs = [row(1024), row(1024), pl.BlockSpec((512, tm), lambda i: (0, i)), row(KV_LORA),
            pl.BlockSpec((QK_ROPE, tm), lambda i: (0, i)), row(512),
            row(512), row(512), row(512)]
    shapes = [jax.ShapeDtypeStruct((n, 1024), BF16), jax.ShapeDtypeStruct((n, 1024), BF16),
              jax.ShapeDtypeStruct((512, n), BF16), jax.ShapeDtypeStruct((n, KV_LORA), F32),
              jax.ShapeDtypeStruct((QK_ROPE, n), F32), jax.ShapeDtypeStruct((n, 512), BF16),
              jax.ShapeDtypeStruct((n, 512), BF16), jax.ShapeDtypeStruct((n, 512), BF16),
              jax.ShapeDtypeStruct((n, 512), F32)]
    return pl.pallas_call(
        _mixer_body, grid=(n // tm,), in_specs=ins, out_specs=outs, out_shape=shapes,
        compiler_params=_cparams(("arbitrary",), 48), name="mixer",
    )(x_all, wts["anw"], wts["win"], wts["qanw"], wts["wuq"], wts["qnw"], wts["kvnw"], wts["wukn"], wts["wuv"],
      wts["knw"], tabs["cq"], tabs["sq"], tabs["cr"], tabs["sr"])


def _pattn_body(q_ref, k_ref, vt_ref, o_ref, st0, st1, pt0, pt1, acc_sc):
    qi = pl.program_id(2)
    t = ATT_T
    ch = ATT_CH
    krow = lax.broadcasted_iota(jnp.int32, (ch, t), 0)
    qcol = lax.broadcasted_iota(jnp.int32, (ch, t), 1)
    acc_sc[...] = jnp.zeros(acc_sc.shape, F32)
    st_sc, pt_sc = (st0, st1), (pt0, pt1)

    def scores(j, slot):
        start = pl.multiple_of(j * t, t)
        for hh in range(2):
            lsl = slice(HEAD_PAD * hh, HEAD_PAD * (hh + 1))
            st_sc[hh][slot] = _dot_nt(k_ref[pl.ds(start, t), lsl], q_ref[:, lsl])

    def update(j, slot, carry, masked):
        start = pl.multiple_of(j * t, t)
        out = []
        for hh in range(2):
            m_prev, l_prev = carry[2 * hh], carry[2 * hh + 1]

            def chunk(r):
                blk = st_sc[hh][slot, ch * r:ch * (r + 1), :]
                return jnp.where(krow + ch * r <= qcol, blk, NEG_INF) if masked else blk

            mc = chunk(0)
            for r in range(1, t // ch):
                mc = jnp.maximum(mc, chunk(r))
            m_new = jnp.maximum(m_prev, jnp.max(mc, 0, keepdims=True))
            alpha = jnp.exp2(m_prev - m_new)
            ls = jnp.zeros((ch, t), F32)
            for r in range(t // ch):
                p = jnp.exp2(chunk(r) - m_new)
                ls = ls + p
                pt_sc[hh][ch * r:ch * (r + 1), :] = p.astype(BF16)
            acc_sc[hh] = alpha * acc_sc[hh] + _dot(vt_ref[:, pl.ds(start, t)], pt_sc[hh][...])
            out += [m_new, alpha * l_prev + jnp.sum(ls, 0, keepdims=True)]
        return tuple(out)

    def step(j, slot, carry):
        scores(j + 1, 1 - slot)
        return update(j, slot, carry, False)

    def pair(i, carry):
        return step(2 * i + 1, 1, step(2 * i, 0, carry))

    scores(0, 0)
    init = (jnp.full((1, t), NEG_INF, F32), jnp.zeros((1, t), F32)) * 2
    carry = lax.fori_loop(0, qi // 2, pair, init)
    _, l0, _, l1 = lax.cond(qi % 2 == 1,
                            lambda c: update(qi, 1, step(qi - 1, 0, c), True),
                            lambda c: update(qi, 0, c, True), carry)
    drow = lax.broadcasted_iota(jnp.int32, (LANES, t), 0)
    o_ref[...] = jnp.where(drow < V_HEAD, acc_sc[0] / l0, acc_sc[1] / l1).T.astype(o_ref.dtype)


def _pattn(q_all, k_all, vt_all, batch, seq):
    t = ATT_T
    nq = seq // t
    return pl.pallas_call(
        _pattn_body, grid=(batch, MLA_HEADS // 2, nq),
        in_specs=[pl.BlockSpec((t, 2 * HEAD_PAD), lambda b, hp, qi: (b * nq + qi, hp)),
                  pl.BlockSpec((seq, 2 * HEAD_PAD), lambda b, hp, qi: (b, hp)),
                  pl.BlockSpec((LANES, seq), lambda b, hp, qi: (hp, b))],
        out_specs=pl.BlockSpec((t, LANES), lambda b, hp, qi: (b * nq + qi, hp)),
        out_shape=jax.ShapeDtypeStruct((batch * seq, MLA_HEADS * V_HEAD), BF16),
        scratch_shapes=[pltpu.VMEM((2, t, t), F32), pltpu.VMEM((2, t, t), F32), pltpu.VMEM((t, t), BF16),
                        pltpu.VMEM((t, t), BF16), pltpu.VMEM((2, LANES, t), F32)],
        compiler_params=_cparams(("arbitrary", "arbitrary", "arbitrary"), 48), name="pattn",
    )(q_all, k_all, vt_all)


def _ret_log_decay():
    return jnp.log1p(-jnp.power(2.0, -5.0 - jnp.arange(RET_HEADS, dtype=F32)))


def _ret_tables(c):
    lg = _ret_log_decay()
    i = jnp.arange(c, dtype=F32)
    diff = i[:, None] - i[None, :]
    dmat = jnp.where(diff >= 0, jnp.exp(lg[:, None, None] * jnp.maximum(diff, 0.0)), 0.0)
    lane_head = lambda a: jnp.repeat(a.reshape(a.shape[0], 4, 2), RET_DV, axis=2).transpose(1, 0, 2)
    cross = lane_head(jnp.exp(lg[None, :] * (i[:, None] + 1.0)))
    kdec = lane_head(jnp.exp(lg[None, :] * (c - 1.0 - i[:, None])))
    gc = lane_head(jnp.exp(lg * c)[None, :])
    return dmat[0::2], dmat[1::2], cross, kdec, gc


def _pret_body(rq_ref, rk_ref, rv_ref, de_ref, do_ref, cross_ref, kdec_ref, gc_ref, o_ref, rout_ref, r_sc):
    c = pl.program_id(1)

    @pl.when(c == 0)
    def _():
        r_sc[...] = jnp.zeros(r_sc.shape, F32)

    n = rq_ref.shape[0]
    lane = lax.broadcasted_iota(jnp.int32, (n, LANES), 1)
    even = lane < RET_DK
    rr = lax.broadcasted_iota(jnp.int32, (LANES, LANES), 0)
    cc = lax.broadcasted_iota(jnp.int32, (LANES, LANES), 1)
    same_head = (rr < RET_DK) == (cc < RET_DV)
    for p in range(4):
        sl = slice(LANES * p, LANES * (p + 1))
        q, k, v = rq_ref[:, sl], rk_ref[:, sl], rv_ref[:, sl]
        zero = jnp.zeros_like(q)
        se = _dot_nt(jnp.where(even, q, zero), k)
        so = _dot_nt(jnp.where(even, zero, q), k)
        oe = _dot((se * de_ref[p]).astype(BF16), v)
        oo = _dot((so * do_ref[p]).astype(BF16), v)
        r = r_sc[p]
        cross = _dot(q, r.astype(BF16)) * cross_ref[p]
        o_ref[:, sl] = jnp.where(even, oe, oo) + cross
        kd = (k.astype(F32) * kdec_ref[p]).astype(BF16)
        upd = _dot_tn(kd, v)
        r_sc[p] = r * gc_ref[p] + jnp.where(same_head, upd, 0.0)

    @pl.when(c == pl.num_programs(1) - 1)
    def _():
        rout_ref[0] = r_sc[...]


def _pret(rq, rk, rv, batch, seq):
    c = RET_CHUNK
    nc = seq // c
    tabs = _ret_tables(c)
    blk = pl.BlockSpec((c, 512), lambda b, i: (b * nc + i, 0))
    return pl.pallas_call(
        _pret_body, grid=(batch, nc),
        in_specs=[blk, blk, blk, _full((4, c, c)), _full((4, c, c)), _full((4, c, LANES)), _full((4, c, LANES)),
                  _full((4, 1, LANES))],
        out_specs=[blk, pl.BlockSpec((1, 4, LANES, LANES), lambda b, i: (b, 0, 0, 0))],
        out_shape=[jax.ShapeDtypeStruct((batch * seq, 512), F32),
                   jax.ShapeDtypeStruct((batch, 4, LANES, LANES), F32)],
        scratch_shapes=[pltpu.VMEM((4, LANES, LANES), F32)],
        compiler_params=_cparams(("arbitrary", "arbitrary"), 32), name="pret",
    )(rq, rk, rv, *tabs)


def _sabs_body(q_ref, knw_ref, wpe_ref, wknt_ref, qa_ref, qcs_ref):
    q = q_ref[...].astype(F32)
    qa_ref[...] = _dot((q * knw_ref[...]).astype(BF16), wknt_ref[0]).astype(BF16)
    lane = lax.broadcasted_iota(jnp.int32, q.shape, 1)
    qpe = pltpu.roll(q, LANES - QK_NOPE, 1)
    half = QK_ROPE // 2
    swapped = jnp.where(lane < half, pltpu.roll(qpe, LANES - half, 1), -pltpu.roll(qpe, half, 1))
    keep = lane < QK_ROPE
    wpe = wpe_ref[...]
    qc = jnp.where(keep, qpe * wpe, 0.0)
    qs = jnp.where(keep, swapped * wpe, 0.0)
    qcs_ref[...] = (qc + pltpu.roll(qs, QK_ROPE, 1)).astype(BF16)


def _sabs(q_s, knw, wpe, wknt):
    n = q_s.shape[0]
    blk = pl.BlockSpec((n, HEAD_PAD), lambda h: (0, h))
    shp = jax.ShapeDtypeStruct((n, MLA_HEADS * HEAD_PAD), BF16)
    return pl.pallas_call(
        _sabs_body, grid=(MLA_HEADS,),
        in_specs=[blk, _full((1, LANES)), _full((1, LANES)), pl.BlockSpec((1, LANES, KV_LORA), lambda h: (h, 0, 0))],
        out_specs=[blk, blk], out_shape=[shp, shp],
        compiler_params=_cparams(("arbitrary",), 32), name="sabs",
    )(q_s, knw, wpe, wknt)


def _sattn_body(pt_ref, qa_ref, qcs_ref, wkn_ref, cost_ref, sint_ref, cnew_ref, pnew_ref, cosn_ref, sinn_ref,
                lat_hbm, pe_hbm, pc_ref, latbuf, pebuf, cbuf, lhs_sc, xcs, m_sc, l_sc, acc_sc, sem, *, n_pages, dec_seq):
    s_id = pl.program_id(0)
    slot = s_id & 1
    nrow = qa_ref.shape[1]
    nkn = wkn_ref.shape[0]

    def page_copies(page_id, slt, i):
        cols = pl.ds(PAGE * i, PAGE)
        return (pltpu.make_async_copy(lat_hbm.at[page_id], latbuf.at[slt, cols, :], sem.at[slt, 0]),
                pltpu.make_async_copy(pe_hbm.at[page_id], pebuf.at[slt, :, cols], sem.at[slt, 1]))

    def start_sequence(seq, slt):
        for i in range(n_pages):
            for cp in page_copies(pt_ref[seq, i], slt, i):
                cp.start(priority=i % 2)

    @pl.when(s_id == 0)
    def _():
        xcs[...] = jnp.zeros(xcs.shape, BF16)
        lhs_sc[0:nkn, :] = wkn_ref[...]
        start_sequence(0, 0)

    @pl.when(s_id + 1 < pl.num_programs(0))
    def _():
        start_sequence(s_id + 1, 1 - slot)

    lhs_sc[nkn:nkn + nrow, :] = qa_ref[0]
    m_sc[...] = jnp.full(m_sc.shape, NEG_INF, F32)
    l_sc[...] = jnp.zeros(l_sc.shape, F32)
    acc_sc[...] = jnp.zeros(acc_sc.shape, F32)

    row8 = lax.broadcasted_iota(jnp.int32, (MLA_HEADS, 1), 0)

    def attend(nk, pe_t, cos_t, sin_t, mask):
        xcs[0:QK_ROPE, 0:nk] = (pe_t * cos_t).astype(BF16)
        xcs[QK_ROPE:2 * QK_ROPE, 0:nk] = (pe_t * sin_t).astype(BF16)
        pe2 = jnp.sum(pe_t * pe_t, 0, keepdims=True)
        lhs = lhs_sc[...]
        qcs = qcs_ref[0]
        parts = []
        ck = min(nk, SAMP_CHUNK)
        for c0 in range(0, nk, ck):
            c = cbuf[c0:c0 + ck, :]
            big = _dot_nt(lhs, c)
            ss = pe2[:, c0:c0 + ck] + jnp.zeros((MLA_HEADS, ck), F32)
            for h in range(MLA_HEADS):
                kh = big[QK_NOPE * h:QK_NOPE * (h + 1), :]
                ss = ss + jnp.where(row8 == h, jnp.sum(kh * kh, 0, keepdims=True), 0.0)
            rn = lax.rsqrt(ss * (1.0 / QK_HEAD) + EPS)
            sc = big[nkn:nkn + nrow, :] + _dot(qcs, xcs[:, c0:c0 + ck])
            parts.append((sc.reshape(dec_seq, MLA_HEADS, ck) * rn[None]).reshape(nrow, ck))
        s = parts[0] if len(parts) == 1 else jnp.concatenate(parts, axis=1)
        if mask is not None:
            s = jnp.where(mask, s, NEG_INF)
        m_prev = m_sc[...]
        m_new = jnp.maximum(m_prev, jnp.max(s, -1, keepdims=True))
        alpha = jnp.exp2(m_prev - m_new)
        p = jnp.exp2(s - m_new)
        l_sc[...] = alpha * l_sc[...] + jnp.sum(p, -1, keepdims=True)
        acc_sc[...] = alpha * acc_sc[...] + _dot(p.astype(BF16), cbuf[0:nk, :])
        m_sc[...] = m_new

    pltpu.make_async_copy(latbuf.at[slot], latbuf.at[slot], sem.at[slot, 0]).wait()
    pltpu.make_async_copy(pebuf.at[slot], pebuf.at[slot], sem.at[slot, 1]).wait()
    sub = SAMP_SUB
    for k0 in range(0, PAGE * n_pages, sub):
        cbuf[0:sub, :] = latbuf[slot, k0:k0 + sub, :].astype(BF16)
        attend(sub, pebuf[slot, :, k0:k0 + sub], cost_ref[:, k0:k0 + sub], sint_ref[:, k0:k0 + sub], None)
    cbuf[0:PAGE, :] = cnew_ref[0].astype(BF16)
    qtok = lax.shift_right_logical(lax.broadcasted_iota(jnp.int32, (nrow, PAGE), 0), 3)
    kcol = lax.broadcasted_iota(jnp.int32, (nrow, PAGE), 1)
    attend(PAGE, pnew_ref[0], cosn_ref[...], sinn_ref[...], jnp.logical_and(kcol <= qtok, kcol < dec_seq))
    pc_ref[0] = acc_sc[...] / l_sc[...]


def _sattn(page_table, qa, qcs, wkn_t, cos_t, sin_t, cnew, pnew_t, cosn_t, sinn_t, cache_lat, cache_pe_t, dec_seq):
    n_seq, nrow, _ = qa.shape
    n_pages = page_table.shape[1]
    past = n_pages * PAGE
    nkn = wkn_t.shape[0]
    assert MLA_HEADS == SUBLANES and nrow == dec_seq * MLA_HEADS and past % SAMP_SUB == 0
    qblk = pl.BlockSpec((1, nrow, LANES), lambda s, pt: (s, 0, 0))
    const2 = lambda shape: pl.BlockSpec(shape, lambda s, pt: (0, 0))
    hbm = pl.BlockSpec(memory_space=pl.ANY)
    in_specs = [qblk, qblk, const2((nkn, KV_LORA)), const2((QK_ROPE, past)), const2((QK_ROPE, past)),
                pl.BlockSpec((1, PAGE, KV_LORA), lambda s, pt: (s, 0, 0)),
                pl.BlockSpec((1, QK_ROPE, PAGE), lambda s, pt: (s, 0, 0)),
                const2((QK_ROPE, PAGE)), const2((QK_ROPE, PAGE)), hbm, hbm]
    gs = pltpu.PrefetchScalarGridSpec(
        num_scalar_prefetch=1, grid=(n_seq,), in_specs=in_specs,
        out_specs=pl.BlockSpec((1, nrow, LANES), lambda s, pt: (s, 0, 0)),
        scratch_shapes=[pltpu.VMEM((2, past, KV_LORA), F32), pltpu.VMEM((2, QK_ROPE, past), F32),
                        pltpu.VMEM((SAMP_SUB, LANES), BF16), pltpu.VMEM((nkn + nrow, KV_LORA), BF16),
                        pltpu.VMEM((LANES, SAMP_SUB), BF16), pltpu.VMEM((nrow, 1), F32), pltpu.VMEM((nrow, 1), F32),
                        pltpu.VMEM((nrow, LANES), F32), pltpu.SemaphoreType.DMA((2, 2))])
    return pl.pallas_call(
        functools.partial(_sattn_body, n_pages=n_pages, dec_seq=dec_seq), grid_spec=gs,
        out_shape=jax.ShapeDtypeStruct((n_seq, nrow, LANES), F32),
        compiler_params=_cparams(("arbitrary",), 48), name="sattn",
    )(page_table, qa, qcs, wkn_t, cos_t, sin_t, cnew, pnew_t, cosn_t, sinn_t, cache_lat, cache_pe_t)


def _sup_body(pc_ref, we_ref, wo_ref, o_ref):
    o_ref[...] = (_dot(pc_ref[0].astype(BF16), we_ref[0]) + _dot(pc_ref[1].astype(BF16), wo_ref[0])).astype(o_ref.dtype)


def _sup(pc_h, wve, wvo):
    n = pc_h.shape[1]
    return pl.pallas_call(
        _sup_body, grid=(MLA_HEADS // 2,),
        in_specs=[pl.BlockSpec((2, n, KV_LORA), lambda p: (p, 0, 0)),
                  pl.BlockSpec((1, KV_LORA, LANES), lambda p: (p, 0, 0)),
                  pl.BlockSpec((1, KV_LORA, LANES), lambda p: (p, 0, 0))],
        out_specs=pl.BlockSpec((n, LANES), lambda p: (0, p)),
        out_shape=jax.ShapeDtypeStruct((n, MLA_HEADS * V_HEAD), BF16),
        compiler_params=_cparams(("arbitrary",), 32), name="sup",
    )(pc_h, wve, wvo)


def _sret_body(q_ref, k_ref, v_ref, r0_ref, d_ref, cross_ref, kdec_ref, g_ref, o_ref, r_ref):
    nt = q_ref.shape[0]
    outs = []
    for i in range(nt):
        acc = jnp.zeros(v_ref.shape[1:], F32)
        for j in range(i + 1):
            sij = jnp.sum(q_ref[i] * k_ref[j], 0, keepdims=True) * d_ref[0, i * nt + j:i * nt + j + 1, :]
            acc = acc + sij * v_ref[j]
        outs.append(acc)

    def body(d, carry):
        r = r0_ref[0, d]
        row = pl.ds(d, 1)
        rn = g_ref[0] * r
        new = []
        for i in range(nt):
            new.append(carry[i] + (q_ref[i, row, :] * cross_ref[0, i:i + 1, :]) * r)
            rn = rn + (k_ref[i, row, :] * kdec_ref[0, i:i + 1, :]) * v_ref[i]
        r_ref[0, d] = rn
        return tuple(new)

    outs = lax.fori_loop(0, RET_DK, body, tuple(outs))
    for i in range(nt):
        o_ref[i] = outs[i]


def _sret(q_t, k_t, v_t, r0_t, dec_seq):
    n_seq = q_t.shape[2]
    lg = _ret_log_decay()
    i = jnp.arange(dec_seq, dtype=F32)
    lanes = jnp.ones((1, 1, n_seq), F32)
    diff = i[:, None] - i[None, :]
    dmat = jnp.where(diff >= 0, jnp.exp(lg[:, None, None] * jnp.maximum(diff, 0.0)), 0.0)
    dtab = dmat.reshape(RET_HEADS, dec_seq * dec_seq, 1) * lanes
    cross = jnp.exp(lg[:, None] * (i[None, :] + 1.0))[:, :, None] * lanes
    kdec = jnp.exp(lg[:, None] * (dec_seq - 1.0 - i[None, :]))[:, :, None] * lanes
    gdec = jnp.exp(lg * dec_seq)[:, None, None] * lanes
    blk = pl.BlockSpec((dec_seq, RET_DK, n_seq), lambda h: (0, h, 0))
    rblk = pl.BlockSpec((1, RET_DK, RET_DV, n_seq), lambda h: (h, 0, 0, 0))
    tab = lambda r: pl.BlockSpec((1, r, n_seq), lambda h: (h, 0, 0))
    return pl.pallas_call(
        _sret_body, grid=(RET_HEADS,),
        in_specs=[blk, blk, blk, rblk, tab(dec_seq * dec_seq), tab(dec_seq), tab(dec_seq), tab(1)],
        out_specs=[blk, rblk],
        out_shape=[jax.ShapeDtypeStruct((dec_seq, RET_HEADS * RET_DV, n_seq), F32),
                   jax.ShapeDtypeStruct((RET_HEADS, RET_DK, RET_DV, n_seq), F32)],
        compiler_params=_cparams(("arbitrary",), 32), name="sret",
    )(q_t, k_t, v_t, r0_t, dtab, cross, kdec, gdec)


def _post_body(x_ref, mla_ref, reto_ref, rg_ref, rnw_ref, wo1_ref, wo2_ref, fnw_ref, wpq_ref, k1_ref, k2_ref,
               h_ref, hn_ref, s1_ref, s2_ref):
    tm = x_ref.shape[0]
    lane = lax.broadcasted_iota(jnp.int32, (tm, LANES), 1)
    lo = lane < RET_DV

    def group(a):
        s_lo = jnp.sum(jnp.where(lo, a, 0.0), -1, keepdims=True)
        s_hi = jnp.sum(jnp.where(lo, 0.0, a), -1, keepdims=True)
        return jnp.where(lo, s_lo, s_hi)

    parts = []
    for p in range(4):
        sl = slice(LANES * p, LANES * (p + 1))
        o = reto_ref[:, sl]
        d = o - group(o) * (1.0 / RET_DV)
        y = d * lax.rsqrt(group(d * d) * (1.0 / RET_DV) + EPS) * rnw_ref[:, sl]
        rg = rg_ref[:, sl]
        parts.append((rg * jax.nn.sigmoid(rg) * y).astype(BF16))
    ret_out = jnp.concatenate(parts, axis=1)
    h = x_ref[...] + _dot(mla_ref[...], wo1_ref[...]) + _dot(ret_out, wo2_ref[...])
    h_ref[...] = h
    hn = _rms(h, fnw_ref[...], D_MODEL).astype(BF16)
    hn_ref[...] = hn
    pq = _dot(hn, wpq_ref[...])
    half = PEER_QDIM // 2
    for hd in range(PEER_HEADS):
        q1 = pq[:, PEER_QDIM * hd:PEER_QDIM * hd + half].astype(BF16)
        q2 = pq[:, PEER_QDIM * hd + half:PEER_QDIM * (hd + 1)].astype(BF16)
        s1_ref[hd] = _dot_nt(k1_ref[...], q1)
        s2_ref[hd] = _dot_nt(k2_ref[...], q2)


def _post(x_all, mla, reto, rg, wts):
    n = x_all.shape[0]
    tm = MIX_TM
    row = lambda w: pl.BlockSpec((tm, w), lambda i: (i, 0))
    sblk = pl.BlockSpec((PEER_HEADS, PEER_KEYS, tm), lambda i: (0, 0, i))
    half = PEER_QDIM // 2
    return pl.pallas_call(
        _post_body, grid=(n // tm,),
        in_specs=[row(D_MODEL), row(512), row(512), row(512), _full((1, 512)), _full((512, D_MODEL)),
                  _full((512, D_MODEL)), _full((1, D_MODEL)), _full((D_MODEL, PEER_HEADS * PEER_QDIM)),
                  _full((PEER_KEYS, half)), _full((PEER_KEYS, half))],
        out_specs=[row(D_MODEL), row(D_MODEL), sblk, sblk],
        out_shape=[jax.ShapeDtypeStruct((n, D_MODEL), F32), jax.ShapeDtypeStruct((n, D_MODEL), BF16),
                   jax.ShapeDtypeStruct((PEER_HEADS, PEER_KEYS, n), F32),
                   jax.ShapeDtypeStruct((PEER_HEADS, PEER_KEYS, n), F32)],
        compiler_params=_cparams(("arbitrary",), 48), name="post",
    )(x_all, mla, reto, rg, wts["rnw"], wts["wo1"], wts["wo2"], wts["fnw"], wts["wpq"], wts["k1"], wts["k2"])


def _topk_body(s1_ref, s2_ref, e1_ref, e2_ref, g_ref):
    kk = PEER_TOPK
    tt = s1_ref.shape[2]
    row16 = lax.broadcasted_iota(jnp.int32, (kk, tt), 0)

    def take_top(x):
        nrow = x.shape[0]
        row = lax.broadcasted_iota(jnp.int32, x.shape, 0)
        vals = jnp.zeros((kk, tt), F32)
        idxs = jnp.zeros((kk, tt), jnp.int32)
        for r in range(kk):
            m = jnp.max(x, axis=0, keepdims=True)
            idx = jnp.min(jnp.where(x == m, row, nrow), axis=0, keepdims=True)
            vals = jnp.where(row16 == r, m, vals)
            idxs = jnp.where(row16 == r, idx, idxs)
            x = jnp.where(row == idx, NEG_INF, x)
        return vals, idxs

    def pick(table, sel):
        return jnp.sum(jnp.where(row16 == sel, table, 0), axis=0, keepdims=True)

    for hh in range(TOPK_HEADS):
        v1, i1 = take_top(s1_ref[hh])
        v2, i2 = take_top(s2_ref[hh])
        cur = v1 + v2[0:1, :]
        ptr = jnp.zeros((kk, tt), jnp.int32)
        vs = jnp.zeros((kk, tt), F32)
        e1 = jnp.zeros((kk, tt), jnp.int32)
        e2 = jnp.zeros((kk, tt), jnp.int32)
        for r in range(kk):
            m = jnp.max(cur, axis=0, keepdims=True)
            a_sel = row16 == jnp.min(jnp.where(cur == m, row16, kk), axis=0, keepdims=True)
            b_cur = jnp.sum(jnp.where(a_sel, ptr, 0), axis=0, keepdims=True)
            vs = jnp.where(row16 == r, m, vs)
            e1 = jnp.where(row16 == r, jnp.sum(jnp.where(a_sel, i1, 0), axis=0, keepdims=True), e1)
            e2 = jnp.where(row16 == r, pick(i2, b_cur), e2)
            b_next = b_cur + 1
            head = jnp.sum(jnp.where(a_sel, v1, 0.0), axis=0, keepdims=True) + pick(v2, b_next)
            cur = jnp.where(a_sel, jnp.where(b_next < kk, head, NEG_INF), cur)
            ptr = jnp.where(a_sel, b_next, ptr)
        p = jnp.exp(vs - jnp.max(vs, axis=0, keepdims=True))
        g_ref[kk * hh:kk * (hh + 1), :] = p / jnp.sum(p, axis=0, keepdims=True)
        e1_ref[kk * hh:kk * (hh + 1), :] = e1.astype(F32)
        e2_ref[kk * hh:kk * (hh + 1), :] = e2.astype(F32)


def _topk(s1t, s2t):
    n = s1t.shape[2]
    tt = TOPK_TT
    sblk = pl.BlockSpec((TOPK_HEADS, PEER_KEYS, tt), lambda i, h: (h, 0, i))
    oblk = pl.BlockSpec((TOPK_HEADS * PEER_TOPK, tt), lambda i, h: (h, i))
    shp = jax.ShapeDtypeStruct((PEER_HEADS * PEER_TOPK, n), F32)
    return pl.pallas_call(
        _topk_body, grid=(n // tt, PEER_HEADS // TOPK_HEADS), in_specs=[sblk, sblk], out_specs=[oblk, oblk, oblk],
        out_shape=[shp, shp, shp],
        compiler_params=_cparams(("arbitrary", "arbitrary"), 32), name="topk",
    )(s1t, s2t)


def _peer_body(e1_ref, e2_ref, g_ref, hn_ref, h_ref, ut_ref, v_ref, o_ref, e1_sc, e2_sc, g_sc, stg_sc, w_sc):
    eb = pl.program_id(1)
    tn = hn_ref.shape[0]
    pack = 2 * SUBLANES

    @pl.when(eb == 0)
    def _():
        e1_sc[...] = e1_ref[...].T
        e2_sc[...] = e2_ref[...].T
        g_sc[...] = g_ref[...].T
        o_ref[...] = h_ref[...]
        iot = lax.broadcasted_iota(jnp.int32, (PEER_KEYS, LANES), 0).astype(F32)

        def body(grp, c):
            for half in range(2):
                rows = pl.ds(pl.multiple_of(grp * pack + half * SUBLANES, SUBLANES), SUBLANES)
                e1g, e2g, gg = e1_sc[rows, :], e2_sc[rows, :], g_sc[rows, :]
                for t in range(SUBLANES):
                    a = jnp.where(e1g[t:t + 1, :] == iot, 1.0, 0.0).astype(BF16)
                    b = jnp.where(e2g[t:t + 1, :] == iot, gg[t:t + 1, :], 0.0).astype(BF16)
                    stg_sc[half, pl.ds(t, PEER_KEYS, stride=SUBLANES), :] = _dot_nt(a, b)
            lo = stg_sc[0].reshape(PEER_KEYS, SUBLANES, LANES)
            hi = stg_sc[1].reshape(PEER_KEYS, SUBLANES, LANES)
            w_sc[:, pl.ds(pl.multiple_of(grp * pack, pack), pack), :] = jnp.concatenate([lo, hi], axis=1).astype(BF16)
            return c

        lax.fori_loop(0, tn // pack, body, 0)

    act = _dot(hn_ref[...], ut_ref[...])
    gl = 0.5 * act * (1.0 + lax.erf(act * INV_SQRT2))
    zs = []
    for c in range(PEER_EB // PEER_KEYS):
        wsl = w_sc[eb * (PEER_EB // PEER_KEYS) + c].astype(F32)
        zs.append((gl[:, LANES * c:LANES * (c + 1)] * wsl).astype(BF16))
    o_ref[...] += _dot(jnp.concatenate(zs, axis=1), v_ref[...])


def _peer(e1t, e2t, gt, hn, h, ut, v):
    n = hn.shape[0]
    tn = PEER_TN
    n_exp = ut.shape[1]
    jn = PEER_HEADS * PEER_TOPK
    sel = pl.BlockSpec((jn, tn), lambda i, e: (0, i))
    row = pl.BlockSpec((tn, D_MODEL), lambda i, e: (i, 0))
    wblk = pl.BlockSpec((PEER_EB, D_MODEL), lambda i, e: (e, 0))
    return pl.pallas_call(
        _peer_body, grid=(n // tn, n_exp // PEER_EB),
        in_specs=[sel, sel, sel, row, row, pl.BlockSpec((D_MODEL, PEER_EB), lambda i, e: (0, e)), wblk],
        out_specs=row, out_shape=jax.ShapeDtypeStruct((n, D_MODEL), F32),
        scratch_shapes=[pltpu.VMEM((tn, jn), F32), pltpu.VMEM((tn, jn), F32), pltpu.VMEM((tn, jn), F32),
                        pltpu.VMEM((2, PEER_KEYS * SUBLANES, LANES), F32),
                        pltpu.VMEM((PEER_KEYS, tn, LANES), BF16)],
        compiler_params=_cparams(("arbitrary", "arbitrary"), 56), name="peer",
    )(e1t, e2t, gt, hn, h, ut, v)


def _pad_last(a, n):
    return jnp.pad(a, [(0, 0)] * (a.ndim - 1) + [(0, n - a.shape[-1])])


def _layer_weights(attn_norm_w, w_in, q_a_norm_w, w_uq, q_norm_w, kv_a_norm_w, w_ukv, k_norm_w,
                   ret_norm_w, w_o, ffn_norm_w, w_peer_q, sub_keys1, sub_keys2, peer_u, peer_v):
    o = np.cumsum([0, Q_LORA, KV_LORA, QK_ROPE, 512, 512, 512, 512])
    zeros = lambda w: jnp.zeros((D_MODEL, w), F32)
    win = jnp.concatenate([w_in[:, o[0]:o[2]], zeros(QK_NOPE), w_in[:, o[2]:o[3]], zeros(LANES - QK_HEAD),
                           w_in[:, o[3]:o[7]]], axis=1)
    wuq = _pad_last(w_uq.reshape(Q_LORA, MLA_HEADS, QK_HEAD), HEAD_PAD).reshape(Q_LORA, MLA_HEADS * HEAD_PAD)
    ukv = w_ukv.reshape(KV_LORA, MLA_HEADS, QK_NOPE + V_HEAD)
    ukn, uv = ukv[:, :, :QK_NOPE], ukv[:, :, QK_NOPE:]
    knw = _pad_last(k_norm_w, LANES)[None]
    wknt = _pad_last(ukn, HEAD_PAD).transpose(1, 2, 0)
    uvp = uv.reshape(KV_LORA, 4, 2, V_HEAD)
    zv = jnp.zeros((KV_LORA, 4, V_HEAD), F32)
    wve = jnp.concatenate([uvp[:, :, 0], zv], -1).transpose(1, 0, 2)
    wvo = jnp.concatenate([zv, uvp[:, :, 1]], -1).transpose(1, 0, 2)
    return dict(
        anw=attn_norm_w[None], win=win.astype(BF16), qanw=q_a_norm_w[None], wuq=wuq.astype(BF16),
        qnw=(_pad_last(q_norm_w, LANES) * (QK_HEAD ** -0.5 * LOG2E))[None], kvnw=kv_a_norm_w[None],
        wukn=_pad_last(ukn, HEAD_PAD).reshape(KV_LORA, MLA_HEADS * HEAD_PAD).astype(BF16),
        wuv=uv.reshape(KV_LORA, MLA_HEADS * V_HEAD).astype(BF16), knw=knw,
        wpe=_pad_last(k_norm_w[QK_NOPE:], LANES)[None], wknt=wknt.astype(BF16),
        wkn_t=ukn.transpose(1, 2, 0).reshape(MLA_HEADS * QK_NOPE, KV_LORA).astype(BF16), wve=wve.astype(BF16), wvo=wvo.astype(BF16),
        rnw=ret_norm_w[None], wo1=w_o[:512].astype(BF16), wo2=w_o[512:].astype(BF16), fnw=ffn_norm_w[None],
        wpq=w_peer_q.astype(BF16), k1=sub_keys1.astype(BF16), k2=sub_keys2.astype(BF16),
        put=peer_u.T.astype(BF16), pv=peer_v.astype(BF16))


def _rope_tables(pos):
    posf = np.asarray(pos, np.float64)[:, None]
    n = posf.shape[0]

    def cos_sin(half):
        ang = posf * (ROPE_THETA ** (-np.arange(half, dtype=np.float64) / half))[None, :]
        return jnp.asarray(np.cos(ang), F32), jnp.asarray(np.sin(ang), F32)

    c, s = cos_sin(QK_ROPE // 2)
    one, zero = jnp.ones((n, QK_NOPE), F32), jnp.zeros((n, QK_NOPE), F32)
    cq = jnp.concatenate([one, c, c, one[:, :LANES - QK_HEAD]], 1)
    sq = jnp.concatenate([zero, -s, s, zero[:, :LANES - QK_HEAD]], 1)
    c2, s2 = cos_sin(RET_DK // 2)
    cr = jnp.concatenate([c2, c2, c2, c2], 1)
    sr = jnp.concatenate([-s2, s2, -s2, s2], 1)
    return dict(cq=cq, sq=sq, cr=cr, sr=sr, cos_t=jnp.concatenate([c, c], 1).T, sin_t=jnp.concatenate([s, s], 1).T)


def kernel(x_prompt, x_sample, cache_kv_latent, cache_k_rope, state_ret, page_table, attn_norm_w, w_in, q_a_norm_w, w_uq, q_norm_w, kv_a_norm_w, w_ukv, k_norm_w, ret_norm_w, w_o, ffn_norm_w, w_peer_q, peer_sub_keys1, peer_sub_keys2, peer_u, peer_v):
    batch, seq, _ = x_prompt.shape
    n_seq, dec_seq, _ = x_sample.shape
    depth = attn_norm_w.shape[0]
    past = page_table.shape[1] * PAGE
    n_p, n_s = batch * seq, n_seq * dec_seq
    nrow = MLA_HEADS * dec_seq

    pos = np.concatenate([np.tile(np.arange(seq), batch), np.tile(past + np.arange(dec_seq), n_seq)])
    tabs = _rope_tables(pos)
    past_tabs = _rope_tables(np.arange(past))
    new_tabs = _rope_tables(past + np.arange(PAGE))

    h_all = jnp.concatenate([x_prompt.reshape(n_p, D_MODEL), x_sample.reshape(n_s, D_MODEL)], 0)
    outs = [[] for _ in range(6)]
    for l in range(depth):
        wts = _layer_weights(attn_norm_w[l], w_in[l], q_a_norm_w[l], w_uq[l], q_norm_w[l], kv_a_norm_w[l], w_ukv[l],
                             k_norm_w[l], ret_norm_w[l], w_o[l], ffn_norm_w[l], w_peer_q[l], peer_sub_keys1[l],
                             peer_sub_keys2[l], peer_u[l], peer_v[l])
        q, k, vt, ckv, kpe_t, rq, rk, rv, rg = _mixer(h_all, wts, tabs)
        kpe_s = kpe_t[:, n_p:].reshape(QK_ROPE, n_seq, dec_seq)

        mla_p = _pattn(q, k, vt, batch, seq)
        reto_p, r_p = _pret(rq, rk, rv, batch, seq)
        r_p = jnp.stack([r_p[:, :, :RET_DK, :RET_DV], r_p[:, :, RET_DK:, RET_DV:]], 2).reshape(batch, RET_HEADS, RET_DK, RET_DV)

        qa, qcs = _sabs(q[n_p:], wts["knw"], wts["wpe"], wts["wknt"])
        cnew = jnp.pad(ckv[n_p:].reshape(n_seq, dec_seq, KV_LORA), ((0, 0), (0, PAGE - dec_seq), (0, 0)))
        pnew_t = jnp.pad(kpe_s.transpose(1, 0, 2), ((0, 0), (0, 0), (0, PAGE - dec_seq)))
        pc = _sattn(page_table, qa.reshape(n_seq, nrow, HEAD_PAD), qcs.reshape(n_seq, nrow, HEAD_PAD), wts["wkn_t"],
                    past_tabs["cos_t"], past_tabs["sin_t"], cnew, pnew_t, new_tabs["cos_t"], new_tabs["sin_t"],
                    cache_kv_latent[l], cache_k_rope[l].transpose(0, 2, 1), dec_seq)
        pc_h = pc.reshape(n_seq, dec_seq, MLA_HEADS, KV_LORA).transpose(2, 0, 1, 3).reshape(MLA_HEADS, n_s, KV_LORA)
        mla_s = _sup(pc_h, wts["wve"], wts["wvo"])
        seq_last = lambda a: a[n_p:].astype(F32).reshape(n_seq, dec_seq, 512).transpose(1, 2, 0)
        reto_s, r_s = _sret(seq_last(rq), seq_last(rk), seq_last(rv), state_ret[l].astype(F32).transpose(1, 2, 3, 0),
                            dec_seq)
        reto_s = reto_s.transpose(2, 0, 1).reshape(n_s, 512)
        r_s = r_s.transpose(3, 0, 1, 2)

        mla = jnp.concatenate([mla_p, mla_s], 0)
        reto = jnp.concatenate([reto_p, reto_s], 0)
        h_mid, hn, s1t, s2t = _post(h_all, mla, reto, rg, wts)
        e1t, e2t, gt = _topk(s1t, s2t)
        h_all = _peer(e1t, e2t, gt, hn, h_mid, wts["put"], wts["pv"])

        outs[0].append(ckv[:n_p].reshape(batch, seq, KV_LORA))
        outs[1].append(kpe_t[:, :n_p].reshape(QK_ROPE, batch, seq).transpose(1, 2, 0))
        outs[2].append(r_p)
        outs[3].append(ckv[n_p:].reshape(n_seq, dec_seq, KV_LORA))
        outs[4].append(kpe_s.transpose(1, 2, 0))
        outs[5].append(r_s)
    return (h_all[:n_p].reshape(batch, seq, D_MODEL), h_all[n_p:].reshape(n_seq, dec_seq, D_MODEL),
            *[jnp.stack(o) for o in outs])
```
